```python
import math
import jax, jax.numpy as jnp
from jax import lax
import numpy as np

D_MODEL = 1024
BATCH = 8
SEQ = 2048
DEPTH = 2

N_A_LAYERS = DEPTH // 2
N_B_LAYERS = DEPTH - N_A_LAYERS
CONV_WIDTH = 3
D_FF = 4 * D_MODEL
HEAD_DIM = 64
N_HEADS = D_MODEL // HEAD_DIM
N_KV_GROUPS = 4
HEADS_PER_GROUP = N_HEADS // N_KV_GROUPS
N_BRANCH = 3
CMP_BLOCK = 32
CMP_STRIDE = 16
CMP_HIDDEN = 4 * HEAD_DIM
SEL_BLOCK = 64
N_SEL = 16
WINDOW = 512
Q_BLOCK = 64
EPS = 1e-6
NEG = -1e30
BIG = 1e30

kernel_name = "yoco_shortconv_nsa_hybrid"


def rms_norm(x, gain):
    xf = x.astype(jnp.float32)
    y = xf * lax.rsqrt(jnp.mean(xf * xf, axis=-1, keepdims=True) + EPS)
    return (y * gain.astype(jnp.float32)).astype(x.dtype)


def modulate(h, shift, scale):
    return h * (1 + scale[:, None, :]) + shift[:, None, :]


def masked_softmax(s, mask):
    s = jnp.where(mask, s.astype(jnp.float32), NEG)
    p = jnp.where(mask, jnp.exp(s - jnp.max(s, axis=-1, keepdims=True)), 0.0)
    return p / jnp.maximum(jnp.sum(p, axis=-1, keepdims=True), 1e-30)


def short_conv_mixer(h, w_in, conv_w, w_out):
    gate_b, gate_c, u = jnp.split(h @ w_in, 3, axis=-1)
    v = gate_c * u
    v = lax.conv_general_dilated(
        v, conv_w[:, None, :], window_strides=(1,),
        padding=[(CONV_WIDTH - 1, 0)],
        dimension_numbers=("NWC", "WIO", "NWC"),
        feature_group_count=v.shape[-1])
    return (gate_b * v) @ w_out


def squared_relu_mlp(h, w1, w2):
    return jnp.square(jax.nn.relu(h @ w1)) @ w2


def n_compressed(seq):
    return (seq - CMP_BLOCK) // CMP_STRIDE + 1


def cmp_to_sel_matrix(seq):
    n_cmp, n_sel = n_compressed(seq), seq // SEL_BLOCK
    c0 = np.arange(n_cmp)[:, None] * CMP_STRIDE
    s0 = np.arange(n_sel)[None, :] * SEL_BLOCK
    ov = np.minimum(c0 + CMP_BLOCK, s0 + SEL_BLOCK) - np.maximum(c0, s0)
    return (np.clip(ov, 0, None) / CMP_BLOCK).astype(np.float32)


def shared_kv(h, w_kv, k_gain, cmp_pe, cmp_w1, cmp_w2):
    B, S, _ = h.shape
    kv = (h @ w_kv).reshape(B, S, 2 * N_BRANCH, N_KV_GROUPS, HEAD_DIM)
    kv = kv.transpose(2, 0, 3, 1, 4)
    kc_raw, vc_raw, ks, vs, kw, vw = kv[0], kv[1], kv[2], kv[3], kv[4], kv[5]
    n_cmp = n_compressed(S)
    idx = np.arange(n_cmp)[:, None] * CMP_STRIDE + np.arange(CMP_BLOCK)[None, :]

    def compress(t, pe, w1, w2):
        blk = t[:, :, idx, :] + pe
        blk = blk.reshape(B, N_KV_GROUPS, n_cmp, CMP_BLOCK * HEAD_DIM)
        return jax.nn.gelu(blk @ w1) @ w2

    kc = rms_norm(compress(kc_raw, cmp_pe[0], cmp_w1[0], cmp_w2[0]), k_gain[0])
    vc = compress(vc_raw, cmp_pe[1], cmp_w1[1], cmp_w2[1])
    ks = rms_norm(ks, k_gain[1])
    kw = rms_norm(kw, k_gain[2])
    pad = ((0, 0), (0, 0), (WINDOW, 0), (0, 0))
    return kc, vc, ks, vs, jnp.pad(kw, pad), jnp.pad(vw, pad)


def nsa_attention(h, w_qg, q_gain, w_o, kc, vc, ks, vs, kw, vw):
    B, S, _ = h.shape
    G, Hg = N_KV_GROUPS, HEADS_PER_GROUP
    qg = h @ w_qg
    q = qg[..., :N_HEADS * HEAD_DIM].reshape(B, S, N_HEADS, HEAD_DIM)
    q = rms_norm(q, q_gain) * (HEAD_DIM ** -0.5)
    q = q.reshape(B, S, G, Hg, HEAD_DIM).transpose(0, 2, 3, 1, 4)
    gates = jax.nn.sigmoid(qg[..., N_HEADS * HEAD_DIM:].astype(jnp.float32))
    gates = gates.reshape(B, S, G, Hg, N_BRANCH).transpose(0, 2, 3, 1, 4)

    n_cmp = kc.shape[2]
    n_sel = S // SEL_BLOCK
    n_top = min(N_SEL, n_sel)
    cmp_map = jnp.asarray(cmp_to_sel_matrix(S))
    cmp_end = jnp.arange(n_cmp) * CMP_STRIDE + CMP_BLOCK - 1
    sel_ids = jnp.arange(n_sel)
    b_ix = jnp.arange(B)[:, None, None, None]
    g_ix = jnp.arange(G)[None, :, None, None]

    def query_block(s0):
        qb = lax.dynamic_slice_in_dim(q, s0, Q_BLOCK, axis=3)
        gb = lax.dynamic_slice_in_dim(gates, s0, Q_BLOCK, axis=3)
        t = s0 + jnp.arange(Q_BLOCK)
        sc = jnp.einsum("bgkqd,bgnd->bgkqn", qb, kc)
        p_cmp = masked_softmax(sc, cmp_end[None, :] <= t[:, None])
        o_cmp = jnp.einsum("bgkqn,bgnd->bgkqd", p_cmp.astype(vc.dtype), vc)
        imp = jnp.einsum("bgkqn,nj->bgqj", p_cmp, cmp_map)
        cur = t[:, None] // SEL_BLOCK
        forced = (sel_ids[None] == 0) | (sel_ids[None] == cur) | (sel_ids[None] == cur - 1)
        causal = sel_ids[None] * SEL_BLOCK <= t[:, None]
        score = jnp.where(forced, BIG, jnp.where(causal, imp, NEG))
        _, top = lax.top_k(score, n_top)
        tok = (top[..., None] * SEL_BLOCK + jnp.arange(SEL_BLOCK)).reshape(
            B, G, Q_BLOCK, n_top * SEL_BLOCK)
        k_sel = ks[b_ix, g_ix, tok]
        v_sel = vs[b_ix, g_ix, tok]
        ss = jnp.einsum("bgkqd,bgqmd->bgkqm", qb, k_sel)
        p_sel = masked_softmax(ss, (tok <= t[:, None])[:, :, None])
        o_sel = jnp.einsum("bgkqm,bgqmd->bgkqd", p_sel.astype(v_sel.dtype), v_sel)
        k_win = lax.dynamic_slice_in_dim(kw, s0, WINDOW + Q_BLOCK, axis=2)
        v_win = lax.dynamic_slice_in_dim(vw, s0, WINDOW + Q_BLOCK, axis=2)
        kpos = s0 - WINDOW + jnp.arange(WINDOW + Q_BLOCK)
        wmask = ((kpos[None] <= t[:, None]) & (kpos[None] > t[:, None] - WINDOW)
                 & (kpos[None] >= 0))
        sw = jnp.einsum("bgkqd,bgnd->bgkqn", qb, k_win)
        p_win = masked_softmax(sw, wmask)
        o_win = jnp.einsum("bgkqn,bgnd->bgkqd", p_win.astype(v_win.dtype), v_win)
        o = gb[..., 0:1] * o_cmp + gb[..., 1:2] * o_sel + gb[..., 2:3] * o_win
        return o.astype(qb.dtype)

    starts = jnp.arange(S // Q_BLOCK) * Q_BLOCK
    out = lax.map(query_block, starts)
    out = out.transpose(1, 0, 4, 2, 3, 5).reshape(B, S, N_HEADS * HEAD_DIM)
    return out @ w_o


def setup_inputs(seed: int = 0) -> dict:
    key = jax.random.key(seed)
    ks = jax.random.split(key, 24)
    D, G = D_MODEL, N_KV_GROUPS
    nrm = jax.random.normal
    f32 = jnp.float32
    return {
        "x": nrm(ks[0], (BATCH, SEQ, D), f32),
        "c": nrm(ks[1], (BATCH, D), f32),
        "norm_gain": 1.0 + 0.1 * nrm(ks[2], (DEPTH, 2, D), f32),
        "w_ada": 0.5 * D ** -0.5 * nrm(ks[3], (DEPTH, D, 6 * D), f32),
        "b_ada": 0.01 * nrm(ks[4], (DEPTH, 6 * D), f32),
        "w_a_in": D ** -0.5 * nrm(ks[5], (N_A_LAYERS, D, 3 * D), f32),
        "conv_w": CONV_WIDTH ** -0.5 * nrm(ks[6], (N_A_LAYERS, CONV_WIDTH, D), f32),
        "w_a_out": D ** -0.5 * nrm(ks[7], (N_A_LAYERS, D, D), f32),
        "w_qg": D ** -0.5 * nrm(ks[8], (N_B_LAYERS, D, N_HEADS * HEAD_DIM + N_BRANCH * N_HEADS), f32),
        "q_gain": 1.0 + 0.1 * nrm(ks[9], (N_B_LAYERS, HEAD_DIM), f32),
        "w_o": (N_HEADS * HEAD_DIM) ** -0.5 * nrm(ks[10], (N_B_LAYERS, N_HEADS * HEAD_DIM, D), f32),
        "kv_norm_gain": 1.0 + 0.1 * nrm(ks[11], (D,), f32),
        "w_ada_kv": 0.5 * D ** -0.5 * nrm(ks[12], (D, 2 * D), f32),
        "b_ada_kv": 0.01 * nrm(ks[13], (2 * D,), f32),
        "w_kv": D ** -0.5 * nrm(ks[14], (D, 2 * N_BRANCH * G * HEAD_DIM), f32),
        "k_gain": 1.0 + 0.1 * nrm(ks[15], (N_BRANCH, HEAD_DIM), f32),
        "cmp_pe": 0.2 * nrm(ks[16], (2, CMP_BLOCK, HEAD_DIM), f32),
        "cmp_w1": (CMP_BLOCK * HEAD_DIM) ** -0.5 * nrm(ks[17], (2, CMP_BLOCK * HEAD_DIM, CMP_HIDDEN), f32),
        "cmp_w2": CMP_HIDDEN ** -0.5 * nrm(ks[18], (2, CMP_HIDDEN, HEAD_DIM), f32),
        "w_mlp1": D ** -0.5 * nrm(ks[19], (DEPTH, D, D_FF), f32),
        "w_mlp2": D_FF ** -0.5 * nrm(ks[20], (DEPTH, D_FF, D), f32),
    }


def reference(x, c, norm_gain, w_ada, b_ada, w_a_in, conv_w, w_a_out, w_qg, q_gain, w_o,
              kv_norm_gain, w_ada_kv, b_ada_kv, w_kv, k_gain, cmp_pe, cmp_w1, cmp_w2,
              w_mlp1, w_mlp2):
    c_act = jax.nn.silu(c)
    kvs = None
    for i in range(DEPTH):
        mod = c_act @ w_ada[i] + b_ada[i]
        sh1, sc1, g1, sh2, sc2, g2 = jnp.split(mod, 6, axis=-1)
        if i == N_A_LAYERS:
            sh_kv, sc_kv = jnp.split(c_act @ w_ada_kv + b_ada_kv, 2, axis=-1)
            h_kv = modulate(rms_norm(x, kv_norm_gain), sh_kv, sc_kv)
            kvs = shared_kv(h_kv, w_kv, k_gain, cmp_pe, cmp_w1, cmp_w2)
        h = modulate(rms_norm(x, norm_gain[i, 0]), sh1, sc1)
        if i < N_A_LAYERS:
            mix = short_conv_mixer(h, w_a_in[i], conv_w[i], w_a_out[i])
        else:
            j = i - N_A_LAYERS
            mix = nsa_attention(h, w_qg[j], q_gain[j], w_o[j], *kvs)
        x = x + g1[:, None, :] * mix
        h = modulate(rms_norm(x, norm_gain[i, 1]), sh2, sc2)
        x = x + g2[:, None, :] * squared_relu_mlp(h, w_mlp1[i], w_mlp2[i])
    return x
```

```python
import functools

import numpy as np
import jax
import jax.numpy as jnp
from jax import lax
from jax.experimental import pallas as pl
from jax.experimental.pallas import tpu as pltpu

D_MODEL = 1024
SEQ = 2048
DEPTH = 2
N_A_LAYERS = DEPTH // 2
CONV_WIDTH = 3
D_FF = 4 * D_MODEL
HEAD_DIM = 64
N_HEADS = D_MODEL // HEAD_DIM
N_KV_GROUPS = 4
HEADS_PER_GROUP = N_HEADS // N_KV_GROUPS
N_BRANCH = 3
CMP_BLOCK = 32
CMP_STRIDE = 16
CMP_HIDDEN = 4 * HEAD_DIM
SEL_BLOCK = 64
N_SEL = 16
WINDOW = 512
EPS = 1e-6
NEG = -1e30
BIG = 1e30

N_CMP = (SEQ - CMP_BLOCK) // CMP_STRIDE + 1
N_CMP_PAD = 128
N_SELB = SEQ // SEL_BLOCK
N_TOP = min(N_SEL, N_SELB)

F32 = jnp.float32
BF16 = jnp.bfloat16

VMEM_LIMIT_BYTES = 48 * 1024 * 1024

TM_PROJ = 512
TN_INPROJ = 512
TM_MLP = 512
TF_MLP = 512
TQ = 256
TK = 256
HALO = 8


def _cparams(sem):
    return pltpu.CompilerParams(dimension_semantics=sem, vmem_limit_bytes=VMEM_LIMIT_BYTES)


def _norm_mod(x, gain, shift, scale):
    ms = jnp.mean(x * x, axis=-1, keepdims=True)
    y = x * lax.rsqrt(ms + EPS) * gain
    return y * (1.0 + scale) + shift


def _ada_kernel(c_ref, w_ref, b_ref, o_ref):
    c = c_ref[...]
    ca = c * jax.nn.sigmoid(c)
    o_ref[...] = jnp.dot(ca.astype(BF16), w_ref[...].astype(BF16),
                         preferred_element_type=F32) + b_ref[...]


def _ada(c, w, b):
    bsz, d = c.shape
    n = w.shape[1]
    tn = 1024
    return pl.pallas_call(
        _ada_kernel,
        grid=(n // tn,),
        in_specs=[pl.BlockSpec((bsz, d), lambda j: (0, 0)),
                  pl.BlockSpec((d, tn), lambda j: (0, j)),
                  pl.BlockSpec((1, tn), lambda j: (0, j))],
        out_specs=pl.BlockSpec((bsz, tn), lambda j: (0, j)),
        out_shape=jax.ShapeDtypeStruct((bsz, n), F32),
        compiler_params=_cparams(("arbitrary",)),
        name="ada",
    )(c, w, b.reshape(1, n))


def _inproj_kernel(x_ref, gain_ref, sh_ref, sc_ref, wb_ref, wc_ref, wu_ref,
                   gb_ref, v_ref, h_ref):
    @pl.when(pl.program_id(1) == 0)
    def _():
        h_ref[...] = _norm_mod(x_ref[...], gain_ref[...], sh_ref[...], sc_ref[...]).astype(BF16)

    h = h_ref[...]
    gb = jnp.dot(h, wb_ref[...], preferred_element_type=F32)
    gc = jnp.dot(h, wc_ref[...], preferred_element_type=F32)
    u = jnp.dot(h, wu_ref[...], preferred_element_type=F32)
    gb_ref[...] = gb.astype(BF16)
    v_ref[...] = (gc * u).astype(BF16)


def _in_proj(x2d, gain, sh, sc, w_in):
    m, d = x2d.shape
    tm, tn = TM_PROJ, TN_INPROJ
    per_b = SEQ // tm
    nj = d // tn
    vec = lambda: pl.BlockSpec((None, 1, d), lambda i, j: (i // per_b, 0, 0))
    return pl.pallas_call(
        _inproj_kernel,
        grid=(m // tm, nj),
        in_specs=[pl.BlockSpec((tm, d), lambda i, j: (i, 0)),
                  pl.BlockSpec((1, d), lambda i, j: (0, 0)),
                  vec(), vec(),
                  pl.BlockSpec((d, tn), lambda i, j: (0, j)),
                  pl.BlockSpec((d, tn), lambda i, j: (0, j + nj)),
                  pl.BlockSpec((d, tn), lambda i, j: (0, j + 2 * nj))],
        out_specs=[pl.BlockSpec((tm, tn), lambda i, j: (i, j)),
                   pl.BlockSpec((tm, tn), lambda i, j: (i, j))],
        out_shape=[jax.ShapeDtypeStruct((m, d), BF16), jax.ShapeDtypeStruct((m, d), BF16)],
        scratch_shapes=[pltpu.VMEM((tm, d), BF16)],
        compiler_params=_cparams(("parallel", "arbitrary")),
        name="in_proj",
    )(x2d, gain, sh, sc, w_in, w_in, w_in)


def _convout_kernel(v_ref, halo_ref, gb_ref, cw_ref, w_ref, x_ref, g_ref, o_ref, vs_ref, *, per_b):
    tm = v_ref.shape[0]
    first = (pl.program_id(0) % per_b) == 0
    halo = halo_ref[...].astype(F32)
    vs_ref[0:HALO, :] = jnp.where(first, 0.0, halo)
    vs_ref[HALO:HALO + tm, :] = v_ref[...].astype(F32)
    cw = cw_ref[...]
    conv = (cw[2:3, :] * vs_ref[HALO:HALO + tm, :]
            + cw[1:2, :] * vs_ref[HALO - 1:HALO - 1 + tm, :]
            + cw[0:1, :] * vs_ref[HALO - 2:HALO - 2 + tm, :])
    a = (gb_ref[...].astype(F32) * conv).astype(BF16)
    mix = jnp.dot(a, w_ref[...], preferred_element_type=F32)
    o_ref[...] = x_ref[...] + g_ref[...] * mix


def _conv_out(v, gb, conv_w, w_out, x2d, gate):
    m, d = x2d.shape
    tm = TM_PROJ
    per_b = SEQ // tm
    hb = tm // HALO
    return pl.pallas_call(
        functools.partial(_convout_kernel, per_b=per_b),
        grid=(m // tm,),
        in_specs=[pl.BlockSpec((tm, d), lambda i: (i, 0)),
                  pl.BlockSpec((HALO, d), lambda i: (jnp.maximum(i * hb - 1, 0), 0)),
                  pl.BlockSpec((tm, d), lambda i: (i, 0)),
                  pl.BlockSpec((CONV_WIDTH, d), lambda i: (0, 0)),
                  pl.BlockSpec((d, d), lambda i: (0, 0)),
                  pl.BlockSpec((tm, d), lambda i: (i, 0)),
                  pl.BlockSpec((None, 1, d), lambda i: (i // per_b, 0, 0))],
        out_specs=pl.BlockSpec((tm, d), lambda i: (i, 0)),
        out_shape=jax.ShapeDtypeStruct((m, d), F32),
        scratch_shapes=[pltpu.VMEM((tm + HALO, d), F32)],
        compiler_params=_cparams(("parallel",)),
        name="conv_out",
    )(v, v, gb, conv_w, w_out, x2d, gate)


def _mlp_kernel(x_ref, gain_ref, sh_ref, sc_ref, g_ref, w1_ref, w2_ref, o_ref, h_ref, acc_ref):
    f = pl.program_id(1)

    @pl.when(f == 0)
    def _():
        h_ref[...] = _norm_mod(x_ref[...], gain_ref[...], sh_ref[...], sc_ref[...]).astype(BF16)
        acc_ref[...] = jnp.zeros_like(acc_ref)

    h1 = jnp.dot(h_ref[...], w1_ref[...], preferred_element_type=F32)
    h1 = jnp.square(jnp.maximum(h1, 0.0)).astype(BF16)
    acc_ref[...] += jnp.dot(h1, w2_ref[...], preferred_element_type=F32)

    @pl.when(f == pl.num_programs(1) - 1)
    def _():
        o_ref[...] = x_ref[...] + g_ref[...] * acc_ref[...]


def _mlp(x2d, gain, sh, sc, gate, w1, w2):
    m, d = x2d.shape
    ff = w1.shape[1]
    tm, tf = TM_MLP, TF_MLP
    per_b = SEQ // tm
    vec = lambda: pl.BlockSpec((None, 1, d), lambda i, f: (i // per_b, 0, 0))
    return pl.pallas_call(
        _mlp_kernel,
        grid=(m // tm, ff // tf),
        in_specs=[pl.BlockSpec((tm, d), lambda i, f: (i, 0)),
                  pl.BlockSpec((1, d), lambda i, f: (0, 0)),
                  vec(), vec(), vec(),
                  pl.BlockSpec((d, tf), lambda i, f: (0, f)),
                  pl.BlockSpec((tf, d), lambda i, f: (f, 0))],
        out_specs=pl.BlockSpec((tm, d), lambda i, f: (i, 0)),
        out_shape=jax.ShapeDtypeStruct((m, d), F32),
        scratch_shapes=[pltpu.VMEM((tm, d), BF16), pltpu.VMEM((tm, d), F32)],
        compiler_params=_cparams(("parallel", "arbitrary")),
        name="mlp",
    )(x2d, gain, sh, sc, gate, w1, w2)


def _qkv_kernel(x_ref, gq_ref, shq_ref, scq_ref, gkv_ref, shkv_ref, sckv_ref,
                wq_ref, wg_ref, wkv_ref, q_ref, gt_ref, kv_ref):
    x = x_ref[...]
    ms = jnp.mean(x * x, axis=-1, keepdims=True)
    y = x * lax.rsqrt(ms + EPS)
    hq = ((y * gq_ref[...]) * (1.0 + scq_ref[...]) + shq_ref[...]).astype(BF16)
    hkv = ((y * gkv_ref[...]) * (1.0 + sckv_ref[...]) + shkv_ref[...]).astype(BF16)
    q_ref[...] = jnp.dot(hq, wq_ref[...], preferred_element_type=F32).astype(BF16)
    gates = jax.nn.sigmoid(jnp.dot(hq, wg_ref[...], preferred_element_type=F32))
    per_g = HEADS_PER_GROUP * N_BRANCH
    for g in range(N_KV_GROUPS):
        gt_ref[g] = gates[:, g * per_g:g * per_g + 16]
    kv = jnp.dot(hkv, wkv_ref[...], preferred_element_type=F32)
    for r in range(2 * N_BRANCH):
        for g in range(N_KV_GROUPS):
            c0 = (r * N_KV_GROUPS + g) * HEAD_DIM
            kv_ref[r, g] = kv[:, c0:c0 + HEAD_DIM].astype(BF16)


def _qkv_proj(x3d, gq, shq, scq, gkv, shkv, sckv, wq, wg, wkv):
    bsz, s, d = x3d.shape
    tm = TM_PROJ
    nkv = wkv.shape[1]
    ngate = wg.shape[1]
    vec = lambda: pl.BlockSpec((None, 1, d), lambda b, i: (b, 0, 0))
    const = lambda shape: pl.BlockSpec(shape, lambda b, i: (0,) * len(shape))
    return pl.pallas_call(
        _qkv_kernel,
        grid=(bsz, s // tm),
        in_specs=[pl.BlockSpec((None, tm, d), lambda b, i: (b, i, 0)),
                  const((1, d)), vec(), vec(),
                  const((1, d)), vec(), vec(),
                  const((d, d)), const((d, ngate)), const((d, nkv))],
        out_specs=[pl.BlockSpec((None, tm, d), lambda b, i: (b, i, 0)),
                   pl.BlockSpec((None, N_KV_GROUPS, tm, 16), lambda b, i: (b, 0, i, 0)),
                   pl.BlockSpec((2 * N_BRANCH, None, N_KV_GROUPS, tm, HEAD_DIM),
                                lambda b, i: (0, b, 0, i, 0))],
        out_shape=[jax.ShapeDtypeStruct((bsz, s, d), BF16),
                   jax.ShapeDtypeStruct((bsz, N_KV_GROUPS, s, 16), F32),
                   jax.ShapeDtypeStruct((2 * N_BRANCH, bsz, N_KV_GROUPS, s, HEAD_DIM), BF16)],
        compiler_params=_cparams(("parallel", "parallel")),
        name="qkv_proj",
    )(x3d, gq, shq, scq, gkv, shkv, sckv, wq, wg, wkv)


def _head_rms(t, gain):
    ms = jnp.mean(t * t, axis=-1, keepdims=True)
    return t * lax.rsqrt(ms + EPS) * gain


def _kvprep_kernel(rk_ref, rv_ref, ks_ref, kw_ref, pe_ref, w1_ref, w2_ref, kg_ref,
                   kc_ref, vc_ref, ksn_ref, kwn_ref):
    half = CMP_STRIDE * HEAD_DIM
    kg = kg_ref[...]

    def compress(r_ref, idx):
        r = r_ref[...]
        w1 = w1_ref[idx]
        z_lo = jnp.dot(r, w1[0:half, :], preferred_element_type=F32)
        z_hi = jnp.dot(r, w1[half:2 * half, :], preferred_element_type=F32)
        z_hi = pltpu.roll(z_hi, N_CMP_PAD - 1, 0)
        pe_b = jnp.dot(pe_ref[idx], w1, preferred_element_type=F32)
        pre = z_lo + z_hi + pe_b[0:1, :]
        hid = jax.nn.gelu(pre).astype(BF16)
        return jnp.dot(hid, w2_ref[idx], preferred_element_type=F32)

    kc_ref[...] = _head_rms(compress(rk_ref, 0), kg[0:1, :]).astype(BF16)
    vc_ref[...] = compress(rv_ref, 1).astype(BF16)
    ksn_ref[...] = _head_rms(ks_ref[...].astype(F32), kg[1:2, :]).astype(BF16)
    kwn_ref[...] = _head_rms(kw_ref[...].astype(F32), kg[2:3, :]).astype(BF16)


def _kv_prep(kv6, kv6_rows, pe8, w1, w2, k_gain):
    _, bsz, ng, s, hd = kv6.shape
    rows = kv6_rows.shape[3]
    feat = kv6_rows.shape[4]
    sel = lambda r: pl.BlockSpec((None, None, None, s, hd), lambda b, g: (r, b, g, 0, 0))
    selr = lambda r: pl.BlockSpec((None, None, None, rows, feat), lambda b, g: (r, b, g, 0, 0))
    const = lambda shape: pl.BlockSpec(shape, lambda b, g: (0,) * len(shape))
    out = lambda n, w: pl.BlockSpec((None, None, n, w), lambda b, g: (b, g, 0, 0))
    return pl.pallas_call(
        _kvprep_kernel,
        grid=(bsz, ng),
        in_specs=[selr(0), selr(1), sel(2), sel(4),
                  const(pe8.shape), const(w1.shape), const(w2.shape), const(k_gain.shape)],
        out_specs=[out(N_CMP_PAD, hd), out(N_CMP_PAD, hd), out(s, hd), out(s, hd)],
        out_shape=[jax.ShapeDtypeStruct((bsz, ng, N_CMP_PAD, hd), BF16),
                   jax.ShapeDtypeStruct((bsz, ng, N_CMP_PAD, hd), BF16),
                   jax.ShapeDtypeStruct((bsz, ng, s, hd), BF16),
                   jax.ShapeDtypeStruct((bsz, ng, s, hd), BF16)],
        compiler_params=_cparams(("parallel", "parallel")),
        name="kv_prep",
    )(kv6_rows, kv6_rows, kv6, kv6, pe8, w1, w2, k_gain)


_NT = (((1,), (1,)), ((), ()))


def _nsa_kernel(q_ref, gt_ref, qg_ref, kc_ref, vc_ref, cmap_ref, eblk_ref,
                ks_ref, vs_ref, kw_ref, vw_ref, o_ref, m_ref, l_ref, acc_ref):
    i = pl.program_id(2)
    t0 = i * TQ
    hg = HEADS_PER_GROUP
    rows = hg * TQ

    qgain = qg_ref[...]
    heads = []
    for k in range(hg):
        qk = q_ref[:, k * HEAD_DIM:(k + 1) * HEAD_DIM].astype(F32)
        heads.append(_head_rms(qk, qgain) * (HEAD_DIM ** -0.5))
    qs = jnp.concatenate(heads, axis=0).astype(BF16)

    row_t = t0 + (lax.broadcasted_iota(jnp.int32, (rows, 1), 0) & (TQ - 1))

    sc = lax.dot_general(qs, kc_ref[...], _NT, preferred_element_type=F32)
    n_id = lax.broadcasted_iota(jnp.int32, (rows, N_CMP_PAD), 1)
    valid = (n_id * CMP_STRIDE + (CMP_BLOCK - 1)) <= row_t
    sc = jnp.where(valid, sc, NEG)
    mc = jnp.max(sc, axis=-1, keepdims=True)
    pc = jnp.where(valid, jnp.exp(sc - mc), 0.0)
    pc = pc / jnp.maximum(jnp.sum(pc, axis=-1, keepdims=True), 1e-30)
    o_cmp = jnp.dot(pc.astype(BF16), vc_ref[...], preferred_element_type=F32)

    psum = pc[0:TQ]
    for k in range(1, hg):
        psum = psum + pc[k * TQ:(k + 1) * TQ]
    imp = jnp.dot(psum, cmap_ref[...], preferred_element_type=F32,
                  precision=lax.Precision.HIGHEST)
    tq = t0 + lax.broadcasted_iota(jnp.int32, (TQ, N_SELB), 0)
    j_id = lax.broadcasted_iota(jnp.int32, (TQ, N_SELB), 1)
    cur = tq // SEL_BLOCK
    forced = (j_id == 0) | (j_id == cur) | (j_id == cur - 1)
    causal = j_id <= cur
    score = jnp.where(forced, BIG, jnp.where(causal, imp, NEG))
    rank = jnp.zeros((TQ, N_SELB), jnp.int32)
    for jp in range(N_SELB):
        col = score[:, jp:jp + 1]
        beats = (col > score) | ((col == score) & (j_id > jp))
        rank = rank + beats.astype(jnp.int32)
    chosen = (rank < N_TOP) & causal
    sel_bias = jnp.where(chosen, 0.0, NEG).astype(BF16)

    def flash_step(k_ref, v_ref, c, bias, mask):
        off = pl.multiple_of(c * TK, TK)
        k = k_ref[pl.ds(off, TK), :]
        v = v_ref[pl.ds(off, TK), :]
        s = lax.dot_general(qs, k, _NT, preferred_element_type=F32)
        if bias is not None:
            s = s + jnp.concatenate([bias] * hg, axis=0)
        if mask is not None:
            s = jnp.where(mask, s, NEG)
        m_old = m_ref[...]
        m_new = jnp.maximum(m_old, jnp.max(s, axis=-1, keepdims=True))
        alpha = jnp.exp(m_old - m_new)
        p = jnp.exp(s - m_new)
        if mask is not None:
            p = jnp.where(mask, p, 0.0)
        l_ref[...] = alpha * l_ref[...] + jnp.sum(p, axis=-1, keepdims=True)
        acc_ref[...] = alpha * acc_ref[...] + jnp.dot(p.astype(BF16), v, preferred_element_type=F32)
        m_ref[...] = m_new

    def reset():
        m_ref[...] = jnp.full(m_ref.shape, NEG, F32)
        l_ref[...] = jnp.zeros(l_ref.shape, F32)
        acc_ref[...] = jnp.zeros(acc_ref.shape, F32)

    def finish():
        return acc_ref[...] / jnp.maximum(l_ref[...], 1e-30)

    col_id = lax.broadcasted_iota(jnp.int32, (rows, TK), 1)

    reset()

    def sel_body(c, carry):
        bias = jnp.dot(sel_bias, eblk_ref[c], preferred_element_type=F32)
        flash_step(ks_ref, vs_ref, c, bias, None)
        return carry

    lax.fori_loop(0, i, sel_body, 0)
    bias_d = jnp.dot(sel_bias, eblk_ref[i], preferred_element_type=F32)
    flash_step(ks_ref, vs_ref, i, bias_d, (t0 + col_id) <= row_t)
    o_sel = finish()

    reset()
    n_back = WINDOW // TK
    for d in range(n_back, -1, -1):
        @pl.when(i >= d)
        def _():
            kpos = (i - d) * TK + col_id
            flash_step(kw_ref, vw_ref, i - d, None, (kpos <= row_t) & (kpos > row_t - WINDOW))
    o_win = finish()

    gt = gt_ref[...]
    for k in range(hg):
        sl = slice(k * TQ, (k + 1) * TQ)
        o_k = (gt[:, 3 * k:3 * k + 1] * o_cmp[sl]
               + gt[:, 3 * k + 1:3 * k + 2] * o_sel[sl]
               + gt[:, 3 * k + 2:3 * k + 3] * o_win[sl])
        o_ref[:, k * HEAD_DIM:(k + 1) * HEAD_DIM] = o_k.astype(BF16)


def _nsa(q, gates, q_gain, kc, vc, cmap, eblk, ksn, kwn, kv6):
    bsz, s, d = q.shape
    ng = N_KV_GROUPS
    gw = HEADS_PER_GROUP * HEAD_DIM
    rows = HEADS_PER_GROUP * TQ
    per_bg = lambda n, w: pl.BlockSpec((None, None, n, w), lambda b, g, i: (b, g, 0, 0))
    raw = lambda r: pl.BlockSpec((None, None, None, s, HEAD_DIM), lambda b, g, i: (r, b, g, 0, 0))
    const = lambda shape: pl.BlockSpec(shape, lambda b, g, i: (0,) * len(shape))
    return pl.pallas_call(
        _nsa_kernel,
        grid=(bsz, ng, s // TQ),
        in_specs=[pl.BlockSpec((None, TQ, gw), lambda b, g, i: (b, i, g)),
                  pl.BlockSpec((None, None, TQ, 16), lambda b, g, i: (b, g, i, 0)),
                  const(q_gain.shape),
                  per_bg(N_CMP_PAD, HEAD_DIM), per_bg(N_CMP_PAD, HEAD_DIM),
                  const(cmap.shape), const(eblk.shape),
                  per_bg(s, HEAD_DIM), raw(3), per_bg(s, HEAD_DIM), raw(5)],
        out_specs=pl.BlockSpec((None, TQ, gw), lambda b, g, i: (b, i, g)),
        out_shape=jax.ShapeDtypeStruct((bsz, s, d), BF16),
        scratch_shapes=[pltpu.VMEM((rows, 1), F32), pltpu.VMEM((rows, 1), F32),
                        pltpu.VMEM((rows, HEAD_DIM), F32)],
        compiler_params=_cparams(("parallel", "parallel", "arbitrary")),
        name="nsa",
    )(q, gates, q_gain, kc, vc, cmap, eblk, ksn, kv6, kwn, kv6)


def _outproj_kernel(a_ref, w_ref, x_ref, g_ref, o_ref):
    mix = jnp.dot(a_ref[...], w_ref[...], preferred_element_type=F32)
    o_ref[...] = x_ref[...] + g_ref[...] * mix


def _out_proj(a2d, w, x2d, gate):
    m, d = x2d.shape
    tm = TM_PROJ
    per_b = SEQ // tm
    return pl.pallas_call(
        _outproj_kernel,
        grid=(m // tm,),
        in_specs=[pl.BlockSpec((tm, d), lambda i: (i, 0)),
                  pl.BlockSpec((d, d), lambda i: (0, 0)),
                  pl.BlockSpec((tm, d), lambda i: (i, 0)),
                  pl.BlockSpec((None, 1, d), lambda i: (i // per_b, 0, 0))],
        out_specs=pl.BlockSpec((tm, d), lambda i: (i, 0)),
        out_shape=jax.ShapeDtypeStruct((m, d), F32),
        compiler_params=_cparams(("parallel",)),
        name="out_proj",
    )(a2d, w, x2d, gate)


def _cmp_to_sel_padded():
    c0 = np.arange(N_CMP_PAD)[:, None] * CMP_STRIDE
    s0 = np.arange(N_SELB)[None, :] * SEL_BLOCK
    ov = np.minimum(c0 + CMP_BLOCK, s0 + SEL_BLOCK) - np.maximum(c0, s0)
    m = (np.clip(ov, 0, None) / CMP_BLOCK).astype(np.float32)
    m[N_CMP:] = 0.0
    return m


def _block_onehot():
    tok_blk = np.arange(SEQ) // SEL_BLOCK
    e = (np.arange(N_SELB)[:, None] == tok_blk[None, :]).astype(np.float32)
    return e.reshape(N_SELB, SEQ // TK, TK).transpose(1, 0, 2)


def kernel(x, c, norm_gain, w_ada, b_ada, w_a_in, conv_w, w_a_out, w_qg, q_gain, w_o,
           kv_norm_gain, w_ada_kv, b_ada_kv, w_kv, k_gain, cmp_pe, cmp_w1, cmp_w2,
           w_mlp1, w_mlp2):
    bsz, s, d = x.shape
    assert (s, d) == (SEQ, D_MODEL)
    m = bsz * s
    x2d = x.reshape(m, d)

    def split_mod(mod, n):
        return [mod[:, k * d:(k + 1) * d].reshape(bsz, 1, d) for k in range(n)]

    sh1, sc1, g1, sh2, sc2, g2 = split_mod(_ada(c, w_ada[0], b_ada[0]), 6)
    gb, v = _in_proj(x2d, norm_gain[0, 0].reshape(1, d), sh1, sc1, w_a_in[0].astype(BF16))
    x2d = _conv_out(v, gb, conv_w[0], w_a_out[0].astype(BF16), x2d, g1)
    x2d = _mlp(x2d, norm_gain[0, 1].reshape(1, d), sh2, sc2, g2,
               w_mlp1[0].astype(BF16), w_mlp2[0].astype(BF16))

    sh1, sc1, g1, sh2, sc2, g2 = split_mod(_ada(c, w_ada[1], b_ada[1]), 6)
    sh_kv, sc_kv = split_mod(_ada(c, w_ada_kv, b_ada_kv), 2)
    nq = N_HEADS * HEAD_DIM
    wq = w_qg[0][:, :nq].astype(BF16)
    ngate = N_HEADS * N_BRANCH
    wg = jnp.pad(w_qg[0][:, nq:], ((0, 0), (0, 128 - ngate))).astype(BF16)
    q, gates, kv6 = _qkv_proj(x2d.reshape(bsz, s, d), norm_gain[1, 0].reshape(1, d), sh1, sc1,
                              kv_norm_gain.reshape(1, d), sh_kv, sc_kv, wq, wg, w_kv.astype(BF16))
    kv6_rows = kv6.reshape(2 * N_BRANCH, bsz, N_KV_GROUPS, s // CMP_STRIDE, CMP_STRIDE * HEAD_DIM)
    pe8 = jnp.broadcast_to(cmp_pe.reshape(2, 1, CMP_BLOCK * HEAD_DIM), (2, 8, CMP_BLOCK * HEAD_DIM))
    kc, vc, ksn, kwn = _kv_prep(kv6, kv6_rows, pe8.astype(BF16), cmp_w1.astype(BF16),
                                cmp_w2.astype(BF16), k_gain)
    attn = _nsa(q, gates, q_gain[0].reshape(1, HEAD_DIM), kc, vc,
                jnp.asarray(_cmp_to_sel_padded()), jnp.asarray(_block_onehot(), dtype=BF16),
                ksn, kwn, kv6)
    x2d = _out_proj(attn.reshape(m, d), w_o[0].astype(BF16), x2d, g1)
    x2d = _mlp(x2d, norm_gain[1, 1].reshape(1, d), sh2, sc2, g2,
               w_mlp1[1].astype(BF16), w_mlp2[1].astype(BF16))
    return x2d.reshape(bsz, s, d)
```

```python
import functools

import numpy as np
import jax
import jax.numpy as jnp
from jax import lax
from jax.experimental import pallas as pl
from jax.experimental.pallas import tpu as pltpu

D_MODEL = 1024
SEQ = 2048
DEPTH = 2
N_A_LAYERS = DEPTH // 2
CONV_WIDTH = 3
D_FF = 4 * D_MODEL
HEAD_DIM = 64
N_HEADS = D_MODEL // HEAD_DIM
N_KV_GROUPS = 4
HEADS_PER_GROUP = N_HEADS // N_KV_GROUPS
N_BRANCH = 3
CMP_BLOCK = 32
CMP_STRIDE = 16
CMP_HIDDEN = 4 * HEAD_DIM
SEL_BLOCK = 64
N_SEL = 16
WINDOW = 512
EPS = 1e-6
NEG = -1e30
BIG = 1e30

N_CMP = (SEQ - CMP_BLOCK) // CMP_STRIDE + 1
N_CMP_PAD = 128
N_SELB = SEQ // SEL_BLOCK
N_TOP = min(N_SEL, N_SELB)

F32 = jnp.float32
BF16 = jnp.bfloat16

VMEM_LIMIT_BYTES = 48 * 1024 * 1024

TM_PROJ = 512
TN_INPROJ = 512
TM_MLP = 512
TF_MLP = 512
TQ = 256
TK = 256
HALO = 8
GATE_ROWS = 16
V_ROWS = HEAD_DIM + 16
K_AUG = 128

_NT = (((1,), (1,)), ((), ()))


def _cparams(sem):
    return pltpu.CompilerParams(dimension_semantics=sem, vmem_limit_bytes=VMEM_LIMIT_BYTES)


def _norm_mod(x, gain, shift, scale):
    ms = jnp.mean(x * x, axis=-1, keepdims=True)
    y = x * lax.rsqrt(ms + EPS) * gain
    return y * (1.0 + scale) + shift


def _ada_kernel(c_ref, w_ref, b_ref, o_ref):
    c = c_ref[...]
    ca = c * jax.nn.sigmoid(c)
    o_ref[...] = jnp.dot(ca.astype(BF16), w_ref[...].astype(BF16),
                         preferred_element_type=F32) + b_ref[...]


def _ada(c, w, b):
    bsz, d = c.shape
    n = w.shape[1]
    tn = 1024
    return pl.pallas_call(
        _ada_kernel,
        grid=(n // tn,),
        in_specs=[pl.BlockSpec((bsz, d), lambda j: (0, 0)),
                  pl.BlockSpec((d, tn), lambda j: (0, j)),
                  pl.BlockSpec((1, tn), lambda j: (0, j))],
        out_specs=pl.BlockSpec((bsz, tn), lambda j: (0, j)),
        out_shape=jax.ShapeDtypeStruct((bsz, n), F32),
        compiler_params=_cparams(("arbitrary",)),
        name="ada",
    )(c, w, b.reshape(1, n))


def _inproj_kernel(x_ref, gain_ref, sh_ref, sc_ref, wb_ref, wc_ref, wu_ref,
                   gb_ref, v_ref, h_ref):
    @pl.when(pl.program_id(1) == 0)
    def _():
        h_ref[...] = _norm_mod(x_ref[...], gain_ref[...], sh_ref[...], sc_ref[...]).astype(BF16)

    h = h_ref[...]
    gb = jnp.dot(h, wb_ref[...], preferred_element_type=F32)
    gc = jnp.dot(h, wc_ref[...], preferred_element_type=F32)
    u = jnp.dot(h, wu_ref[...], preferred_element_type=F32)
    gb_ref[...] = gb.astype(BF16)
    v_ref[...] = (gc * u).astype(BF16)


def _in_proj(x2d, gain, sh, sc, w_in):
    m, d = x2d.shape
    tm, tn = TM_PROJ, TN_INPROJ
    per_b = SEQ // tm
    nj = d // tn
    vec = lambda: pl.BlockSpec((None, 1, d), lambda i, j: (i // per_b, 0, 0))
    return pl.pallas_call(
        _inproj_kernel,
        grid=(m // tm, nj),
        in_specs=[pl.BlockSpec((tm, d), lambda i, j: (i, 0)),
                  pl.BlockSpec((1, d), lambda i, j: (0, 0)),
                  vec(), vec(),
                  pl.BlockSpec((d, tn), lambda i, j: (0, j)),
                  pl.BlockSpec((d, tn), lambda i, j: (0, j + nj)),
                  pl.BlockSpec((d, tn), lambda i, j: (0, j + 2 * nj))],
        out_specs=[pl.BlockSpec((tm, tn), lambda i, j: (i, j)),
                   pl.BlockSpec((tm, tn), lambda i, j: (i, j))],
        out_shape=[jax.ShapeDtypeStruct((m, d), BF16), jax.ShapeDtypeStruct((m, d), BF16)],
        scratch_shapes=[pltpu.VMEM((tm, d), BF16)],
        compiler_params=_cparams(("parallel", "arbitrary")),
        name="in_proj",
    )(x2d, gain, sh, sc, w_in, w_in, w_in)


def _convout_kernel(v_ref, halo_ref, gb_ref, cw_ref, w_ref, x_ref, g_ref, o_ref, vs_ref, *, per_b):
    tm = v_ref.shape[0]
    first = (pl.program_id(0) % per_b) == 0
    halo = halo_ref[...].astype(F32)
    vs_ref[0:HALO, :] = jnp.where(first, 0.0, halo)
    vs_ref[HALO:HALO + tm, :] = v_ref[...].astype(F32)
    cw = cw_ref[...]
    conv = (cw[2:3, :] * vs_ref[HALO:HALO + tm, :]
            + cw[1:2, :] * vs_ref[HALO - 1:HALO - 1 + tm, :]
            + cw[0:1, :] * vs_ref[HALO - 2:HALO - 2 + tm, :])
    a = (gb_ref[...].astype(F32) * conv).astype(BF16)
    mix = jnp.dot(a, w_ref[...], preferred_element_type=F32)
    o_ref[...] = x_ref[...] + g_ref[...] * mix


def _conv_out(v, gb, conv_w, w_out, x2d, gate):
    m, d = x2d.shape
    tm = TM_PROJ
    per_b = SEQ // tm
    hb = tm // HALO
    return pl.pallas_call(
        functools.partial(_convout_kernel, per_b=per_b),
        grid=(m // tm,),
        in_specs=[pl.BlockSpec((tm, d), lambda i: (i, 0)),
                  pl.BlockSpec((HALO, d), lambda i: (jnp.maximum(i * hb - 1, 0), 0)),
                  pl.BlockSpec((tm, d), lambda i: (i, 0)),
                  pl.BlockSpec((CONV_WIDTH, d), lambda i: (0, 0)),
                  pl.BlockSpec((d, d), lambda i: (0, 0)),
                  pl.BlockSpec((tm, d), lambda i: (i, 0)),
                  pl.BlockSpec((None, 1, d), lambda i: (i // per_b, 0, 0))],
        out_specs=pl.BlockSpec((tm, d), lambda i: (i, 0)),
        out_shape=jax.ShapeDtypeStruct((m, d), F32),
        scratch_shapes=[pltpu.VMEM((tm + HALO, d), F32)],
        compiler_params=_cparams(("parallel",)),
        name="conv_out",
    )(v, v, gb, conv_w, w_out, x2d, gate)


def _mlp_kernel(x_ref, gain_ref, sh_ref, sc_ref, g_ref, w1_ref, w2_ref, o_ref, h_ref, acc_ref):
    f = pl.program_id(1)

    @pl.when(f == 0)
    def _():
        h_ref[...] = _norm_mod(x_ref[...], gain_ref[...], sh_ref[...], sc_ref[...]).astype(BF16)
        acc_ref[...] = jnp.zeros_like(acc_ref)

    h1 = jnp.dot(h_ref[...], w1_ref[...], preferred_element_type=F32)
    h1 = jnp.square(jnp.maximum(h1, 0.0)).astype(BF16)
    acc_ref[...] += jnp.dot(h1, w2_ref[...], preferred_element_type=F32)

    @pl.when(f == pl.num_programs(1) - 1)
    def _():
        o_ref[...] = x_ref[...] + g_ref[...] * acc_ref[...]


def _mlp(x2d, gain, sh, sc, gate, w1, w2):
    m, d = x2d.shape
    ff = w1.shape[1]
    tm, tf = TM_MLP, TF_MLP
    per_b = SEQ // tm
    vec = lambda: pl.BlockSpec((None, 1, d), lambda i, f: (i // per_b, 0, 0))
    return pl.pallas_call(
        _mlp_kernel,
        grid=(m // tm, ff // tf),
        in_specs=[pl.BlockSpec((tm, d), lambda i, f: (i, 0)),
                  pl.BlockSpec((1, d), lambda i, f: (0, 0)),
                  vec(), vec(), vec(),
                  pl.BlockSpec((d, tf), lambda i, f: (0, f)),
                  pl.BlockSpec((tf, d), lambda i, f: (f, 0))],
        out_specs=pl.BlockSpec((tm, d), lambda i, f: (i, 0)),
        out_shape=jax.ShapeDtypeStruct((m, d), F32),
        scratch_shapes=[pltpu.VMEM((tm, d), BF16), pltpu.VMEM((tm, d), F32)],
        compiler_params=_cparams(("parallel", "arbitrary")),
        name="mlp",
    )(x2d, gain, sh, sc, gate, w1, w2)


def _qkv_kernel(x_ref, gq_ref, shq_ref, scq_ref, gkv_ref, shkv_ref, sckv_ref,
                wq_ref, wg_ref, wkv_ref, q_ref, gt_ref, kv_ref):
    x = x_ref[...]
    ms = jnp.mean(x * x, axis=-1, keepdims=True)
    y = x * lax.rsqrt(ms + EPS)
    hq = ((y * gq_ref[...]) * (1.0 + scq_ref[...]) + shq_ref[...]).astype(BF16)
    hkv = ((y * gkv_ref[...]) * (1.0 + sckv_ref[...]) + shkv_ref[...]).astype(BF16)
    q_ref[...] = jnp.dot(hq, wq_ref[...], preferred_element_type=F32).astype(BF16)
    gates_t = jax.nn.sigmoid(lax.dot_general(wg_ref[...], hq, _NT, preferred_element_type=F32))
    for g in range(N_KV_GROUPS):
        gt_ref[g] = gates_t[g * GATE_ROWS:(g + 1) * GATE_ROWS, :]
    kv = jnp.dot(hkv, wkv_ref[...], preferred_element_type=F32)
    for r in range(2 * N_BRANCH):
        for g in range(N_KV_GROUPS):
            c0 = (r * N_KV_GROUPS + g) * HEAD_DIM
            kv_ref[r, g] = kv[:, c0:c0 + HEAD_DIM].astype(BF16)


def _qkv_proj(x3d, gq, shq, scq, gkv, shkv, sckv, wq, wg, wkv):
    bsz, s, d = x3d.shape
    tm = TM_PROJ
    nkv = wkv.shape[1]
    vec = lambda: pl.BlockSpec((None, 1, d), lambda b, i: (b, 0, 0))
    const = lambda shape: pl.BlockSpec(shape, lambda b, i: (0,) * len(shape))
    return pl.pallas_call(
        _qkv_kernel,
        grid=(bsz, s // tm),
        in_specs=[pl.BlockSpec((None, tm, d), lambda b, i: (b, i, 0)),
                  const((1, d)), vec(), vec(),
                  const((1, d)), vec(), vec(),
                  const((d, d)), const(wg.shape), const((d, nkv))],
        out_specs=[pl.BlockSpec((None, tm, d), lambda b, i: (b, i, 0)),
                   pl.BlockSpec((None, N_KV_GROUPS, GATE_ROWS, tm), lambda b, i: (b, 0, 0, i)),
                   pl.BlockSpec((2 * N_BRANCH, None, N_KV_GROUPS, tm, HEAD_DIM),
                                lambda b, i: (0, b, 0, i, 0))],
        out_shape=[jax.ShapeDtypeStruct((bsz, s, d), BF16),
                   jax.ShapeDtypeStruct((bsz, N_KV_GROUPS, GATE_ROWS, s), F32),
                   jax.ShapeDtypeStruct((2 * N_BRANCH, bsz, N_KV_GROUPS, s, HEAD_DIM), BF16)],
        compiler_params=_cparams(("parallel", "parallel")),
        name="qkv_proj",
    )(x3d, gq, shq, scq, gkv, shkv, sckv, wq, wg, wkv)


def _head_rms(t, gain):
    ms = jnp.mean(t * t, axis=-1, keepdims=True)
    return t * lax.rsqrt(ms + EPS) * gain


def _kvprep_kernel(kcr_ref, vcr_ref, ks_ref, vs_ref, kw_ref, vw_ref,
                   pe_ref, w1_ref, w2_ref, kg_ref, eye_ref,
                   kc_ref, vct_ref, ksa_ref, vst_ref, kwn_ref, vwt_ref, tok_ref):
    half = CMP_STRIDE * HEAD_DIM
    kg = kg_ref[...]
    eye_hd = eye_ref[0:HEAD_DIM, 0:HEAD_DIM]

    def compress(src_ref, idx):
        tok_ref[...] = src_ref[...].astype(F32)
        z_lo = jnp.zeros((N_CMP_PAD, CMP_HIDDEN), F32)
        z_hi = jnp.zeros((N_CMP_PAD, CMP_HIDDEN), F32)
        for l in range(CMP_STRIDE):
            t = tok_ref[pl.ds(l, N_CMP_PAD, stride=CMP_STRIDE), :].astype(BF16)
            z_lo = z_lo + jnp.dot(t, w1_ref[idx, l * HEAD_DIM:(l + 1) * HEAD_DIM, :],
                                  preferred_element_type=F32)
            z_hi = z_hi + jnp.dot(t, w1_ref[idx, half + l * HEAD_DIM:half + (l + 1) * HEAD_DIM, :],
                                  preferred_element_type=F32)
        z_hi = pltpu.roll(z_hi, N_CMP_PAD - 1, 0)
        pe_b = jnp.dot(pe_ref[idx], w1_ref[idx], preferred_element_type=F32)
        hid = jax.nn.gelu(z_lo + z_hi + pe_b[0:1, :]).astype(BF16)
        return jnp.dot(hid, w2_ref[idx], preferred_element_type=F32)

    kc_ref[...] = _head_rms(compress(kcr_ref, 0), kg[0:1, :]).astype(BF16)
    vc = compress(vcr_ref, 1).astype(BF16)
    vct_ref[...] = lax.dot_general(eye_hd, vc, _NT, preferred_element_type=F32).astype(BF16)

    ksn = _head_rms(ks_ref[...].astype(F32), kg[1:2, :]).astype(BF16)
    placed = jnp.dot(ksn, eye_ref[0:HEAD_DIM, 0:K_AUG], preferred_element_type=F32)
    row = lax.broadcasted_iota(jnp.int32, (SEQ, K_AUG), 0)
    col = lax.broadcasted_iota(jnp.int32, (SEQ, K_AUG), 1)
    onehot = (col - HEAD_DIM) == lax.shift_right_logical(row, 6)
    ksa_ref[...] = jnp.where(onehot, 1.0, placed).astype(BF16)
    kwn_ref[...] = _head_rms(kw_ref[...].astype(F32), kg[2:3, :]).astype(BF16)

    ones = jnp.ones((V_ROWS - HEAD_DIM, TK), BF16)
    for src, dst in ((vs_ref, vst_ref), (vw_ref, vwt_ref)):
        for c in range(SEQ // TK):
            blk = src[c * TK:(c + 1) * TK, :]
            dst[c, 0:HEAD_DIM, :] = lax.dot_general(eye_hd, blk, _NT,
                                                    preferred_element_type=F32).astype(BF16)
            dst[c, HEAD_DIM:V_ROWS, :] = ones


def _kv_prep(kv6, pe8, w1, w2, k_gain, eye):
    _, bsz, ng, s, hd = kv6.shape
    nt = s // TK
    sel = lambda r: pl.BlockSpec((None, None, None, s, hd), lambda b, g: (r, b, g, 0, 0))
    const = lambda shape: pl.BlockSpec(shape, lambda b, g: (0,) * len(shape))
    out2 = lambda n, w: pl.BlockSpec((None, None, n, w), lambda b, g: (b, g, 0, 0))
    out3 = pl.BlockSpec((None, None, nt, V_ROWS, TK), lambda b, g: (b, g, 0, 0, 0))
    vt_shape = jax.ShapeDtypeStruct((bsz, ng, nt, V_ROWS, TK), BF16)
    return pl.pallas_call(
        _kvprep_kernel,
        grid=(bsz, ng),
        in_specs=[sel(0), sel(1), sel(2), sel(3), sel(4), sel(5),
                  const(pe8.shape), const(w1.shape), const(w2.shape), const(k_gain.shape),
                  const(eye.shape)],
        out_specs=[out2(N_CMP_PAD, hd), out2(hd, N_CMP_PAD), out2(s, K_AUG), out3, out2(s, hd), out3],
        out_shape=[jax.ShapeDtypeStruct((bsz, ng, N_CMP_PAD, hd), BF16),
                   jax.ShapeDtypeStruct((bsz, ng, hd, N_CMP_PAD), BF16),
                   jax.ShapeDtypeStruct((bsz, ng, s, K_AUG), BF16),
                   vt_shape,
                   jax.ShapeDtypeStruct((bsz, ng, s, hd), BF16),
                   vt_shape],
        scratch_shapes=[pltpu.VMEM((s, hd), F32)],
        compiler_params=_cparams(("parallel", "parallel")),
        name="kv_prep",
    )(kv6, kv6, kv6, kv6, kv6, kv6, pe8, w1, w2, k_gain, eye)


def _nsa_kernel(q_ref, gt_ref, qg_ref, kc_ref, vct_ref, cmapt_ref, eye_ref, tri_ref,
                ksa_ref, vst_ref, kwn_ref, vwt_ref, o_ref, qa_ref, m_ref, acc_ref):
    i = pl.program_id(2)
    t0 = i * TQ
    hg = HEADS_PER_GROUP
    lanes = hg * TQ
    eye = eye_ref[...]
    head = lambda k: slice(k * TQ, (k + 1) * TQ)

    q_t = lax.dot_general(eye, q_ref[...], _NT, preferred_element_type=F32)
    qg = jnp.concatenate([qg_ref[...]] * (TQ // 128), axis=1)
    for k in range(hg):
        xk = q_t[k * HEAD_DIM:(k + 1) * HEAD_DIM, :]
        ms = jnp.mean(xk * xk, axis=0, keepdims=True)
        qa_ref[0:HEAD_DIM, head(k)] = (xk * lax.rsqrt(ms + EPS) * qg * (HEAD_DIM ** -0.5)).astype(BF16)
    qn = qa_ref[0:HEAD_DIM, :]

    lane_t = t0 + (lax.broadcasted_iota(jnp.int32, (1, lanes), 1) & (TQ - 1))

    sc = jnp.dot(kc_ref[...], qn, preferred_element_type=F32)
    n_id = lax.broadcasted_iota(jnp.int32, (N_CMP_PAD, lanes), 0)
    valid = (n_id * CMP_STRIDE + (CMP_BLOCK - 1)) <= lane_t
    sc = jnp.where(valid, sc, NEG)
    mc = jnp.max(sc, axis=0, keepdims=True)
    pc = jnp.where(valid, jnp.exp(sc - mc), 0.0)
    pc = pc / jnp.maximum(jnp.sum(pc, axis=0, keepdims=True), 1e-30)
    o_cmp = jnp.dot(vct_ref[...], pc.astype(BF16), preferred_element_type=F32)

    psum = pc[:, head(0)]
    for k in range(1, hg):
        psum = psum + pc[:, head(k)]
    imp = jnp.dot(cmapt_ref[...], psum, preferred_element_type=F32,
                  precision=lax.Precision.HIGHEST)
    tq = t0 + lax.broadcasted_iota(jnp.int32, (N_SELB, TQ), 1)
    j_id = lax.broadcasted_iota(jnp.int32, (N_SELB, TQ), 0)
    cur = lax.shift_right_logical(tq, 6)
    forced = (j_id == 0) | (j_id == cur) | (j_id == cur - 1)
    causal = j_id <= cur
    score = jnp.where(forced, BIG, jnp.where(causal, imp, NEG))
    rank = jnp.zeros((N_SELB, TQ), jnp.int32)
    for jp in range(N_SELB):
        other = score[jp:jp + 1, :]
        beats = (other > score) | ((other == score) & (j_id > jp))
        rank = rank + beats.astype(jnp.int32)
    chosen = (rank < N_TOP) & causal
    sel_bias = jnp.where(chosen, 0.0, NEG).astype(BF16)
    for k in range(hg):
        qa_ref[HEAD_DIM:HEAD_DIM + N_SELB, head(k)] = sel_bias
    qa_ref[HEAD_DIM + N_SELB:K_AUG, :] = jnp.zeros((K_AUG - HEAD_DIM - N_SELB, lanes), BF16)

    def reset():
        m_ref[...] = jnp.full(m_ref.shape, NEG, F32)
        acc_ref[...] = jnp.zeros(acc_ref.shape, F32)

    def flash_step(k_tile, q_op, v_t, bias):
        s = jnp.dot(k_tile, q_op, preferred_element_type=F32)
        if bias is not None:
            s = s + jnp.concatenate([bias] * hg, axis=1)
        m_old = m_ref[...]
        m_new = jnp.maximum(m_old, jnp.max(s, axis=0, keepdims=True))
        alpha = jnp.exp(m_old - m_new)
        p = jnp.exp(s - m_new).astype(BF16)
        acc_ref[...] = alpha * acc_ref[...] + jnp.dot(v_t, p, preferred_element_type=F32)
        m_ref[...] = m_new

    def finish():
        return acc_ref[0:HEAD_DIM, :] / jnp.maximum(acc_ref[HEAD_DIM:HEAD_DIM + 1, :], 1e-30)

    def k_rows(ref, c):
        return ref[pl.ds(pl.multiple_of(c * TK, TK), TK), :]

    reset()

    def sel_body(c, carry):
        flash_step(k_rows(ksa_ref, c), qa_ref[...], vst_ref[c], None)
        return carry

    lax.fori_loop(0, i, sel_body, 0)
    flash_step(k_rows(ksa_ref, i), qa_ref[...], vst_ref[i], tri_ref[0])
    o_sel = finish()

    reset()
    n_back = WINDOW // TK
    flash_step(k_rows(kwn_ref, i), qn, vwt_ref[i], tri_ref[0])
    for d in range(1, n_back + 1):
        @pl.when(i >= d)
        def _():
            flash_step(k_rows(kwn_ref, i - d), qn, vwt_ref[i - d], tri_ref[1] if d == n_back else None)
    o_win = finish()

    gt = gt_ref[...]
    parts = []
    for k in range(hg):
        parts.append(gt[3 * k:3 * k + 1, :] * o_cmp[:, head(k)]
                     + gt[3 * k + 1:3 * k + 2, :] * o_sel[:, head(k)]
                     + gt[3 * k + 2:3 * k + 3, :] * o_win[:, head(k)])
    o_t = jnp.concatenate(parts, axis=0).astype(BF16)
    o_ref[...] = lax.dot_general(eye, o_t, _NT, preferred_element_type=F32).astype(BF16)


def _nsa(q, gates_t, qg_b, kc, vct, cmapt, eye, tri, ksa, vst, kwn, vwt):
    bsz, s, d = q.shape
    ng = N_KV_GROUPS
    gw = HEADS_PER_GROUP * HEAD_DIM
    lanes = HEADS_PER_GROUP * TQ
    nt = s // TK
    assert TQ == TK and WINDOW % TK == 0 and gw == TQ
    per_bg = lambda n, w: pl.BlockSpec((None, None, n, w), lambda b, g, i: (b, g, 0, 0))
    vt = lambda: pl.BlockSpec((None, None, nt, V_ROWS, TK), lambda b, g, i: (b, g, 0, 0, 0))
    const = lambda shape: pl.BlockSpec(shape, lambda b, g, i: (0,) * len(shape))
    return pl.pallas_call(
        _nsa_kernel,
        grid=(bsz, ng, s // TQ),
        in_specs=[pl.BlockSpec((None, TQ, gw), lambda b, g, i: (b, i, g)),
                  pl.BlockSpec((None, None, GATE_ROWS, TQ), lambda b, g, i: (b, g, 0, i)),
                  const(qg_b.shape),
                  per_bg(N_CMP_PAD, HEAD_DIM), per_bg(HEAD_DIM, N_CMP_PAD),
                  const(cmapt.shape), const(eye.shape), const(tri.shape),
                  per_bg(s, K_AUG), vt(), per_bg(s, HEAD_DIM), vt()],
        out_specs=pl.BlockSpec((None, TQ, gw), lambda b, g, i: (b, i, g)),
        out_shape=jax.ShapeDtypeStruct((bsz, s, d), BF16),
        scratch_shapes=[pltpu.VMEM((K_AUG, lanes), BF16), pltpu.VMEM((1, lanes), F32),
                        pltpu.VMEM((V_ROWS, lanes), F32)],
        compiler_params=_cparams(("parallel", "parallel", "arbitrary")),
        name="nsa",
    )(q, gates_t, qg_b, kc, vct, cmapt, eye, tri, ksa, vst, kwn, vwt)


def _outproj_kernel(a_ref, w_ref, x_ref, g_ref, o_ref):
    mix = jnp.dot(a_ref[...], w_ref[...], preferred_element_type=F32)
    o_ref[...] = x_ref[...] + g_ref[...] * mix


def _out_proj(a2d, w, x2d, gate):
    m, d = x2d.shape
    tm = TM_PROJ
    per_b = SEQ // tm
    return pl.pallas_call(
        _outproj_kernel,
        grid=(m // tm,),
        in_specs=[pl.BlockSpec((tm, d), lambda i: (i, 0)),
                  pl.BlockSpec((d, d), lambda i: (0, 0)),
                  pl.BlockSpec((tm, d), lambda i: (i, 0)),
                  pl.BlockSpec((None, 1, d), lambda i: (i // per_b, 0, 0))],
        out_specs=pl.BlockSpec((tm, d), lambda i: (i, 0)),
        out_shape=jax.ShapeDtypeStruct((m, d), F32),
        compiler_params=_cparams(("parallel",)),
        name="out_proj",
    )(a2d, w, x2d, gate)


def _cmp_to_sel_t():
    c0 = np.arange(N_CMP_PAD)[None, :] * CMP_STRIDE
    s0 = np.arange(N_SELB)[:, None] * SEL_BLOCK
    ov = np.minimum(c0 + CMP_BLOCK, s0 + SEL_BLOCK) - np.maximum(c0, s0)
    m = (np.clip(ov, 0, None) / CMP_BLOCK).astype(np.float32)
    m[:, N_CMP:] = 0.0
    return m


def _tri_bias():
    r = np.arange(TK)[:, None]
    c = np.arange(TQ)[None, :]
    return np.stack([np.where(r <= c, 0.0, NEG), np.where(r > c, 0.0, NEG)]).astype(np.float32)


def kernel(x, c, norm_gain, w_ada, b_ada, w_a_in, conv_w, w_a_out, w_qg, q_gain, w_o,
           kv_norm_gain, w_ada_kv, b_ada_kv, w_kv, k_gain, cmp_pe, cmp_w1, cmp_w2,
           w_mlp1, w_mlp2):
    bsz, s, d = x.shape
    assert (s, d) == (SEQ, D_MODEL)
    m = bsz * s
    x2d = x.reshape(m, d)

    def split_mod(mod, n):
        return [mod[:, k * d:(k + 1) * d].reshape(bsz, 1, d) for k in range(n)]

    sh1, sc1, g1, sh2, sc2, g2 = split_mod(_ada(c, w_ada[0], b_ada[0]), 6)
    gb, v = _in_proj(x2d, norm_gain[0, 0].reshape(1, d), sh1, sc1, w_a_in[0].astype(BF16))
    x2d = _conv_out(v, gb, conv_w[0], w_a_out[0].astype(BF16), x2d, g1)
    x2d = _mlp(x2d, norm_gain[0, 1].reshape(1, d), sh2, sc2, g2,
               w_mlp1[0].astype(BF16), w_mlp2[0].astype(BF16))

    sh1, sc1, g1, sh2, sc2, g2 = split_mod(_ada(c, w_ada[1], b_ada[1]), 6)
    sh_kv, sc_kv = split_mod(_ada(c, w_ada_kv, b_ada_kv), 2)
    nq = N_HEADS * HEAD_DIM
    wq = w_qg[0][:, :nq].astype(BF16)
    per_g = HEADS_PER_GROUP * N_BRANCH
    wg = w_qg[0][:, nq:].reshape(d, N_KV_GROUPS, per_g)
    wg = jnp.pad(wg, ((0, 0), (0, 0), (0, GATE_ROWS - per_g))).reshape(d, N_KV_GROUPS * GATE_ROWS)
    q, gates_t, kv6 = _qkv_proj(x2d.reshape(bsz, s, d), norm_gain[1, 0].reshape(1, d), sh1, sc1,
                                kv_norm_gain.reshape(1, d), sh_kv, sc_kv, wq, wg.T.astype(BF16),
                                w_kv.astype(BF16))
    eye = jnp.eye(TQ, dtype=BF16)
    pe8 = jnp.broadcast_to(cmp_pe.reshape(2, 1, CMP_BLOCK * HEAD_DIM), (2, 8, CMP_BLOCK * HEAD_DIM))
    kc, vct, ksa, vst, kwn, vwt = _kv_prep(kv6, pe8.astype(BF16), cmp_w1.astype(BF16),
                                           cmp_w2.astype(BF16), k_gain, eye)
    qg_b = jnp.broadcast_to(q_gain[0].reshape(HEAD_DIM, 1), (HEAD_DIM, 128))
    attn = _nsa(q, gates_t, qg_b, kc, vct, jnp.asarray(_cmp_to_sel_t()), eye, jnp.asarray(_tri_bias()),
                ksa, vst, kwn, vwt)
    x2d = _out_proj(attn.reshape(m, d), w_o[0].astype(BF16), x2d, g1)
    x2d = _mlp(x2d, norm_gain[1, 1].reshape(1, d), sh2, sc2, g2,
               w_mlp1[1].astype(BF16), w_mlp2[1].astype(BF16))
    return x2d.reshape(bsz, s, d)
```

```python
import functools

import numpy as np
import jax
import jax.numpy as jnp
from jax import lax
from jax.experimental import pallas as pl
from jax.experimental.pallas import tpu as pltpu

D_MODEL = 1024
SEQ = 2048
DEPTH = 2
N_A_LAYERS = DEPTH // 2
CONV_WIDTH = 3
D_FF = 4 * D_MODEL
HEAD_DIM = 64
N_HEADS = D_MODEL // HEAD_DIM
N_KV_GROUPS = 4
HEADS_PER_GROUP = N_HEADS // N_KV_GROUPS
N_BRANCH = 3
CMP_BLOCK = 32
CMP_STRIDE = 16
CMP_HIDDEN = 4 * HEAD_DIM
SEL_BLOCK = 64
N_SEL = 16
WINDOW = 512
EPS = 1e-6
NEG = -1e30
BIG = 1e30

N_CMP = (SEQ - CMP_BLOCK) // CMP_STRIDE + 1
N_CMP_PAD = 128
N_SELB = SEQ // SEL_BLOCK
N_TOP = min(N_SEL, N_SELB)

F32 = jnp.float32
BF16 = jnp.bfloat16

VMEM_LIMIT_BYTES = 56 * 1024 * 1024

TM_PROJ = 1024
TM_QKV = 512
TN_INPROJ = 512
TM_MLP = 1024
TF_MLP = 512
TQ = 256
TK = 256
HALO = 8
GATE_ROWS = 16
V_ROWS = HEAD_DIM + 16
K_AUG = 128

_NT = (((1,), (1,)), ((), ()))

Q_SCALE = HEAD_DIM ** -0.5 * float(np.log2(np.e))


def _cparams(sem):
    return pltpu.CompilerParams(dimension_semantics=sem, vmem_limit_bytes=VMEM_LIMIT_BYTES)


def _norm_mod(x, gain, shift, scale):
    ms = jnp.mean(x * x, axis=-1, keepdims=True)
    y = x * lax.rsqrt(ms + EPS) * gain
    return y * (1.0 + scale) + shift


def _ada_kernel(c_ref, w_ref, b_ref, o_ref):
    c = c_ref[...]
    ca = c * jax.nn.sigmoid(c)
    o_ref[...] = jnp.dot(ca.astype(BF16), w_ref[...].astype(BF16),
                         preferred_element_type=F32) + b_ref[...]


def _ada(c, w, b):
    bsz, d = c.shape
    n = w.shape[1]
    tn = 1024
    return pl.pallas_call(
        _ada_kernel,
        grid=(n // tn,),
        in_specs=[pl.BlockSpec((bsz, d), lambda j: (0, 0)),
                  pl.BlockSpec((d, tn), lambda j: (0, j)),
                  pl.BlockSpec((1, tn), lambda j: (0, j))],
        out_specs=pl.BlockSpec((bsz, tn), lambda j: (0, j)),
        out_shape=jax.ShapeDtypeStruct((bsz, n), F32),
        compiler_params=_cparams(("arbitrary",)),
        name="ada",
    )(c, w, b.reshape(1, n))


def _inproj_kernel(x_ref, gain_ref, sh_ref, sc_ref, wb_ref, wc_ref, wu_ref,
                   gb_ref, v_ref, h_ref):
    @pl.when(pl.program_id(1) == 0)
    def _():
        h_ref[...] = _norm_mod(x_ref[...], gain_ref[...], sh_ref[...], sc_ref[...]).astype(BF16)

    h = h_ref[...]
    gb = jnp.dot(h, wb_ref[...], preferred_element_type=F32)
    gc = jnp.dot(h, wc_ref[...], preferred_element_type=F32)
    u = jnp.dot(h, wu_ref[...], preferred_element_type=F32)
    gb_ref[...] = gb.astype(BF16)
    v_ref[...] = (gc * u).astype(BF16)


def _in_proj(x2d, gain, sh, sc, w_in):
    m, d = x2d.shape
    tm, tn = TM_PROJ, TN_INPROJ
    per_b = SEQ // tm
    nj = d // tn
    vec = lambda: pl.BlockSpec((None, 1, d), lambda i, j: (i // per_b, 0, 0))
    return pl.pallas_call(
        _inproj_kernel,
        grid=(m // tm, nj),
        in_specs=[pl.BlockSpec((tm, d), lambda i, j: (i, 0)),
                  pl.BlockSpec((1, d), lambda i, j: (0, 0)),
                  vec(), vec(),
                  pl.BlockSpec((d, tn), lambda i, j: (0, j)),
                  pl.BlockSpec((d, tn), lambda i, j: (0, j + nj)),
                  pl.BlockSpec((d, tn), lambda i, j: (0, j + 2 * nj))],
        out_specs=[pl.BlockSpec((tm, tn), lambda i, j: (i, j)),
                   pl.BlockSpec((tm, tn), lambda i, j: (i, j))],
        out_shape=[jax.ShapeDtypeStruct((m, d), BF16), jax.ShapeDtypeStruct((m, d), BF16)],
        scratch_shapes=[pltpu.VMEM((tm, d), BF16)],
        compiler_params=_cparams(("parallel", "arbitrary")),
        name="in_proj",
    )(x2d, gain, sh, sc, w_in, w_in, w_in)


def _convout_kernel(v_ref, halo_ref, gb_ref, cw_ref, w_ref, x_ref, g_ref, o_ref, vs_ref, *, per_b):
    tm = v_ref.shape[0]
    first = (pl.program_id(0) % per_b) == 0
    halo = halo_ref[...].astype(F32)
    vs_ref[0:HALO, :] = jnp.where(first, 0.0, halo)
    vs_ref[HALO:HALO + tm, :] = v_ref[...].astype(F32)
    cw = cw_ref[...]
    conv = (cw[2:3, :] * vs_ref[HALO:HALO + tm, :]
            + cw[1:2, :] * vs_ref[HALO - 1:HALO - 1 + tm, :]
            + cw[0:1, :] * vs_ref[HALO - 2:HALO - 2 + tm, :])
    a = (gb_ref[...].astype(F32) * conv).astype(BF16)
    mix = jnp.dot(a, w_ref[...], preferred_element_type=F32)
    o_ref[...] = x_ref[...] + g_ref[...] * mix


def _conv_out(v, gb, conv_w, w_out, x2d, gate):
    m, d = x2d.shape
    tm = TM_PROJ
    per_b = SEQ // tm
    hb = tm // HALO
    return pl.pallas_call(
        functools.partial(_convout_kernel, per_b=per_b),
        grid=(m // tm,),
        in_specs=[pl.BlockSpec((tm, d), lambda i: (i, 0)),
                  pl.BlockSpec((HALO, d), lambda i: (jnp.maximum(i * hb - 1, 0), 0)),
                  pl.BlockSpec((tm, d), lambda i: (i, 0)),
                  pl.BlockSpec((CONV_WIDTH, d), lambda i: (0, 0)),
                  pl.BlockSpec((d, d), lambda i: (0, 0)),
                  pl.BlockSpec((tm, d), lambda i: (i, 0)),
                  pl.BlockSpec((None, 1, d), lambda i: (i // per_b, 0, 0))],
        out_specs=pl.BlockSpec((tm, d), lambda i: (i, 0)),
        out_shape=jax.ShapeDtypeStruct((m, d), F32),
        scratch_shapes=[pltpu.VMEM((tm + HALO, d), F32)],
        compiler_params=_cparams(("parallel",)),
        name="conv_out",
    )(v, v, gb, conv_w, w_out, x2d, gate)


def _mlp_kernel(x_ref, gain_ref, sh_ref, sc_ref, g_ref, w1_ref, w2_ref, o_ref, h_ref, acc_ref):
    f = pl.program_id(1)

    @pl.when(f == 0)
    def _():
        h_ref[...] = _norm_mod(x_ref[...], gain_ref[...], sh_ref[...], sc_ref[...]).astype(BF16)
        acc_ref[...] = jnp.zeros_like(acc_ref)

    h1 = jnp.dot(h_ref[...], w1_ref[...], preferred_element_type=F32)
    h1 = jnp.square(jnp.maximum(h1, 0.0)).astype(BF16)
    acc_ref[...] += jnp.dot(h1, w2_ref[...], preferred_element_type=F32)

    @pl.when(f == pl.num_programs(1) - 1)
    def _():
        o_ref[...] = x_ref[...] + g_ref[...] * acc_ref[...]


def _mlp(x2d, gain, sh, sc, gate, w1, w2):
    m, d = x2d.shape
    ff = w1.shape[1]
    tm, tf = TM_MLP, TF_MLP
    per_b = SEQ // tm
    vec = lambda: pl.BlockSpec((None, 1, d), lambda i, f: (i // per_b, 0, 0))
    return pl.pallas_call(
        _mlp_kernel,
        grid=(m // tm, ff // tf),
        in_specs=[pl.BlockSpec((tm, d), lambda i, f: (i, 0)),
                  pl.BlockSpec((1, d), lambda i, f: (0, 0)),
                  vec(), vec(), vec(),
                  pl.BlockSpec((d, tf), lambda i, f: (0, f)),
                  pl.BlockSpec((tf, d), lambda i, f: (f, 0))],
        out_specs=pl.BlockSpec((tm, d), lambda i, f: (i, 0)),
        out_shape=jax.ShapeDtypeStruct((m, d), F32),
        scratch_shapes=[pltpu.VMEM((tm, d), BF16), pltpu.VMEM((tm, d), F32)],
        compiler_params=_cparams(("parallel", "arbitrary")),
        name="mlp",
    )(x2d, gain, sh, sc, gate, w1, w2)


def _qkv_kernel(x_ref, gq_ref, shq_ref, scq_ref, gkv_ref, shkv_ref, sckv_ref,
                wq_ref, wg_ref, wkv_ref, q_ref, gt_ref, kv_ref):
    x = x_ref[...]
    ms = jnp.mean(x * x, axis=-1, keepdims=True)
    y = x * lax.rsqrt(ms + EPS)
    hq = ((y * gq_ref[...]) * (1.0 + scq_ref[...]) + shq_ref[...]).astype(BF16)
    hkv = ((y * gkv_ref[...]) * (1.0 + sckv_ref[...]) + shkv_ref[...]).astype(BF16)
    q_ref[...] = jnp.dot(hq, wq_ref[...], preferred_element_type=F32).astype(BF16)
    gates_t = jax.nn.sigmoid(lax.dot_general(wg_ref[...], hq, _NT, preferred_element_type=F32))
    for g in range(N_KV_GROUPS):
        gt_ref[g] = gates_t[g * GATE_ROWS:(g + 1) * GATE_ROWS, :]
    kv = jnp.dot(hkv, wkv_ref[...], preferred_element_type=F32)
    for r in range(2 * N_BRANCH):
        for g in range(N_KV_GROUPS):
            c0 = (r * N_KV_GROUPS + g) * HEAD_DIM
            kv_ref[r, g] = kv[:, c0:c0 + HEAD_DIM].astype(BF16)


def _qkv_proj(x3d, gq, shq, scq, gkv, shkv, sckv, wq, wg, wkv):
    bsz, s, d = x3d.shape
    tm = TM_QKV
    nkv = wkv.shape[1]
    vec = lambda: pl.BlockSpec((None, 1, d), lambda b, i: (b, 0, 0))
    const = lambda shape: pl.BlockSpec(shape, lambda b, i: (0,) * len(shape))
    return pl.pallas_call(
        _qkv_kernel,
        grid=(bsz, s // tm),
        in_specs=[pl.BlockSpec((None, tm, d), lambda b, i: (b, i, 0)),
                  const((1, d)), vec(), vec(),
                  const((1, d)), vec(), vec(),
                  const((d, d)), const(wg.shape), const((d, nkv))],
        out_specs=[pl.BlockSpec((None, tm, d), lambda b, i: (b, i, 0)),
                   pl.BlockSpec((None, N_KV_GROUPS, GATE_ROWS, tm), lambda b, i: (b, 0, 0, i)),
                   pl.BlockSpec((2 * N_BRANCH, None, N_KV_GROUPS, tm, HEAD_DIM),
                                lambda b, i: (0, b, 0, i, 0))],
        out_shape=[jax.ShapeDtypeStruct((bsz, s, d), BF16),
                   jax.ShapeDtypeStruct((bsz, N_KV_GROUPS, GATE_ROWS, s), F32),
                   jax.ShapeDtypeStruct((2 * N_BRANCH, bsz, N_KV_GROUPS, s, HEAD_DIM), BF16)],
        compiler_params=_cparams(("parallel", "parallel")),
        name="qkv_proj",
    )(x3d, gq, shq, scq, gkv, shkv, sckv, wq, wg, wkv)


def _head_rms(t, gain):
    ms = jnp.mean(t * t, axis=-1, keepdims=True)
    return t * lax.rsqrt(ms + EPS) * gain


def _kvprep_kernel(kcr_ref, vcr_ref, ks_ref, vs_ref, kw_ref, vw_ref,
                   pe_ref, w1_ref, w2_ref, kg_ref, eye_ref,
                   kc_ref, vct_ref, ksa_ref, vst_ref, kwn_ref, vwt_ref, tok_ref):
    half = CMP_STRIDE * HEAD_DIM
    kg = kg_ref[...]
    eye_hd = eye_ref[0:HEAD_DIM, 0:HEAD_DIM]

    def compress(src_ref, idx):
        tok_ref[...] = src_ref[...].astype(F32)
        z_lo = jnp.zeros((N_CMP_PAD, CMP_HIDDEN), F32)
        z_hi = jnp.zeros((N_CMP_PAD, CMP_HIDDEN), F32)
        for l in range(CMP_STRIDE):
            t = tok_ref[pl.ds(l, N_CMP_PAD, stride=CMP_STRIDE), :].astype(BF16)
            z_lo = z_lo + jnp.dot(t, w1_ref[idx, l * HEAD_DIM:(l + 1) * HEAD_DIM, :],
                                  preferred_element_type=F32)
            z_hi = z_hi + jnp.dot(t, w1_ref[idx, half + l * HEAD_DIM:half + (l + 1) * HEAD_DIM, :],
                                  preferred_element_type=F32)
        z_hi = pltpu.roll(z_hi, N_CMP_PAD - 1, 0)
        pe_b = jnp.dot(pe_ref[idx], w1_ref[idx], preferred_element_type=F32)
        hid = jax.nn.gelu(z_lo + z_hi + pe_b[0:1, :]).astype(BF16)
        return jnp.dot(hid, w2_ref[idx], preferred_element_type=F32)

    kc_ref[...] = _head_rms(compress(kcr_ref, 0), kg[0:1, :]).astype(BF16)
    vc = compress(vcr_ref, 1).astype(BF16)
    vct_ref[...] = lax.dot_general(eye_hd, vc, _NT, preferred_element_type=F32).astype(BF16)

    ksn = _head_rms(ks_ref[...].astype(F32), kg[1:2, :]).astype(BF16)
    placed = jnp.dot(ksn, eye_ref[0:HEAD_DIM, 0:K_AUG], preferred_element_type=F32)
    row = lax.broadcasted_iota(jnp.int32, (SEQ, K_AUG), 0)
    col = lax.broadcasted_iota(jnp.int32, (SEQ, K_AUG), 1)
    onehot = (col - HEAD_DIM) == lax.shift_right_logical(row, 6)
    ksa_ref[...] = jnp.where(onehot, 1.0, placed).astype(BF16)
    kwn_ref[...] = _head_rms(kw_ref[...].astype(F32), kg[2:3, :]).astype(BF16)

    ones = jnp.ones((V_ROWS - HEAD_DIM, TK), BF16)
    for src, dst in ((vs_ref, vst_ref), (vw_ref, vwt_ref)):
        for c in range(SEQ // TK):
            blk = src[c * TK:(c + 1) * TK, :]
            dst[c, 0:HEAD_DIM, :] = lax.dot_general(eye_hd, blk, _NT,
                                                    preferred_element_type=F32).astype(BF16)
            dst[c, HEAD_DIM:V_ROWS, :] = ones


def _kv_prep(kv6, pe8, w1, w2, k_gain, eye):
    _, bsz, ng, s, hd = kv6.shape
    nt = s // TK
    sel = lambda r: pl.BlockSpec((None, None, None, s, hd), lambda b, g: (r, b, g, 0, 0))
    const = lambda shape: pl.BlockSpec(shape, lambda b, g: (0,) * len(shape))
    out2 = lambda n, w: pl.BlockSpec((None, None, n, w), lambda b, g: (b, g, 0, 0))
    out3 = pl.BlockSpec((None, None, nt, V_ROWS, TK), lambda b, g: (b, g, 0, 0, 0))
    vt_shape = jax.ShapeDtypeStruct((bsz, ng, nt, V_ROWS, TK), BF16)
    return pl.pallas_call(
        _kvprep_kernel,
        grid=(bsz, ng),
        in_specs=[sel(0), sel(1), sel(2), sel(3), sel(4), sel(5),
                  const(pe8.shape), const(w1.shape), const(w2.shape), const(k_gain.shape),
                  const(eye.shape)],
        out_specs=[out2(N_CMP_PAD, hd), out2(hd, N_CMP_PAD), out2(s, K_AUG), out3, out2(s, hd), out3],
        out_shape=[jax.ShapeDtypeStruct((bsz, ng, N_CMP_PAD, hd), BF16),
                   jax.ShapeDtypeStruct((bsz, ng, hd, N_CMP_PAD), BF16),
                   jax.ShapeDtypeStruct((bsz, ng, s, K_AUG), BF16),
                   vt_shape,
                   jax.ShapeDtypeStruct((bsz, ng, s, hd), BF16),
                   vt_shape],
        scratch_shapes=[pltpu.VMEM((s, hd), F32)],
        compiler_params=_cparams(("parallel", "parallel")),
        name="kv_prep",
    )(kv6, kv6, kv6, kv6, kv6, kv6, pe8, w1, w2, k_gain, eye)


def _nsa_kernel(q_ref, gt_ref, qg_ref, kc_ref, vct_ref, cmapt_ref, eye_ref, tri_ref, wtri_ref,
                ksa_ref, vst_ref, kwn_ref, vwt_ref, o_ref,
                qa_ref, m_ref, a_ref, acc_ref, s_buf, p_buf):
    i = pl.program_id(2)
    t0 = i * TQ
    hg = HEADS_PER_GROUP
    lanes = hg * TQ
    eye = eye_ref[...]
    head = lambda k: slice(k * TQ, (k + 1) * TQ)

    q_t = lax.dot_general(eye, q_ref[...], _NT, preferred_element_type=F32)
    qg = jnp.concatenate([qg_ref[...]] * (TQ // 128), axis=1)
    for k in range(hg):
        xk = q_t[k * HEAD_DIM:(k + 1) * HEAD_DIM, :]
        ms = jnp.mean(xk * xk, axis=0, keepdims=True)
        qa_ref[0:HEAD_DIM, head(k)] = (xk * lax.rsqrt(ms + EPS) * qg * Q_SCALE).astype(BF16)
    qn = qa_ref[0:HEAD_DIM, :]

    lane_t = t0 + (lax.broadcasted_iota(jnp.int32, (1, lanes), 1) & (TQ - 1))

    sc = jnp.dot(kc_ref[...], qn, preferred_element_type=F32)
    n_id = lax.broadcasted_iota(jnp.int32, (N_CMP_PAD, lanes), 0)
    valid = (n_id * CMP_STRIDE + (CMP_BLOCK - 1)) <= lane_t
    sc = jnp.where(valid, sc, NEG)
    mc = jnp.max(sc, axis=0, keepdims=True)
    pc = jnp.where(valid, jnp.exp2(sc - mc), 0.0)
    pc = pc / jnp.maximum(jnp.sum(pc, axis=0, keepdims=True), 1e-30)
    o_cmp = jnp.dot(vct_ref[...], pc.astype(BF16), preferred_element_type=F32)

    psum = pc[:, head(0)]
    for k in range(1, hg):
        psum = psum + pc[:, head(k)]
    imp = jnp.dot(cmapt_ref[...], psum, preferred_element_type=F32,
                  precision=lax.Precision.HIGHEST)
    tq = t0 + lax.broadcasted_iota(jnp.int32, (N_SELB, TQ), 1)
    j_id = lax.broadcasted_iota(jnp.int32, (N_SELB, TQ), 0)
    cur = lax.shift_right_logical(tq, 6)
    forced = (j_id == 0) | (j_id == cur) | (j_id == cur - 1)
    causal = j_id <= cur
    score = jnp.where(forced, BIG, jnp.where(causal, imp, NEG))
    rank = jnp.zeros((N_SELB, TQ), jnp.int32)
    for jp in range(N_SELB):
        other = score[jp:jp + 1, :]
        beats = (other > score) | ((other == score) & (j_id > jp))
        rank = rank + beats.astype(jnp.int32)
    chosen = (rank < N_TOP) & causal
    sel_bias = jnp.where(chosen, 0.0, NEG).astype(BF16)
    for k in range(hg):
        qa_ref[HEAD_DIM:HEAD_DIM + N_SELB, head(k)] = sel_bias
    qa_ref[HEAD_DIM + N_SELB:K_AUG, :] = jnp.zeros((K_AUG - HEAD_DIM - N_SELB, lanes), BF16)

    def k_rows(ref, c, n=TK):
        return ref[pl.ds(pl.multiple_of(c * TK, TK), n), :]

    def sel_scores(c):
        return jnp.dot(k_rows(ksa_ref, c), qa_ref[...], preferred_element_type=F32)

    def softmax_update(s):
        m_old = m_ref[...]
        m_new = jnp.maximum(m_old, jnp.max(s, axis=0, keepdims=True))
        m_ref[...] = m_new
        return jnp.exp2(m_old - m_new), jnp.exp2(s - m_new).astype(BF16)

    m_ref[...] = jnp.full(m_ref.shape, NEG, F32)
    acc_ref[...] = jnp.zeros(acc_ref.shape, F32)
    a_ref[...] = jnp.ones(a_ref.shape, F32)
    p_buf[1] = jnp.zeros((TK, lanes), BF16)
    s_buf[0] = sel_scores(0)

    def pv_prev(c, slot_prev):
        pv = jnp.dot(vst_ref[jnp.maximum(c - 1, 0)], p_buf[slot_prev], preferred_element_type=F32)
        return a_ref[...] * acc_ref[...] + pv

    def sel_step(c, cur):
        nxt = 1 - cur
        acc_new = pv_prev(c, nxt)
        s_buf[nxt] = sel_scores(c + 1)
        alpha, p = softmax_update(s_buf[cur])
        p_buf[cur] = p
        acc_ref[...] = acc_new
        a_ref[...] = alpha

    def sel_pair(j, carry):
        sel_step(2 * j, 0)
        sel_step(2 * j + 1, 1)
        return carry

    lax.fori_loop(0, lax.shift_right_logical(i, 1), sel_pair, 0)

    @pl.when((i & 1) == 1)
    def _():
        sel_step(i - 1, 0)

    n_back = WINDOW // TK
    c_w = jnp.maximum(i - n_back, 0)
    acc_prev = pv_prev(i, 1 - (i & 1))
    s_w = jnp.dot(k_rows(kwn_ref, c_w, (n_back + 1) * TK), qn, preferred_element_type=F32)
    alpha, p = softmax_update(s_buf[i & 1] + jnp.concatenate([tri_ref[...]] * hg, axis=1))
    acc_sel = alpha * acc_prev + jnp.dot(vst_ref[i], p, preferred_element_type=F32)
    o_sel = acc_sel[0:HEAD_DIM, :] / jnp.maximum(acc_sel[HEAD_DIM:HEAD_DIM + 1, :], 1e-30)

    s_w = s_w + jnp.concatenate([wtri_ref[jnp.minimum(i, n_back)]] * hg, axis=1)
    p_w = jnp.exp2(s_w - jnp.max(s_w, axis=0, keepdims=True)).astype(BF16)
    acc_w = jnp.dot(vwt_ref[c_w], p_w[0:TK, :], preferred_element_type=F32)
    for j in range(1, n_back + 1):
        acc_w = acc_w + jnp.dot(vwt_ref[c_w + j], p_w[j * TK:(j + 1) * TK, :], preferred_element_type=F32)
    o_win = acc_w[0:HEAD_DIM, :] / jnp.maximum(acc_w[HEAD_DIM:HEAD_DIM + 1, :], 1e-30)

    gt = gt_ref[...]
    parts = []
    for k in range(hg):
        parts.append(gt[3 * k:3 * k + 1, :] * o_cmp[:, head(k)]
                     + gt[3 * k + 1:3 * k + 2, :] * o_sel[:, head(k)]
                     + gt[3 * k + 2:3 * k + 3, :] * o_win[:, head(k)])
    o_t = jnp.concatenate(parts, axis=0).astype(BF16)
    o_ref[...] = lax.dot_general(eye, o_t, _NT, preferred_element_type=F32).astype(BF16)


def _nsa(q, gates_t, qg_b, kc, vct, cmapt, eye, tri, wtri, ksa, vst, kwn, vwt):
    bsz, s, d = q.shape
    ng = N_KV_GROUPS
    gw = HEADS_PER_GROUP * HEAD_DIM
    lanes = HEADS_PER_GROUP * TQ
    nt = s // TK
    assert TQ == TK and WINDOW % TK == 0 and gw == TQ
    per_bg = lambda n, w: pl.BlockSpec((None, None, n, w), lambda b, g, i: (b, g, 0, 0))
    vt = lambda: pl.BlockSpec((None, None, nt, V_ROWS, TK), lambda b, g, i: (b, g, 0, 0, 0))
    const = lambda shape: pl.BlockSpec(shape, lambda b, g, i: (0,) * len(shape))
    return pl.pallas_call(
        _nsa_kernel,
        grid=(bsz, ng, s // TQ),
        in_specs=[pl.BlockSpec((None, TQ, gw), lambda b, g, i: (b, i, g)),
                  pl.BlockSpec((None, None, GATE_ROWS, TQ), lambda b, g, i: (b, g, 0, i)),
                  const(qg_b.shape),
                  per_bg(N_CMP_PAD, HEAD_DIM), per_bg(HEAD_DIM, N_CMP_PAD),
                  const(cmapt.shape), const(eye.shape), const(tri.shape), const(wtri.shape),
                  per_bg(s, K_AUG), vt(), per_bg(s, HEAD_DIM), vt()],
        out_specs=pl.BlockSpec((None, TQ, gw), lambda b, g, i: (b, i, g)),
        out_shape=jax.ShapeDtypeStruct((bsz, s, d), BF16),
        scratch_shapes=[pltpu.VMEM((K_AUG, lanes), BF16),
                        pltpu.VMEM((1, lanes), F32),
                        pltpu.VMEM((1, lanes), F32),
                        pltpu.VMEM((V_ROWS, lanes), F32),
                        pltpu.VMEM((2, TK, lanes), F32),
                        pltpu.VMEM((2, TK, lanes), BF16)],
        compiler_params=_cparams(("parallel", "parallel", "arbitrary")),
        name="nsa",
    )(q, gates_t, qg_b, kc, vct, cmapt, eye, tri, wtri, ksa, vst, kwn, vwt)


def _outproj_kernel(a_ref, w_ref, x_ref, g_ref, o_ref):
    mix = jnp.dot(a_ref[...], w_ref[...], preferred_element_type=F32)
    o_ref[...] = x_ref[...] + g_ref[...] * mix


def _out_proj(a2d, w, x2d, gate):
    m, d = x2d.shape
    tm = TM_PROJ
    per_b = SEQ // tm
    return pl.pallas_call(
        _outproj_kernel,
        grid=(m // tm,),
        in_specs=[pl.BlockSpec((tm, d), lambda i: (i, 0)),
                  pl.BlockSpec((d, d), lambda i: (0, 0)),
                  pl.BlockSpec((tm, d), lambda i: (i, 0)),
                  pl.BlockSpec((None, 1, d), lambda i: (i // per_b, 0, 0))],
        out_specs=pl.BlockSpec((tm, d), lambda i: (i, 0)),
        out_shape=jax.ShapeDtypeStruct((m, d), F32),
        compiler_params=_cparams(("parallel",)),
        name="out_proj",
    )(a2d, w, x2d, gate)


def _cmp_to_sel_t():
    c0 = np.arange(N_CMP_PAD)[None, :] * CMP_STRIDE
    s0 = np.arange(N_SELB)[:, None] * SEL_BLOCK
    ov = np.minimum(c0 + CMP_BLOCK, s0 + SEL_BLOCK) - np.maximum(c0, s0)
    m = (np.clip(ov, 0, None) / CMP_BLOCK).astype(np.float32)
    m[:, N_CMP:] = 0.0
    return m


def _tri_bias():
    r = np.arange(TK)[:, None]
    c = np.arange(TQ)[None, :]
    return np.where(r <= c, 0.0, NEG).astype(np.float32)


def _window_bias():
    n_back = WINDOW // TK
    r = np.arange(WINDOW + TQ)[:, None]
    c = np.arange(TQ)[None, :]
    tabs = [np.where(r <= i * TQ + c, 0.0, NEG) for i in range(n_back)]
    tabs.append(np.where((r > c) & (r <= c + WINDOW), 0.0, NEG))
    return np.stack(tabs).astype(np.float32)


def kernel(x, c, norm_gain, w_ada, b_ada, w_a_in, conv_w, w_a_out, w_qg, q_gain, w_o,
           kv_norm_gain, w_ada_kv, b_ada_kv, w_kv, k_gain, cmp_pe, cmp_w1, cmp_w2,
           w_mlp1, w_mlp2):
    bsz, s, d = x.shape
    assert (s, d) == (SEQ, D_MODEL)
    m = bsz * s
    x2d = x.reshape(m, d)

    def split_mod(mod, n):
        return [mod[:, k * d:(k + 1) * d].reshape(bsz, 1, d) for k in range(n)]

    sh1, sc1, g1, sh2, sc2, g2 = split_mod(_ada(c, w_ada[0], b_ada[0]), 6)
    gb, v = _in_proj(x2d, norm_gain[0, 0].reshape(1, d), sh1, sc1, w_a_in[0].astype(BF16))
    x2d = _conv_out(v, gb, conv_w[0], w_a_out[0].astype(BF16), x2d, g1)
    x2d = _mlp(x2d, norm_gain[0, 1].reshape(1, d), sh2, sc2, g2,
               w_mlp1[0].astype(BF16), w_mlp2[0].astype(BF16))

    sh1, sc1, g1, sh2, sc2, g2 = split_mod(_ada(c, w_ada[1], b_ada[1]), 6)
    sh_kv, sc_kv = split_mod(_ada(c, w_ada_kv, b_ada_kv), 2)
    nq = N_HEADS * HEAD_DIM
    wq = w_qg[0][:, :nq].astype(BF16)
    per_g = HEADS_PER_GROUP * N_BRANCH
    wg = w_qg[0][:, nq:].reshape(d, N_KV_GROUPS, per_g)
    wg = jnp.pad(wg, ((0, 0), (0, 0), (0, GATE_ROWS - per_g))).reshape(d, N_KV_GROUPS * GATE_ROWS)
    q, gates_t, kv6 = _qkv_proj(x2d.reshape(bsz, s, d), norm_gain[1, 0].reshape(1, d), sh1, sc1,
                                kv_norm_gain.reshape(1, d), sh_kv, sc_kv, wq, wg.T.astype(BF16),
                                w_kv.astype(BF16))
    eye = jnp.eye(TQ, dtype=BF16)
    pe8 = jnp.broadcast_to(cmp_pe.reshape(2, 1, CMP_BLOCK * HEAD_DIM), (2, 8, CMP_BLOCK * HEAD_DIM))
    kc, vct, ksa, vst, kwn, vwt = _kv_prep(kv6, pe8.astype(BF16), cmp_w1.astype(BF16),
                                           cmp_w2.astype(BF16), k_gain, eye)
    qg_b = jnp.broadcast_to(q_gain[0].reshape(HEAD_DIM, 1), (HEAD_DIM, 128))
    attn = _nsa(q, gates_t, qg_b, kc, vct, jnp.asarray(_cmp_to_sel_t()), eye, jnp.asarray(_tri_bias()),
                jnp.asarray(_window_bias()), ksa, vst, kwn, vwt)
    x2d = _out_proj(attn.reshape(m, d), w_o[0].astype(BF16), x2d, g1)
    x2d = _mlp(x2d, norm_gain[1, 1].reshape(1, d), sh2, sc2, g2,
               w_mlp1[1].astype(BF16), w_mlp2[1].astype(BF16))
    return x2d.reshape(bsz, s, d)
```

```python
import functools

import numpy as np
import jax
import jax.numpy as jnp
from jax import lax
from jax.experimental import pallas as pl
from jax.experimental.pallas import tpu as pltpu

D_MODEL = 1024
SEQ = 2048
DEPTH = 2
N_A_LAYERS = DEPTH // 2
CONV_WIDTH = 3
D_FF = 4 * D_MODEL
HEAD_DIM = 64
N_HEADS = D_MODEL // HEAD_DIM
N_KV_GROUPS = 4
HEADS_PER_GROUP = N_HEADS // N_KV_GROUPS
N_BRANCH = 3
CMP_BLOCK = 32
CMP_STRIDE = 16
CMP_HIDDEN = 4 * HEAD_DIM
SEL_BLOCK = 64
N_SEL = 16
WINDOW = 512
EPS = 1e-6
NEG = -1e30
BIG = 1e30

N_CMP = (SEQ - CMP_BLOCK) // CMP_STRIDE + 1
N_CMP_PAD = 128
N_SELB = SEQ // SEL_BLOCK
N_TOP = min(N_SEL, N_SELB)

F32 = jnp.float32
BF16 = jnp.bfloat16

VMEM_LIMIT_BYTES = 56 * 1024 * 1024

TM_PROJ = 1024
TM_QKV = 512
TN_INPROJ = 512
TM_MLP = 1024
TF_MLP = 1024
TQ = 256
TK = 256
HALO = 8
GATE_ROWS = 16
V_ROWS = HEAD_DIM + 16
K_AUG = 128

_NT = (((1,), (1,)), ((), ()))

Q_SCALE = HEAD_DIM ** -0.5 * float(np.log2(np.e))


def _cparams(sem):
    return pltpu.CompilerParams(dimension_semantics=sem, vmem_limit_bytes=VMEM_LIMIT_BYTES)


def _norm_mod(x, gain, shift, scale):
    ms = jnp.mean(x * x, axis=-1, keepdims=True)
    y = x * lax.rsqrt(ms + EPS) * gain
    return y * (1.0 + scale) + shift


def _ada_kernel(c_ref, w_ref, b_ref, o_ref):
    c = c_ref[...]
    ca = c * jax.nn.sigmoid(c)
    o_ref[...] = jnp.dot(ca.astype(BF16), w_ref[...].astype(BF16),
                         preferred_element_type=F32) + b_ref[...]


def _ada(c, w, b, layer):
    bsz, d = c.shape
    n = w.shape[2]
    tn = 1024
    return pl.pallas_call(
        _ada_kernel,
        grid=(n // tn,),
        in_specs=[pl.BlockSpec((bsz, d), lambda j: (0, 0)),
                  pl.BlockSpec((None, d, tn), lambda j: (layer, 0, j)),
                  pl.BlockSpec((None, 1, tn), lambda j: (layer, 0, j))],
        out_specs=pl.BlockSpec((bsz, tn), lambda j: (0, j)),
        out_shape=jax.ShapeDtypeStruct((bsz, n), F32),
        compiler_params=_cparams(("arbitrary",)),
        name="ada",
    )(c, w, b.reshape(b.shape[0], 1, n))


def _inproj_kernel(x_ref, gain_ref, sh_ref, sc_ref, wb_ref, wc_ref, wu_ref,
                   gb_ref, v_ref, h_ref):
    @pl.when(pl.program_id(1) == 0)
    def _():
        h_ref[...] = _norm_mod(x_ref[...], gain_ref[...], sh_ref[...], sc_ref[...]).astype(BF16)

    h = h_ref[...]
    gb = jnp.dot(h, wb_ref[...], preferred_element_type=F32)
    gc = jnp.dot(h, wc_ref[...], preferred_element_type=F32)
    u = jnp.dot(h, wu_ref[...], preferred_element_type=F32)
    gb_ref[...] = gb.astype(BF16)
    v_ref[...] = (gc * u).astype(BF16)


def _in_proj(x2d, gain, sh, sc, w_in):
    m, d = x2d.shape
    tm, tn = TM_PROJ, TN_INPROJ
    per_b = SEQ // tm
    nj = d // tn
    vec = lambda: pl.BlockSpec((None, 1, d), lambda i, j: (i // per_b, 0, 0))
    return pl.pallas_call(
        _inproj_kernel,
        grid=(m // tm, nj),
        in_specs=[pl.BlockSpec((tm, d), lambda i, j: (i, 0)),
                  pl.BlockSpec((1, d), lambda i, j: (0, 0)),
                  vec(), vec(),
                  pl.BlockSpec((d, tn), lambda i, j: (0, j)),
                  pl.BlockSpec((d, tn), lambda i, j: (0, j + nj)),
                  pl.BlockSpec((d, tn), lambda i, j: (0, j + 2 * nj))],
        out_specs=[pl.BlockSpec((tm, tn), lambda i, j: (i, j)),
                   pl.BlockSpec((tm, tn), lambda i, j: (i, j))],
        out_shape=[jax.ShapeDtypeStruct((m, d), BF16), jax.ShapeDtypeStruct((m, d), BF16)],
        scratch_shapes=[pltpu.VMEM((tm, d), BF16)],
        compiler_params=_cparams(("parallel", "arbitrary")),
        name="in_proj",
    )(x2d, gain, sh, sc, w_in, w_in, w_in)


def _convout_kernel(v_ref, halo_ref, gb_ref, cw_ref, w_ref, x_ref, g_ref, o_ref, vs_ref, *, per_b):
    tm = v_ref.shape[0]
    first = (pl.program_id(0) % per_b) == 0
    halo = halo_ref[...].astype(F32)
    vs_ref[0:HALO, :] = jnp.where(first, 0.0, halo)
    vs_ref[HALO:HALO + tm, :] = v_ref[...].astype(F32)
    cw = cw_ref[...]
    conv = (cw[2:3, :] * vs_ref[HALO:HALO + tm, :]
            + cw[1:2, :] * vs_ref[HALO - 1:HALO - 1 + tm, :]
            + cw[0:1, :] * vs_ref[HALO - 2:HALO - 2 + tm, :])
    a = (gb_ref[...].astype(F32) * conv).astype(BF16)
    mix = jnp.dot(a, w_ref[...], preferred_element_type=F32)
    o_ref[...] = x_ref[...] + g_ref[...] * mix


def _conv_out(v, gb, conv_w, w_out, x2d, gate):
    m, d = x2d.shape
    tm = TM_PROJ
    per_b = SEQ // tm
    hb = tm // HALO
    return pl.pallas_call(
        functools.partial(_convout_kernel, per_b=per_b),
        grid=(m // tm,),
        in_specs=[pl.BlockSpec((tm, d), lambda i: (i, 0)),
                  pl.BlockSpec((HALO, d), lambda i: (jnp.maximum(i * hb - 1, 0), 0)),
                  pl.BlockSpec((tm, d), lambda i: (i, 0)),
                  pl.BlockSpec((CONV_WIDTH, d), lambda i: (0, 0)),
                  pl.BlockSpec((d, d), lambda i: (0, 0)),
                  pl.BlockSpec((tm, d), lambda i: (i, 0)),
                  pl.BlockSpec((None, 1, d), lambda i: (i // per_b, 0, 0))],
        out_specs=pl.BlockSpec((tm, d), lambda i: (i, 0)),
        out_shape=jax.ShapeDtypeStruct((m, d), F32),
        scratch_shapes=[pltpu.VMEM((tm + HALO, d), F32)],
        compiler_params=_cparams(("parallel",)),
        name="conv_out",
    )(v, v, gb, conv_w, w_out, x2d, gate)


def _mlp_kernel(x_ref, gain_ref, sh_ref, sc_ref, g_ref, w1_ref, w2_ref, o_ref, h_ref, acc_ref):
    f = pl.program_id(1)

    @pl.when(f == 0)
    def _():
        h_ref[...] = _norm_mod(x_ref[...], gain_ref[...], sh_ref[...], sc_ref[...]).astype(BF16)
        acc_ref[...] = jnp.zeros_like(acc_ref)

    h1 = jnp.dot(h_ref[...], w1_ref[...], preferred_element_type=F32)
    h1 = jnp.square(jnp.maximum(h1, 0.0)).astype(BF16)
    acc_ref[...] += jnp.dot(h1, w2_ref[...], preferred_element_type=F32)

    @pl.when(f == pl.num_programs(1) - 1)
    def _():
        o_ref[...] = x_ref[...] + g_ref[...] * acc_ref[...]


def _mlp(x2d, gain, sh, sc, gate, w1, w2):
    m, d = x2d.shape
    ff = w1.shape[1]
    tm, tf = TM_MLP, TF_MLP
    per_b = SEQ // tm
    vec = lambda: pl.BlockSpec((None, 1, d), lambda i, f: (i // per_b, 0, 0))
    return pl.pallas_call(
        _mlp_kernel,
        grid=(m // tm, ff // tf),
        in_specs=[pl.BlockSpec((tm, d), lambda i, f: (i, 0)),
                  pl.BlockSpec((1, d), lambda i, f: (0, 0)),
                  vec(), vec(), vec(),
                  pl.BlockSpec((d, tf), lambda i, f: (0, f)),
                  pl.BlockSpec((tf, d), lambda i, f: (f, 0))],
        out_specs=pl.BlockSpec((tm, d), lambda i, f: (i, 0)),
        out_shape=jax.ShapeDtypeStruct((m, d), F32),
        scratch_shapes=[pltpu.VMEM((tm, d), BF16), pltpu.VMEM((tm, d), F32)],
        compiler_params=_cparams(("parallel", "arbitrary")),
        name="mlp",
    )(x2d, gain, sh, sc, gate, w1, w2)


def _qkv_kernel(x_ref, gq_ref, shq_ref, scq_ref, gkv_ref, shkv_ref, sckv_ref,
                wq_ref, wg_ref, wkv_ref, qg_ref, q_ref, gt_ref, kv_ref):
    x = x_ref[...]
    tm = x.shape[0]
    ms = jnp.mean(x * x, axis=-1, keepdims=True)
    y = x * lax.rsqrt(ms + EPS)
    hq = ((y * gq_ref[...]) * (1.0 + scq_ref[...]) + shq_ref[...]).astype(BF16)
    hkv = ((y * gkv_ref[...]) * (1.0 + sckv_ref[...]) + shkv_ref[...]).astype(BF16)
    q_t = lax.dot_general(wq_ref[...], hq, _NT, preferred_element_type=F32)
    qg = jnp.concatenate([qg_ref[...]] * (tm // 128), axis=1)
    for h in range(N_HEADS):
        xh = q_t[h * HEAD_DIM:(h + 1) * HEAD_DIM, :]
        msh = jnp.mean(xh * xh, axis=0, keepdims=True)
        q_ref[h * HEAD_DIM:(h + 1) * HEAD_DIM, :] = (xh * lax.rsqrt(msh + EPS) * qg * Q_SCALE).astype(BF16)
    gates_t = jax.nn.sigmoid(lax.dot_general(wg_ref[...], hq, _NT, preferred_element_type=F32))
    for g in range(N_KV_GROUPS):
        gt_ref[g] = gates_t[g * GATE_ROWS:(g + 1) * GATE_ROWS, :]
    kv = jnp.dot(hkv, wkv_ref[...], preferred_element_type=F32)
    for r in range(2 * N_BRANCH):
        for g in range(N_KV_GROUPS):
            c0 = (r * N_KV_GROUPS + g) * HEAD_DIM
            kv_ref[r, g] = kv[:, c0:c0 + HEAD_DIM].astype(BF16)


def _qkv_proj(x3d, gq, shq, scq, gkv, shkv, sckv, wq, wg, wkv, qg_b):
    bsz, s, d = x3d.shape
    tm = TM_QKV
    nkv = wkv.shape[1]
    vec = lambda: pl.BlockSpec((None, 1, d), lambda b, i: (b, 0, 0))
    const = lambda shape: pl.BlockSpec(shape, lambda b, i: (0,) * len(shape))
    return pl.pallas_call(
        _qkv_kernel,
        grid=(bsz, s // tm),
        in_specs=[pl.BlockSpec((None, tm, d), lambda b, i: (b, i, 0)),
                  const((1, d)), vec(), vec(),
                  const((1, d)), vec(), vec(),
                  const((d, d)), const(wg.shape), const((d, nkv)), const(qg_b.shape)],
        out_specs=[pl.BlockSpec((None, d, tm), lambda b, i: (b, 0, i)),
                   pl.BlockSpec((None, N_KV_GROUPS, GATE_ROWS, tm), lambda b, i: (b, 0, 0, i)),
                   pl.BlockSpec((2 * N_BRANCH, None, N_KV_GROUPS, tm, HEAD_DIM),
                                lambda b, i: (0, b, 0, i, 0))],
        out_shape=[jax.ShapeDtypeStruct((bsz, d, s), BF16),
                   jax.ShapeDtypeStruct((bsz, N_KV_GROUPS, GATE_ROWS, s), F32),
                   jax.ShapeDtypeStruct((2 * N_BRANCH, bsz, N_KV_GROUPS, s, HEAD_DIM), BF16)],
        compiler_params=_cparams(("parallel", "parallel")),
        name="qkv_proj",
    )(x3d, gq, shq, scq, gkv, shkv, sckv, wq, wg, wkv, qg_b)


def _head_rms(t, gain):
    ms = jnp.mean(t * t, axis=-1, keepdims=True)
    return t * lax.rsqrt(ms + EPS) * gain


def _kvprep_kernel(kcr_ref, vcr_ref, ks_ref, vs_ref, kw_ref, vw_ref,
                   pe_ref, w1_ref, w2_ref, kg_ref, eye_ref,
                   kc_ref, vct_ref, ksa_ref, vst_ref, kwn_ref, vwt_ref, tok_ref):
    half = CMP_STRIDE * HEAD_DIM
    kg = kg_ref[...]
    eye_hd = eye_ref[0:HEAD_DIM, 0:HEAD_DIM]

    def compress(src_ref, idx):
        tok_ref[...] = src_ref[...].astype(F32)
        z_lo = jnp.zeros((N_CMP_PAD, CMP_HIDDEN), F32)
        z_hi = jnp.zeros((N_CMP_PAD, CMP_HIDDEN), F32)
        for l in range(CMP_STRIDE):
            t = tok_ref[pl.ds(l, N_CMP_PAD, stride=CMP_STRIDE), :].astype(BF16)
            z_lo = z_lo + jnp.dot(t, w1_ref[idx, l * HEAD_DIM:(l + 1) * HEAD_DIM, :],
                                  preferred_element_type=F32)
            z_hi = z_hi + jnp.dot(t, w1_ref[idx, half + l * HEAD_DIM:half + (l + 1) * HEAD_DIM, :],
                                  preferred_element_type=F32)
        z_hi = pltpu.roll(z_hi, N_CMP_PAD - 1, 0)
        pe_b = jnp.dot(pe_ref[idx], w1_ref[idx], preferred_element_type=F32)
        hid = jax.nn.gelu(z_lo + z_hi + pe_b[0:1, :]).astype(BF16)
        return jnp.dot(hid, w2_ref[idx], preferred_element_type=F32)

    kc_ref[...] = _head_rms(compress(kcr_ref, 0), kg[0:1, :]).astype(BF16)
    vc = compress(vcr_ref, 1).astype(BF16)
    vct_ref[...] = lax.dot_general(eye_hd, vc, _NT, preferred_element_type=F32).astype(BF16)

    ksn = _head_rms(ks_ref[...].astype(F32), kg[1:2, :]).astype(BF16)
    placed = jnp.dot(ksn, eye_ref[0:HEAD_DIM, 0:K_AUG], preferred_element_type=F32)
    row = lax.broadcasted_iota(jnp.int32, (SEQ, K_AUG), 0)
    col = lax.broadcasted_iota(jnp.int32, (SEQ, K_AUG), 1)
    onehot = (col - HEAD_DIM) == lax.shift_right_logical(row, 6)
    ksa_ref[...] = jnp.where(onehot, 1.0, placed).astype(BF16)
    kwn_ref[...] = _head_rms(kw_ref[...].astype(F32), kg[2:3, :]).astype(BF16)

    ones = jnp.ones((V_ROWS - HEAD_DIM, TK), BF16)
    for src, dst in ((vs_ref, vst_ref), (vw_ref, vwt_ref)):
        for c in range(SEQ // TK):
            blk = src[c * TK:(c + 1) * TK, :]
            dst[c, 0:HEAD_DIM, :] = lax.dot_general(eye_hd, blk, _NT,
                                                    preferred_element_type=F32).astype(BF16)
            dst[c, HEAD_DIM:V_ROWS, :] = ones


def _kv_prep(kv6, pe8, w1, w2, k_gain, eye):
    _, bsz, ng, s, hd = kv6.shape
    nt = s // TK
    sel = lambda r: pl.BlockSpec((None, None, None, s, hd), lambda b, g: (r, b, g, 0, 0))
    const = lambda shape: pl.BlockSpec(shape, lambda b, g: (0,) * len(shape))
    out2 = lambda n, w: pl.BlockSpec((None, None, n, w), lambda b, g: (b, g, 0, 0))
    out3 = pl.BlockSpec((None, None, nt, V_ROWS, TK), lambda b, g: (b, g, 0, 0, 0))
    vt_shape = jax.ShapeDtypeStruct((bsz, ng, nt, V_ROWS, TK), BF16)
    return pl.pallas_call(
        _kvprep_kernel,
        grid=(bsz, ng),
        in_specs=[sel(0), sel(1), sel(2), sel(3), sel(4), sel(5),
                  const(pe8.shape), const(w1.shape), const(w2.shape), const(k_gain.shape),
                  const(eye.shape)],
        out_specs=[out2(N_CMP_PAD, hd), out2(hd, N_CMP_PAD), out2(s, K_AUG), out3, out2(s, hd), out3],
        out_shape=[jax.ShapeDtypeStruct((bsz, ng, N_CMP_PAD, hd), BF16),
                   jax.ShapeDtypeStruct((bsz, ng, hd, N_CMP_PAD), BF16),
                   jax.ShapeDtypeStruct((bsz, ng, s, K_AUG), BF16),
                   vt_shape,
                   jax.ShapeDtypeStruct((bsz, ng, s, hd), BF16),
                   vt_shape],
        scratch_shapes=[pltpu.VMEM((s, hd), F32)],
        compiler_params=_cparams(("parallel", "parallel")),
        name="kv_prep",
    )(kv6, kv6, kv6, kv6, kv6, kv6, pe8, w1, w2, k_gain, eye)


def _nsa_kernel(q_ref, gt_ref, kc_ref, vct_ref, cmapt_ref, eye_ref, tri_ref, wtri_ref,
                ksa_ref, vst_ref, kwn_ref, vwt_ref, o_ref,
                qa_ref, m_ref, a_ref, acc_ref, s_buf, p_buf):
    i = pl.program_id(2)
    t0 = i * TQ
    hg = HEADS_PER_GROUP
    lanes = hg * TQ
    eye = eye_ref[...]
    head = lambda k: slice(k * TQ, (k + 1) * TQ)

    for k in range(hg):
        qa_ref[0:HEAD_DIM, head(k)] = q_ref[k * HEAD_DIM:(k + 1) * HEAD_DIM, :]
    qn = qa_ref[0:HEAD_DIM, :]

    lane_t = t0 + (lax.broadcasted_iota(jnp.int32, (1, lanes), 1) & (TQ - 1))

    def k_rows(ref, c, n=TK):
        return ref[pl.ds(pl.multiple_of(c * TK, TK), n), :]

    n_back = WINDOW // TK
    c_w = jnp.maximum(i - n_back, 0)
    sc = jnp.dot(kc_ref[...], qn, preferred_element_type=F32)
    s_w = jnp.dot(k_rows(kwn_ref, c_w, (n_back + 1) * TK), qn, preferred_element_type=F32)

    n_id = lax.broadcasted_iota(jnp.int32, (N_CMP_PAD, lanes), 0)
    valid = (n_id * CMP_STRIDE + (CMP_BLOCK - 1)) <= lane_t
    sc = jnp.where(valid, sc, NEG)
    mc = jnp.max(sc, axis=0, keepdims=True)
    pc = jnp.where(valid, jnp.exp2(sc - mc), 0.0)
    pc = pc / jnp.maximum(jnp.sum(pc, axis=0, keepdims=True), 1e-30)
    o_cmp = jnp.dot(vct_ref[...], pc.astype(BF16), preferred_element_type=F32)

    psum = pc[:, head(0)]
    for k in range(1, hg):
        psum = psum + pc[:, head(k)]
    imp = jnp.dot(cmapt_ref[...], psum, preferred_element_type=F32,
                  precision=lax.Precision.HIGHEST)

    s_w = s_w + jnp.concatenate([wtri_ref[jnp.minimum(i, n_back)]] * hg, axis=1)
    p_w = jnp.exp2(s_w - jnp.max(s_w, axis=0, keepdims=True)).astype(BF16)
    acc_w = jnp.dot(vwt_ref[c_w], p_w[0:TK, :], preferred_element_type=F32)
    for j in range(1, n_back + 1):
        acc_w = acc_w + jnp.dot(vwt_ref[c_w + j], p_w[j * TK:(j + 1) * TK, :], preferred_element_type=F32)
    o_win = acc_w[0:HEAD_DIM, :] / jnp.maximum(acc_w[HEAD_DIM:HEAD_DIM + 1, :], 1e-30)

    tq = t0 + lax.broadcasted_iota(jnp.int32, (N_SELB, TQ), 1)
    j_id = lax.broadcasted_iota(jnp.int32, (N_SELB, TQ), 0)
    cur = lax.shift_right_logical(tq, 6)
    forced = (j_id == 0) | (j_id == cur) | (j_id == cur - 1)
    causal = j_id <= cur
    score = jnp.where(forced, BIG, jnp.where(causal, imp, NEG))
    rank = jnp.zeros((N_SELB, TQ), jnp.int32)
    for jp in range(N_SELB):
        other = score[jp:jp + 1, :]
        beats = (other > score) | ((other == score) & (j_id > jp))
        rank = rank + beats.astype(jnp.int32)
    chosen = (rank < N_TOP) & causal
    sel_bias = jnp.where(chosen, 0.0, NEG).astype(BF16)
    for k in range(hg):
        qa_ref[HEAD_DIM:HEAD_DIM + N_SELB, head(k)] = sel_bias
    qa_ref[HEAD_DIM + N_SELB:K_AUG, :] = jnp.zeros((K_AUG - HEAD_DIM - N_SELB, lanes), BF16)

    def sel_scores(c):
        return jnp.dot(k_rows(ksa_ref, c), qa_ref[...], preferred_element_type=F32)

    def softmax_update(s):
        m_old = m_ref[...]
        m_new = jnp.maximum(m_old, jnp.max(s, axis=0, keepdims=True))
        m_ref[...] = m_new
        return jnp.exp2(m_old - m_new), jnp.exp2(s - m_new).astype(BF16)

    m_ref[...] = jnp.full(m_ref.shape, NEG, F32)
    acc_ref[...] = jnp.zeros(acc_ref.shape, F32)
    a_ref[...] = jnp.ones(a_ref.shape, F32)
    p_buf[1] = jnp.zeros((TK, lanes), BF16)
    s_buf[0] = sel_scores(0)

    def pv_prev(c, slot_prev):
        pv = jnp.dot(vst_ref[jnp.maximum(c - 1, 0)], p_buf[slot_prev], preferred_element_type=F32)
        return a_ref[...] * acc_ref[...] + pv

    def sel_step(c, cur):
        nxt = 1 - cur
        acc_new = pv_prev(c, nxt)
        s_buf[nxt] = sel_scores(c + 1)
        alpha, p = softmax_update(s_buf[cur])
        p_buf[cur] = p
        acc_ref[...] = acc_new
        a_ref[...] = alpha

    def sel_pair(j, carry):
        sel_step(2 * j, 0)
        sel_step(2 * j + 1, 1)
        return carry

    lax.fori_loop(0, lax.shift_right_logical(i, 1), sel_pair, 0)

    @pl.when((i & 1) == 1)
    def _():
        sel_step(i - 1, 0)

    acc_prev = pv_prev(i, 1 - (i & 1))
    alpha, p = softmax_update(s_buf[i & 1] + jnp.concatenate([tri_ref[...]] * hg, axis=1))
    acc_sel = alpha * acc_prev + jnp.dot(vst_ref[i], p, preferred_element_type=F32)
    o_sel = acc_sel[0:HEAD_DIM, :] / jnp.maximum(acc_sel[HEAD_DIM:HEAD_DIM + 1, :], 1e-30)

    gt = gt_ref[...]
    parts = []
    for k in range(hg):
        parts.append(gt[3 * k:3 * k + 1, :] * o_cmp[:, head(k)]
                     + gt[3 * k + 1:3 * k + 2, :] * o_sel[:, head(k)]
                     + gt[3 * k + 2:3 * k + 3, :] * o_win[:, head(k)])
    o_t = jnp.concatenate(parts, axis=0).astype(BF16)
    o_ref[...] = lax.dot_general(eye, o_t, _NT, preferred_element_type=F32).astype(BF16)


def _nsa(q_t, gates_t, kc, vct, cmapt, eye, tri, wtri, ksa, vst, kwn, vwt):
    bsz, d, s = q_t.shape
    ng = N_KV_GROUPS
    gw = HEADS_PER_GROUP * HEAD_DIM
    lanes = HEADS_PER_GROUP * TQ
    nt = s // TK
    assert TQ == TK and WINDOW % TK == 0 and gw == TQ
    per_bg = lambda n, w: pl.BlockSpec((None, None, n, w), lambda b, g, i: (b, g, 0, 0))
    vt = lambda: pl.BlockSpec((None, None, nt, V_ROWS, TK), lambda b, g, i: (b, g, 0, 0, 0))
    const = lambda shape: pl.BlockSpec(shape, lambda b, g, i: (0,) * len(shape))
    return pl.pallas_call(
        _nsa_kernel,
        grid=(bsz, ng, s // TQ),
        in_specs=[pl.BlockSpec((None, gw, TQ), lambda b, g, i: (b, g, i)),
                  pl.BlockSpec((None, None, GATE_ROWS, TQ), lambda b, g, i: (b, g, 0, i)),
                  per_bg(N_CMP_PAD, HEAD_DIM), per_bg(HEAD_DIM, N_CMP_PAD),
                  const(cmapt.shape), const(eye.shape), const(tri.shape), const(wtri.shape),
                  per_bg(s, K_AUG), vt(), per_bg(s, HEAD_DIM), vt()],
        out_specs=pl.BlockSpec((None, TQ, gw), lambda b, g, i: (b, i, g)),
        out_shape=jax.ShapeDtypeStruct((bsz, s, d), BF16),
        scratch_shapes=[pltpu.VMEM((K_AUG, lanes), BF16),
                        pltpu.VMEM((1, lanes), F32),
                        pltpu.VMEM((1, lanes), F32),
                        pltpu.VMEM((V_ROWS, lanes), F32),
                        pltpu.VMEM((2, TK, lanes), F32),
                        pltpu.VMEM((2, TK, lanes), BF16)],
        compiler_params=_cparams(("parallel", "parallel", "arbitrary")),
        name="nsa",
    )(q_t, gates_t, kc, vct, cmapt, eye, tri, wtri, ksa, vst, kwn, vwt)


def _outproj_kernel(a_ref, w_ref, x_ref, g_ref, o_ref):
    mix = jnp.dot(a_ref[...], w_ref[...], preferred_element_type=F32)
    o_ref[...] = x_ref[...] + g_ref[...] * mix


def _out_proj(a2d, w, x2d, gate):
    m, d = x2d.shape
    tm = TM_PROJ
    per_b = SEQ // tm
    return pl.pallas_call(
        _outproj_kernel,
        grid=(m // tm,),
        in_specs=[pl.BlockSpec((tm, d), lambda i: (i, 0)),
                  pl.BlockSpec((d, d), lambda i: (0, 0)),
                  pl.BlockSpec((tm, d), lambda i: (i, 0)),
                  pl.BlockSpec((None, 1, d), lambda i: (i // per_b, 0, 0))],
        out_specs=pl.BlockSpec((tm, d), lambda i: (i, 0)),
        out_shape=jax.ShapeDtypeStruct((m, d), F32),
        compiler_params=_cparams(("parallel",)),
        name="out_proj",
    )(a2d, w, x2d, gate)


def _cmp_to_sel_t():
    c0 = np.arange(N_CMP_PAD)[None, :] * CMP_STRIDE
    s0 = np.arange(N_SELB)[:, None] * SEL_BLOCK
    ov = np.minimum(c0 + CMP_BLOCK, s0 + SEL_BLOCK) - np.maximum(c0, s0)
    m = (np.clip(ov, 0, None) / CMP_BLOCK).astype(np.float32)
    m[:, N_CMP:] = 0.0
    return m


def _tri_bias():
    r = np.arange(TK)[:, None]
    c = np.arange(TQ)[None, :]
    return np.where(r <= c, 0.0, NEG).astype(np.float32)


def _window_bias():
    n_back = WINDOW // TK
    r = np.arange(WINDOW + TQ)[:, None]
    c = np.arange(TQ)[None, :]
    tabs = [np.where(r <= i * TQ + c, 0.0, NEG) for i in range(n_back)]
    tabs.append(np.where((r > c) & (r <= c + WINDOW), 0.0, NEG))
    return np.stack(tabs).astype(np.float32)


def kernel(x, c, norm_gain, w_ada, b_ada, w_a_in, conv_w, w_a_out, w_qg, q_gain, w_o,
           kv_norm_gain, w_ada_kv, b_ada_kv, w_kv, k_gain, cmp_pe, cmp_w1, cmp_w2,
           w_mlp1, w_mlp2):
    bsz, s, d = x.shape
    assert (s, d) == (SEQ, D_MODEL)
    m = bsz * s
    x2d = x.reshape(m, d)

    def split_mod(mod, n):
        return [mod[:, k * d:(k + 1) * d].reshape(bsz, 1, d) for k in range(n)]

    sh1, sc1, g1, sh2, sc2, g2 = split_mod(_ada(c, w_ada, b_ada, 0), 6)
    gb, v = _in_proj(x2d, norm_gain[0, 0].reshape(1, d), sh1, sc1, w_a_in[0].astype(BF16))
    x2d = _conv_out(v, gb, conv_w[0], w_a_out[0].astype(BF16), x2d, g1)
    x2d = _mlp(x2d, norm_gain[0, 1].reshape(1, d), sh2, sc2, g2,
               w_mlp1[0].astype(BF16), w_mlp2[0].astype(BF16))

    sh1, sc1, g1, sh2, sc2, g2 = split_mod(_ada(c, w_ada, b_ada, 1), 6)
    sh_kv, sc_kv = split_mod(_ada(c, w_ada_kv[None], b_ada_kv[None], 0), 2)
    nq = N_HEADS * HEAD_DIM
    wq_t = w_qg[0][:, :nq].T.astype(BF16)
    per_g = HEADS_PER_GROUP * N_BRANCH
    wg = w_qg[0][:, nq:].reshape(d, N_KV_GROUPS, per_g)
    wg = jnp.pad(wg, ((0, 0), (0, 0), (0, GATE_ROWS - per_g))).reshape(d, N_KV_GROUPS * GATE_ROWS)
    qg_b = jnp.broadcast_to(q_gain[0].reshape(HEAD_DIM, 1), (HEAD_DIM, 128))
    q_t, gates_t, kv6 = _qkv_proj(x2d.reshape(bsz, s, d), norm_gain[1, 0].reshape(1, d), sh1, sc1,
                                  kv_norm_gain.reshape(1, d), sh_kv, sc_kv, wq_t, wg.T.astype(BF16),
                                  w_kv.astype(BF16), qg_b)
    eye = jnp.eye(TQ, dtype=BF16)
    pe8 = jnp.broadcast_to(cmp_pe.reshape(2, 1, CMP_BLOCK * HEAD_DIM), (2, 8, CMP_BLOCK * HEAD_DIM))
    kc, vct, ksa, vst, kwn, vwt = _kv_prep(kv6, pe8.astype(BF16), cmp_w1.astype(BF16),
                                           cmp_w2.astype(BF16), k_gain, eye)
    attn = _nsa(q_t, gates_t, kc, vct, jnp.asarray(_cmp_to_sel_t()), eye, jnp.asarray(_tri_bias()),
                jnp.asarray(_window_bias()), ksa, vst, kwn, vwt)
    x2d = _out_proj(attn.reshape(m, d), w_o[0].astype(BF16), x2d, g1)
    x2d = _mlp(x2d, norm_gain[1, 1].reshape(1, d), sh2, sc2, g2,
               w_mlp1[1].astype(BF16), w_mlp2[1].astype(BF16))
    return x2d.reshape(bsz, s, d)
```

```python
import functools

import numpy as np
import jax
import jax.numpy as jnp
from jax import lax
from jax.experimental import pallas as pl
from jax.experimental.pallas import tpu as pltpu

D_MODEL = 1024
SEQ = 2048
DEPTH = 2
N_A_LAYERS = DEPTH // 2
CONV_WIDTH = 3
D_FF = 4 * D_MODEL
HEAD_DIM = 64
N_HEADS = D_MODEL // HEAD_DIM
N_KV_GROUPS = 4
HEADS_PER_GROUP = N_HEADS // N_KV_GROUPS
N_BRANCH = 3
CMP_BLOCK = 32
CMP_STRIDE = 16
CMP_HIDDEN = 4 * HEAD_DIM
SEL_BLOCK = 64
N_SEL = 16
WINDOW = 512
EPS = 1e-6
NEG = -1e30
BIG = 1e30

N_CMP = (SEQ - CMP_BLOCK) // CMP_STRIDE + 1
N_CMP_PAD = 128
N_SELB = SEQ // SEL_BLOCK
N_TOP = min(N_SEL, N_SELB)

F32 = jnp.float32
BF16 = jnp.bfloat16

VMEM_LIMIT_BYTES = 56 * 1024 * 1024

TM_PROJ = 1024
TM_QKV = 512
TM_MLP = 1024
TF_MLP = 1024
TQ = 256
TK = 256
HALO = 8
GATE_ROWS = 16
V_ROWS = HEAD_DIM + 16
K_AUG = 128

_NT = (((1,), (1,)), ((), ()))

Q_SCALE = HEAD_DIM ** -0.5 * float(np.log2(np.e))


def _cparams(sem):
    return pltpu.CompilerParams(dimension_semantics=sem, vmem_limit_bytes=VMEM_LIMIT_BYTES)


def _norm_mod(x, gain, shift, scale):
    ms = jnp.mean(x * x, axis=-1, keepdims=True)
    y = x * lax.rsqrt(ms + EPS) * gain
    return y * (1.0 + scale) + shift


def _ada_kernel(c_ref, w_ref, b_ref, o_ref):
    c = c_ref[...]
    ca = c * jax.nn.sigmoid(c)
    o_ref[...] = jnp.dot(ca.astype(BF16), w_ref[...].astype(BF16),
                         preferred_element_type=F32) + b_ref[...]


def _ada(c, w, b, layer):
    bsz, d = c.shape
    n = w.shape[2]
    tn = 1024
    return pl.pallas_call(
        _ada_kernel,
        grid=(n // tn,),
        in_specs=[pl.BlockSpec((bsz, d), lambda j: (0, 0)),
                  pl.BlockSpec((None, d, tn), lambda j: (layer, 0, j)),
                  pl.BlockSpec((None, 1, tn), lambda j: (layer, 0, j))],
        out_specs=pl.BlockSpec((bsz, tn), lambda j: (0, j)),
        out_shape=jax.ShapeDtypeStruct((bsz, n), F32),
        compiler_params=_cparams(("arbitrary",)),
        name="ada",
    )(c, w, b.reshape(b.shape[0], 1, n))


def _inproj_kernel(x_ref, gain_ref, sh_ref, sc_ref, wb_ref, wc_ref, wu_ref,
                   gb_ref, v_ref):
    h = _norm_mod(x_ref[...], gain_ref[...], sh_ref[...], sc_ref[...]).astype(BF16)
    gb = jnp.dot(h, wb_ref[...], preferred_element_type=F32)
    gc = jnp.dot(h, wc_ref[...], preferred_element_type=F32)
    u = jnp.dot(h, wu_ref[...], preferred_element_type=F32)
    gb_ref[...] = gb.astype(BF16)
    v_ref[...] = (gc * u).astype(BF16)


def _in_proj(x2d, gain, sh, sc, w_in):
    m, d = x2d.shape
    tm = TM_PROJ
    per_b = SEQ // tm
    vec = lambda: pl.BlockSpec((None, 1, d), lambda i: (i // per_b, 0, 0))
    wcol = lambda k: pl.BlockSpec((d, d), lambda i: (0, k))
    return pl.pallas_call(
        _inproj_kernel,
        grid=(m // tm,),
        in_specs=[pl.BlockSpec((tm, d), lambda i: (i, 0)),
                  pl.BlockSpec((1, d), lambda i: (0, 0)),
                  vec(), vec(), wcol(0), wcol(1), wcol(2)],
        out_specs=[pl.BlockSpec((tm, d), lambda i: (i, 0)),
                   pl.BlockSpec((tm, d), lambda i: (i, 0))],
        out_shape=[jax.ShapeDtypeStruct((m, d), BF16), jax.ShapeDtypeStruct((m, d), BF16)],
        compiler_params=_cparams(("parallel",)),
        name="in_proj",
    )(x2d, gain, sh, sc, w_in, w_in, w_in)


def _convout_kernel(v_ref, halo_ref, gb_ref, cw_ref, w_ref, x_ref, g_ref, o_ref, vs_ref, *, per_b):
    tm = v_ref.shape[0]
    first = (pl.program_id(0) % per_b) == 0
    halo = halo_ref[...].astype(F32)
    vs_ref[0:HALO, :] = jnp.where(first, 0.0, halo)
    vs_ref[HALO:HALO + tm, :] = v_ref[...].astype(F32)
    cw = cw_ref[...]
    conv = (cw[2:3, :] * vs_ref[HALO:HALO + tm, :]
            + cw[1:2, :] * vs_ref[HALO - 1:HALO - 1 + tm, :]
            + cw[0:1, :] * vs_ref[HALO - 2:HALO - 2 + tm, :])
    a = (gb_ref[...].astype(F32) * conv).astype(BF16)
    mix = jnp.dot(a, w_ref[...], preferred_element_type=F32)
    o_ref[...] = x_ref[...] + g_ref[...] * mix


def _conv_out(v, gb, conv_w, w_out, x2d, gate):
    m, d = x2d.shape
    tm = TM_PROJ
    per_b = SEQ // tm
    hb = tm // HALO
    return pl.pallas_call(
        functools.partial(_convout_kernel, per_b=per_b),
        grid=(m // tm,),
        in_specs=[pl.BlockSpec((tm, d), lambda i: (i, 0)),
                  pl.BlockSpec((HALO, d), lambda i: (jnp.maximum(i * hb - 1, 0), 0)),
                  pl.BlockSpec((tm, d), lambda i: (i, 0)),
                  pl.BlockSpec((CONV_WIDTH, d), lambda i: (0, 0)),
                  pl.BlockSpec((d, d), lambda i: (0, 0)),
                  pl.BlockSpec((tm, d), lambda i: (i, 0)),
                  pl.BlockSpec((None, 1, d), lambda i: (i // per_b, 0, 0))],
        out_specs=pl.BlockSpec((tm, d), lambda i: (i, 0)),
        out_shape=jax.ShapeDtypeStruct((m, d), F32),
        scratch_shapes=[pltpu.VMEM((tm + HALO, d), F32)],
        compiler_params=_cparams(("parallel",)),
        name="conv_out",
    )(v, v, gb, conv_w, w_out, x2d, gate)


def _mlp_kernel(x_ref, gain_ref, sh_ref, sc_ref, g_ref, w1_ref, w2_ref, o_ref):
    @pl.when(pl.program_id(1) == 0)
    def _():
        o_ref[...] = x_ref[...]

    h = _norm_mod(x_ref[...], gain_ref[...], sh_ref[...], sc_ref[...]).astype(BF16)
    h1 = jnp.dot(h, w1_ref[...], preferred_element_type=F32)
    h1 = jnp.square(jnp.maximum(h1, 0.0)).astype(BF16)
    o_ref[...] += g_ref[...] * jnp.dot(h1, w2_ref[...], preferred_element_type=F32)


def _mlp(x2d, gain, sh, sc, gate, w1, w2):
    m, d = x2d.shape
    ff = w1.shape[1]
    tm, tf = TM_MLP, TF_MLP
    per_b = SEQ // tm
    vec = lambda: pl.BlockSpec((None, 1, d), lambda i, f: (i // per_b, 0, 0))
    return pl.pallas_call(
        _mlp_kernel,
        grid=(m // tm, ff // tf),
        in_specs=[pl.BlockSpec((tm, d), lambda i, f: (i, 0)),
                  pl.BlockSpec((1, d), lambda i, f: (0, 0)),
                  vec(), vec(), vec(),
                  pl.BlockSpec((d, tf), lambda i, f: (0, f)),
                  pl.BlockSpec((tf, d), lambda i, f: (f, 0))],
        out_specs=pl.BlockSpec((tm, d), lambda i, f: (i, 0)),
        out_shape=jax.ShapeDtypeStruct((m, d), F32),
        compiler_params=_cparams(("parallel", "arbitrary")),
        name="mlp",
    )(x2d, gain, sh, sc, gate, w1, w2)


def _qkv_kernel(x_ref, gq_ref, shq_ref, scq_ref, gkv_ref, shkv_ref, sckv_ref,
                wq_ref, wg_ref, wkv_ref, qg_ref, q_ref, gt_ref, kv_ref):
    x = x_ref[...]
    tm = x.shape[0]
    ms = jnp.mean(x * x, axis=-1, keepdims=True)
    y = x * lax.rsqrt(ms + EPS)
    hq = ((y * gq_ref[...]) * (1.0 + scq_ref[...]) + shq_ref[...]).astype(BF16)
    hkv = ((y * gkv_ref[...]) * (1.0 + sckv_ref[...]) + shkv_ref[...]).astype(BF16)
    q_t = lax.dot_general(wq_ref[...], hq, _NT, preferred_element_type=F32)
    qg = jnp.concatenate([qg_ref[...]] * (tm // 128), axis=1)
    for h in range(N_HEADS):
        xh = q_t[h * HEAD_DIM:(h + 1) * HEAD_DIM, :]
        msh = jnp.mean(xh * xh, axis=0, keepdims=True)
        q_ref[h * HEAD_DIM:(h + 1) * HEAD_DIM, :] = (xh * lax.rsqrt(msh + EPS) * qg * Q_SCALE).astype(BF16)
    gates_t = jax.nn.sigmoid(lax.dot_general(wg_ref[...], hq, _NT, preferred_element_type=F32))
    for g in range(N_KV_GROUPS):
        gt_ref[g] = gates_t[g * GATE_ROWS:(g + 1) * GATE_ROWS, :]
    kv = jnp.dot(hkv, wkv_ref[...], preferred_element_type=F32)
    for r in range(2 * N_BRANCH):
        for g in range(N_KV_GROUPS):
            c0 = (r * N_KV_GROUPS + g) * HEAD_DIM
            kv_ref[r, g] = kv[:, c0:c0 + HEAD_DIM].astype(BF16)


def _qkv_proj(x3d, gq, shq, scq, gkv, shkv, sckv, wq, wg, wkv, qg_b):
    bsz, s, d = x3d.shape
    tm = TM_QKV
    nkv = wkv.shape[1]
    vec = lambda: pl.BlockSpec((None, 1, d), lambda b, i: (b, 0, 0))
    const = lambda shape: pl.BlockSpec(shape, lambda b, i: (0,) * len(shape))
    return pl.pallas_call(
        _qkv_kernel,
        grid=(bsz, s // tm),
        in_specs=[pl.BlockSpec((None, tm, d), lambda b, i: (b, i, 0)),
                  const((1, d)), vec(), vec(),
                  const((1, d)), vec(), vec(),
                  const((d, d)), const(wg.shape), const((d, nkv)), const(qg_b.shape)],
        out_specs=[pl.BlockSpec((None, d, tm), lambda b, i: (b, 0, i)),
                   pl.BlockSpec((None, N_KV_GROUPS, GATE_ROWS, tm), lambda b, i: (b, 0, 0, i)),
                   pl.BlockSpec((2 * N_BRANCH, None, N_KV_GROUPS, tm, HEAD_DIM),
                                lambda b, i: (0, b, 0, i, 0))],
        out_shape=[jax.ShapeDtypeStruct((bsz, d, s), BF16),
                   jax.ShapeDtypeStruct((bsz, N_KV_GROUPS, GATE_ROWS, s), F32),
                   jax.ShapeDtypeStruct((2 * N_BRANCH, bsz, N_KV_GROUPS, s, HEAD_DIM), BF16)],
        compiler_params=_cparams(("parallel", "parallel")),
        name="qkv_proj",
    )(x3d, gq, shq, scq, gkv, shkv, sckv, wq, wg, wkv, qg_b)


def _head_rms(t, gain):
    ms = jnp.mean(t * t, axis=-1, keepdims=True)
    return t * lax.rsqrt(ms + EPS) * gain


def _kvprep_kernel(kcr_ref, vcr_ref, ks_ref, vs_ref, kw_ref, vw_ref,
                   pe_ref, w1_ref, w2_ref, kg_ref, eye_ref,
                   kc_ref, vct_ref, ksa_ref, vst_ref, kwn_ref, vwt_ref, tok_ref):
    half = CMP_STRIDE * HEAD_DIM
    kg = kg_ref[...]
    eye_hd = eye_ref[0:HEAD_DIM, 0:HEAD_DIM]

    def compress(src_ref, idx):
        tok_ref[...] = src_ref[...].astype(F32)
        r = jnp.concatenate([tok_ref[pl.ds(l, N_CMP_PAD, stride=CMP_STRIDE), :] for l in range(CMP_STRIDE)],
                            axis=1).astype(BF16)
        z_lo = jnp.dot(r, w1_ref[idx, 0:half, :], preferred_element_type=F32)
        z_hi = jnp.dot(r, w1_ref[idx, half:2 * half, :], preferred_element_type=F32)
        z_hi = pltpu.roll(z_hi, N_CMP_PAD - 1, 0)
        pe_b = jnp.dot(pe_ref[idx], w1_ref[idx], preferred_element_type=F32)
        hid = jax.nn.gelu(z_lo + z_hi + pe_b[0:1, :]).astype(BF16)
        return jnp.dot(hid, w2_ref[idx], preferred_element_type=F32)

    kc_ref[...] = _head_rms(compress(kcr_ref, 0), kg[0:1, :]).astype(BF16)
    vc = compress(vcr_ref, 1).astype(BF16)
    vct_ref[...] = lax.dot_general(eye_hd, vc, _NT, preferred_element_type=F32).astype(BF16)

    ksn = _head_rms(ks_ref[...].astype(F32), kg[1:2, :]).astype(BF16)
    placed = jnp.dot(ksn, eye_ref[0:HEAD_DIM, 0:K_AUG], preferred_element_type=F32)
    row = lax.broadcasted_iota(jnp.int32, (SEQ, K_AUG), 0)
    col = lax.broadcasted_iota(jnp.int32, (SEQ, K_AUG), 1)
    onehot = (col - HEAD_DIM) == lax.shift_right_logical(row, 6)
    ksa_ref[...] = jnp.where(onehot, 1.0, placed).astype(BF16)
    kwn_ref[...] = _head_rms(kw_ref[...].astype(F32), kg[2:3, :]).astype(BF16)

    ones = jnp.ones((V_ROWS - HEAD_DIM, TK), BF16)
    for src, dst in ((vs_ref, vst_ref), (vw_ref, vwt_ref)):
        for c in range(SEQ // TK):
            blk = src[c * TK:(c + 1) * TK, :]
            dst[c, 0:HEAD_DIM, :] = lax.dot_general(eye_hd, blk, _NT,
                                                    preferred_element_type=F32).astype(BF16)
            dst[c, HEAD_DIM:V_ROWS, :] = ones


def _kv_prep(kv6, pe8, w1, w2, k_gain, eye):
    _, bsz, ng, s, hd = kv6.shape
    nt = s // TK
    sel = lambda r: pl.BlockSpec((None, None, None, s, hd), lambda b, g: (r, b, g, 0, 0))
    const = lambda shape: pl.BlockSpec(shape, lambda b, g: (0,) * len(shape))
    out2 = lambda n, w: pl.BlockSpec((None, None, n, w), lambda b, g: (b, g, 0, 0))
    out3 = pl.BlockSpec((None, None, nt, V_ROWS, TK), lambda b, g: (b, g, 0, 0, 0))
    vt_shape = jax.ShapeDtypeStruct((bsz, ng, nt, V_ROWS, TK), BF16)
    return pl.pallas_call(
        _kvprep_kernel,
        grid=(bsz, ng),
        in_specs=[sel(0), sel(1), sel(2), sel(3), sel(4), sel(5),
                  const(pe8.shape), const(w1.shape), const(w2.shape), const(k_gain.shape),
                  const(eye.shape)],
        out_specs=[out2(N_CMP_PAD, hd), out2(hd, N_CMP_PAD), out2(s, K_AUG), out3, out2(s, hd), out3],
        out_shape=[jax.ShapeDtypeStruct((bsz, ng, N_CMP_PAD, hd), BF16),
                   jax.ShapeDtypeStruct((bsz, ng, hd, N_CMP_PAD), BF16),
                   jax.ShapeDtypeStruct((bsz, ng, s, K_AUG), BF16),
                   vt_shape,
                   jax.ShapeDtypeStruct((bsz, ng, s, hd), BF16),
                   vt_shape],
        scratch_shapes=[pltpu.VMEM((s, hd), F32)],
        compiler_params=_cparams(("parallel", "parallel")),
        name="kv_prep",
    )(kv6, kv6, kv6, kv6, kv6, kv6, pe8, w1, w2, k_gain, eye)


def _nsa_kernel(q_ref, gt_ref, kc_ref, vct_ref, cmapt_ref, eye_ref, tri_ref, wtri_ref,
                ksa_ref, vst_ref, kwn_ref, vwt_ref, o_ref,
                qa_ref, m_ref, a_ref, acc_ref, s_buf, p_buf):
    i = pl.program_id(2)
    t0 = i * TQ
    hg = HEADS_PER_GROUP
    lanes = hg * TQ
    eye = eye_ref[...]
    head = lambda k: slice(k * TQ, (k + 1) * TQ)

    for k in range(hg):
        qa_ref[0:HEAD_DIM, head(k)] = q_ref[k * HEAD_DIM:(k + 1) * HEAD_DIM, :]
    qn = qa_ref[0:HEAD_DIM, :]

    lane_t = t0 + (lax.broadcasted_iota(jnp.int32, (1, lanes), 1) & (TQ - 1))

    def k_rows(ref, c, n=TK):
        return ref[pl.ds(pl.multiple_of(c * TK, TK), n), :]

    n_back = WINDOW // TK
    c_w = jnp.maximum(i - n_back, 0)
    sc = jnp.dot(kc_ref[...], qn, preferred_element_type=F32)
    s_w = jnp.dot(k_rows(kwn_ref, c_w, (n_back + 1) * TK), qn, preferred_element_type=F32)

    n_id = lax.broadcasted_iota(jnp.int32, (N_CMP_PAD, lanes), 0)
    valid = (n_id * CMP_STRIDE + (CMP_BLOCK - 1)) <= lane_t
    sc = jnp.where(valid, sc, NEG)
    mc = jnp.max(sc, axis=0, keepdims=True)
    pc = jnp.where(valid, jnp.exp2(sc - mc), 0.0)
    pc = pc / jnp.maximum(jnp.sum(pc, axis=0, keepdims=True), 1e-30)
    o_cmp = jnp.dot(vct_ref[...], pc.astype(BF16), preferred_element_type=F32)

    psum = pc[:, head(0)]
    for k in range(1, hg):
        psum = psum + pc[:, head(k)]
    imp = jnp.dot(cmapt_ref[...], psum, preferred_element_type=F32,
                  precision=lax.Precision.HIGHEST)

    s_w = s_w + jnp.concatenate([wtri_ref[jnp.minimum(i, n_back)]] * hg, axis=1)
    p_w = jnp.exp2(s_w - jnp.max(s_w, axis=0, keepdims=True)).astype(BF16)
    acc_w = jnp.dot(vwt_ref[c_w], p_w[0:TK, :], preferred_element_type=F32)
    for j in range(1, n_back + 1):
        acc_w = acc_w + jnp.dot(vwt_ref[c_w + j], p_w[j * TK:(j + 1) * TK, :], preferred_element_type=F32)
    o_win = acc_w[0:HEAD_DIM, :] / jnp.maximum(acc_w[HEAD_DIM:HEAD_DIM + 1, :], 1e-30)

    tq = t0 + lax.broadcasted_iota(jnp.int32, (N_SELB, TQ), 1)
    j_id = lax.broadcasted_iota(jnp.int32, (N_SELB, TQ), 0)
    cur = lax.shift_right_logical(tq, 6)
    forced = (j_id == 0) | (j_id == cur) | (j_id == cur - 1)
    causal = j_id <= cur
    score = jnp.where(forced, BIG, jnp.where(causal, imp, NEG))
    rank = jnp.zeros((N_SELB, TQ), jnp.int32)
    for jp in range(N_SELB):
        other = score[jp:jp + 1, :]
        beats = (other > score) | ((other == score) & (j_id > jp))
        rank = rank + beats.astype(jnp.int32)
    chosen = (rank < N_TOP) & causal
    sel_bias = jnp.where(chosen, 0.0, NEG).astype(BF16)
    for k in range(hg):
        qa_ref[HEAD_DIM:HEAD_DIM + N_SELB, head(k)] = sel_bias
    qa_ref[HEAD_DIM + N_SELB:K_AUG, :] = jnp.zeros((K_AUG - HEAD_DIM - N_SELB, lanes), BF16)

    def sel_scores(c):
        return jnp.dot(k_rows(ksa_ref, c), qa_ref[...], preferred_element_type=F32)

    def softmax_update(s):
        m_old = m_ref[...]
        m_new = jnp.maximum(m_old, jnp.max(s, axis=0, keepdims=True))
        m_ref[...] = m_new
        return jnp.exp2(m_old - m_new), jnp.exp2(s - m_new).astype(BF16)

    m_ref[...] = jnp.full(m_ref.shape, NEG, F32)
    acc_ref[...] = jnp.zeros(acc_ref.shape, F32)
    a_ref[...] = jnp.ones(a_ref.shape, F32)
    p_buf[1] = jnp.zeros((TK, lanes), BF16)
    s_buf[0] = sel_scores(0)

    def pv_prev(c, slot_prev):
        pv = jnp.dot(vst_ref[jnp.maximum(c - 1, 0)], p_buf[slot_prev], preferred_element_type=F32)
        return a_ref[...] * acc_ref[...] + pv

    def sel_step(c, cur):
        nxt = 1 - cur
        acc_new = pv_prev(c, nxt)
        s_buf[nxt] = sel_scores(c + 1)
        alpha, p = softmax_update(s_buf[cur])
        p_buf[cur] = p
        acc_ref[...] = acc_new
        a_ref[...] = alpha

    def sel_pair(j, carry):
        sel_step(2 * j, 0)
        sel_step(2 * j + 1, 1)
        return carry

    lax.fori_loop(0, lax.shift_right_logical(i, 1), sel_pair, 0)

    @pl.when((i & 1) == 1)
    def _():
        sel_step(i - 1, 0)

    acc_prev = pv_prev(i, 1 - (i & 1))
    alpha, p = softmax_update(s_buf[i & 1] + jnp.concatenate([tri_ref[...]] * hg, axis=1))
    acc_sel = alpha * acc_prev + jnp.dot(vst_ref[i], p, preferred_element_type=F32)
    o_sel = acc_sel[0:HEAD_DIM, :] / jnp.maximum(acc_sel[HEAD_DIM:HEAD_DIM + 1, :], 1e-30)

    gt = gt_ref[...]
    parts = []
    for k in range(hg):
        parts.append(gt[3 * k:3 * k + 1, :] * o_cmp[:, head(k)]
                     + gt[3 * k + 1:3 * k + 2, :] * o_sel[:, head(k)]
                     + gt[3 * k + 2:3 * k + 3, :] * o_win[:, head(k)])
    o_t = jnp.concatenate(parts, axis=0).astype(BF16)
    o_ref[...] = lax.dot_general(eye, o_t, _NT, preferred_element_type=F32).astype(BF16)


def _nsa(q_t, gates_t, kc, vct, cmapt, eye, tri, wtri, ksa, vst, kwn, vwt):
    bsz, d, s = q_t.shape
    ng = N_KV_GROUPS
    gw = HEADS_PER_GROUP * HEAD_DIM
    lanes = HEADS_PER_GROUP * TQ
    nt = s // TK
    assert TQ == TK and WINDOW % TK == 0 and gw == TQ
    per_bg = lambda n, w: pl.BlockSpec((None, None, n, w), lambda b, g, i: (b, g, 0, 0))
    vt = lambda: pl.BlockSpec((None, None, nt, V_ROWS, TK), lambda b, g, i: (b, g, 0, 0, 0))
    const = lambda shape: pl.BlockSpec(shape, lambda b, g, i: (0,) * len(shape))
    return pl.pallas_call(
        _nsa_kernel,
        grid=(bsz, ng, s // TQ),
        in_specs=[pl.BlockSpec((None, gw, TQ), lambda b, g, i: (b, g, i)),
                  pl.BlockSpec((None, None, GATE_ROWS, TQ), lambda b, g, i: (b, g, 0, i)),
                  per_bg(N_CMP_PAD, HEAD_DIM), per_bg(HEAD_DIM, N_CMP_PAD),
                  const(cmapt.shape), const(eye.shape), const(tri.shape), const(wtri.shape),
                  per_bg(s, K_AUG), vt(), per_bg(s, HEAD_DIM), vt()],
        out_specs=pl.BlockSpec((None, TQ, gw), lambda b, g, i: (b, i, g)),
        out_shape=jax.ShapeDtypeStruct((bsz, s, d), BF16),
        scratch_shapes=[pltpu.VMEM((K_AUG, lanes), BF16),
                        pltpu.VMEM((1, lanes), F32),
                        pltpu.VMEM((1, lanes), F32),
                        pltpu.VMEM((V_ROWS, lanes), F32),
                        pltpu.VMEM((2, TK, lanes), F32),
                        pltpu.VMEM((2, TK, lanes), BF16)],
        compiler_params=_cparams(("parallel", "parallel", "arbitrary")),
        name="nsa",
    )(q_t, gates_t, kc, vct, cmapt, eye, tri, wtri, ksa, vst, kwn, vwt)


def _outproj_kernel(a_ref, w_ref, x_ref, g_ref, o_ref):
    mix = jnp.dot(a_ref[...], w_ref[...], preferred_element_type=F32)
    o_ref[...] = x_ref[...] + g_ref[...] * mix


def _out_proj(a2d, w, x2d, gate):
    m, d = x2d.shape
    tm = TM_PROJ
    per_b = SEQ // tm
    return pl.pallas_call(
        _outproj_kernel,
        grid=(m // tm,),
        in_specs=[pl.BlockSpec((tm, d), lambda i: (i, 0)),
                  pl.BlockSpec((d, d), lambda i: (0, 0)),
                  pl.BlockSpec((tm, d), lambda i: (i, 0)),
                  pl.BlockSpec((None, 1, d), lambda i: (i // per_b, 0, 0))],
        out_specs=pl.BlockSpec((tm, d), lambda i: (i, 0)),
        out_shape=jax.ShapeDtypeStruct((m, d), F32),
        compiler_params=_cparams(("parallel",)),
        name="out_proj",
    )(a2d, w, x2d, gate)


def _cmp_to_sel_t():
    c0 = np.arange(N_CMP_PAD)[None, :] * CMP_STRIDE
    s0 = np.arange(N_SELB)[:, None] * SEL_BLOCK
    ov = np.minimum(c0 + CMP_BLOCK, s0 + SEL_BLOCK) - np.maximum(c0, s0)
    m = (np.clip(ov, 0, None) / CMP_BLOCK).astype(np.float32)
    m[:, N_CMP:] = 0.0
    return m


def _tri_bias():
    r = np.arange(TK)[:, None]
    c = np.arange(TQ)[None, :]
    return np.where(r <= c, 0.0, NEG).astype(np.float32)


def _window_bias():
    n_back = WINDOW // TK
    r = np.arange(WINDOW + TQ)[:, None]
    c = np.arange(TQ)[None, :]
    tabs = [np.where(r <= i * TQ + c, 0.0, NEG) for i in range(n_back)]
    tabs.append(np.where((r > c) & (r <= c + WINDOW), 0.0, NEG))
    return np.stack(tabs).astype(np.float32)


def kernel(x, c, norm_gain, w_ada, b_ada, w_a_in, conv_w, w_a_out, w_qg, q_gain, w_o,
           kv_norm_gain, w_ada_kv, b_ada_kv, w_kv, k_gain, cmp_pe, cmp_w1, cmp_w2,
           w_mlp1, w_mlp2):
    bsz, s, d = x.shape
    assert (s, d) == (SEQ, D_MODEL)
    m = bsz * s
    x2d = x.reshape(m, d)

    def split_mod(mod, n):
        return [mod[:, k * d:(k + 1) * d].reshape(bsz, 1, d) for k in range(n)]

    sh1, sc1, g1, sh2, sc2, g2 = split_mod(_ada(c, w_ada, b_ada, 0), 6)
    gb, v = _in_proj(x2d, norm_gain[0, 0].reshape(1, d), sh1, sc1, w_a_in[0].astype(BF16))
    x2d = _conv_out(v, gb, conv_w[0], w_a_out[0].astype(BF16), x2d, g1)
    x2d = _mlp(x2d, norm_gain[0, 1].reshape(1, d), sh2, sc2, g2,
               w_mlp1[0].astype(BF16), w_mlp2[0].astype(BF16))

    sh1, sc1, g1, sh2, sc2, g2 = split_mod(_ada(c, w_ada, b_ada, 1), 6)
    sh_kv, sc_kv = split_mod(_ada(c, w_ada_kv[None], b_ada_kv[None], 0), 2)
    nq = N_HEADS * HEAD_DIM
    wq_t = w_qg[0][:, :nq].T.astype(BF16)
    per_g = HEADS_PER_GROUP * N_BRANCH
    wg = w_qg[0][:, nq:].reshape(d, N_KV_GROUPS, per_g)
    wg = jnp.pad(wg, ((0, 0), (0, 0), (0, GATE_ROWS - per_g))).reshape(d, N_KV_GROUPS * GATE_ROWS)
    qg_b = jnp.broadcast_to(q_gain[0].reshape(HEAD_DIM, 1), (HEAD_DIM, 128))
    q_t, gates_t, kv6 = _qkv_proj(x2d.reshape(bsz, s, d), norm_gain[1, 0].reshape(1, d), sh1, sc1,
                                  kv_norm_gain.reshape(1, d), sh_kv, sc_kv, wq_t, wg.T.astype(BF16),
                                  w_kv.astype(BF16), qg_b)
    eye = jnp.eye(TQ, dtype=BF16)
    pe8 = jnp.broadcast_to(cmp_pe.reshape(2, 1, CMP_BLOCK * HEAD_DIM), (2, 8, CMP_BLOCK * HEAD_DIM))
    kc, vct, ksa, vst, kwn, vwt = _kv_prep(kv6, pe8.astype(BF16), cmp_w1.astype(BF16),
                                           cmp_w2.astype(BF16), k_gain, eye)
    attn = _nsa(q_t, gates_t, kc, vct, jnp.asarray(_cmp_to_sel_t()), eye, jnp.asarray(_tri_bias()),
                jnp.asarray(_window_bias()), ksa, vst, kwn, vwt)
    x2d = _out_proj(attn.reshape(m, d), w_o[0].astype(BF16), x2d, g1)
    x2d = _mlp(x2d, norm_gain[1, 1].reshape(1, d), sh2, sc2, g2,
               w_mlp1[1].astype(BF16), w_mlp2[1].astype(BF16))
    return x2d.reshape(bsz, s, d)
```

```python
import functools

import numpy as np
import jax
import jax.numpy as jnp
from jax import lax
from jax.experimental import pallas as pl
from jax.experimental.pallas import tpu as pltpu

D_MODEL = 1024
SEQ = 2048
DEPTH = 2
N_A_LAYERS = DEPTH // 2
CONV_WIDTH = 3
D_FF = 4 * D_MODEL
HEAD_DIM = 64
N_HEADS = D_MODEL // HEAD_DIM
N_KV_GROUPS = 4
HEADS_PER_GROUP = N_HEADS // N_KV_GROUPS
N_BRANCH = 3
CMP_BLOCK = 32
CMP_STRIDE = 16
CMP_HIDDEN = 4 * HEAD_DIM
SEL_BLOCK = 64
N_SEL = 16
WINDOW = 512
EPS = 1e-6
NEG = -1e30
BIG = 1e30

N_CMP = (SEQ - CMP_BLOCK) // CMP_STRIDE + 1
N_CMP_PAD = 128
N_SELB = SEQ // SEL_BLOCK
N_TOP = min(N_SEL, N_SELB)

F32 = jnp.float32
BF16 = jnp.bfloat16

VMEM_LIMIT_BYTES = 56 * 1024 * 1024

TM_PROJ = 1024
TM_QKV = 512
TM_MLP = 1024
TF_MLP = 1024
TQ = 256
TK = 256
Q_TILES_PER_STEP = 2
HALO = 8
GATE_ROWS = 16
V_ROWS = HEAD_DIM + 16
K_AUG = 128

_NT = (((1,), (1,)), ((), ()))

Q_SCALE = HEAD_DIM ** -0.5 * float(np.log2(np.e))


def _cparams(sem):
    return pltpu.CompilerParams(dimension_semantics=sem, vmem_limit_bytes=VMEM_LIMIT_BYTES)


def _norm_mod(x, gain, shift, scale):
    ms = jnp.mean(x * x, axis=-1, keepdims=True)
    y = x * lax.rsqrt(ms + EPS) * gain
    return y * (1.0 + scale) + shift


def _ada_kernel(c_ref, w_ref, b_ref, o_ref):
    c = c_ref[...]
    ca = c * jax.nn.sigmoid(c)
    o_ref[...] = jnp.dot(ca.astype(BF16), w_ref[...].astype(BF16),
                         preferred_element_type=F32) + b_ref[...]


def _ada(c, w, b, layer):
    bsz, d = c.shape
    n = w.shape[2]
    tn = 1024
    return pl.pallas_call(
        _ada_kernel,
        grid=(n // tn,),
        in_specs=[pl.BlockSpec((bsz, d), lambda j: (0, 0)),
                  pl.BlockSpec((None, d, tn), lambda j: (layer, 0, j)),
                  pl.BlockSpec((None, 1, tn), lambda j: (layer, 0, j))],
        out_specs=pl.BlockSpec((bsz, tn), lambda j: (0, j)),
        out_shape=jax.ShapeDtypeStruct((bsz, n), F32),
        compiler_params=_cparams(("arbitrary",)),
        name="ada",
    )(c, w, b.reshape(b.shape[0], 1, n))


def _inproj_kernel(x_ref, gain_ref, sh_ref, sc_ref, wb_ref, wc_ref, wu_ref,
                   gb_ref, v_ref):
    h = _norm_mod(x_ref[...], gain_ref[...], sh_ref[...], sc_ref[...]).astype(BF16)
    gb = jnp.dot(h, wb_ref[...], preferred_element_type=F32)
    gc = jnp.dot(h, wc_ref[...], preferred_element_type=F32)
    u = jnp.dot(h, wu_ref[...], preferred_element_type=F32)
    gb_ref[...] = gb.astype(BF16)
    v_ref[...] = (gc * u).astype(BF16)


def _in_proj(x2d, gain, sh, sc, w_in):
    m, d = x2d.shape
    tm = TM_PROJ
    per_b = SEQ // tm
    vec = lambda: pl.BlockSpec((None, 1, d), lambda i: (i // per_b, 0, 0))
    wcol = lambda k: pl.BlockSpec((d, d), lambda i: (0, k))
    return pl.pallas_call(
        _inproj_kernel,
        grid=(m // tm,),
        in_specs=[pl.BlockSpec((tm, d), lambda i: (i, 0)),
                  pl.BlockSpec((1, d), lambda i: (0, 0)),
                  vec(), vec(), wcol(0), wcol(1), wcol(2)],
        out_specs=[pl.BlockSpec((tm, d), lambda i: (i, 0)),
                   pl.BlockSpec((tm, d), lambda i: (i, 0))],
        out_shape=[jax.ShapeDtypeStruct((m, d), BF16), jax.ShapeDtypeStruct((m, d), BF16)],
        compiler_params=_cparams(("parallel",)),
        name="in_proj",
    )(x2d, gain, sh, sc, w_in, w_in, w_in)


def _convout_kernel(v_ref, halo_ref, gb_ref, cw_ref, w_ref, x_ref, g_ref, o_ref, vs_ref, *, per_b):
    tm = v_ref.shape[0]
    first = (pl.program_id(0) % per_b) == 0
    halo = halo_ref[...].astype(F32)
    vs_ref[0:HALO, :] = jnp.where(first, 0.0, halo)
    vs_ref[HALO:HALO + tm, :] = v_ref[...].astype(F32)
    cw = cw_ref[...]
    conv = (cw[2:3, :] * vs_ref[HALO:HALO + tm, :]
            + cw[1:2, :] * vs_ref[HALO - 1:HALO - 1 + tm, :]
            + cw[0:1, :] * vs_ref[HALO - 2:HALO - 2 + tm, :])
    a = (gb_ref[...].astype(F32) * conv).astype(BF16)
    mix = jnp.dot(a, w_ref[...], preferred_element_type=F32)
    o_ref[...] = x_ref[...] + g_ref[...] * mix


def _conv_out(v, gb, conv_w, w_out, x2d, gate):
    m, d = x2d.shape
    tm = TM_PROJ
    per_b = SEQ // tm
    hb = tm // HALO
    return pl.pallas_call(
        functools.partial(_convout_kernel, per_b=per_b),
        grid=(m // tm,),
        in_specs=[pl.BlockSpec((tm, d), lambda i: (i, 0)),
                  pl.BlockSpec((HALO, d), lambda i: (jnp.maximum(i * hb - 1, 0), 0)),
                  pl.BlockSpec((tm, d), lambda i: (i, 0)),
                  pl.BlockSpec((CONV_WIDTH, d), lambda i: (0, 0)),
                  pl.BlockSpec((d, d), lambda i: (0, 0)),
                  pl.BlockSpec((tm, d), lambda i: (i, 0)),
                  pl.BlockSpec((None, 1, d), lambda i: (i // per_b, 0, 0))],
        out_specs=pl.BlockSpec((tm, d), lambda i: (i, 0)),
        out_shape=jax.ShapeDtypeStruct((m, d), F32),
        scratch_shapes=[pltpu.VMEM((tm + HALO, d), F32)],
        compiler_params=_cparams(("parallel",)),
        name="conv_out",
    )(v, v, gb, conv_w, w_out, x2d, gate)


def _mlp_kernel(x_ref, gain_ref, sh_ref, sc_ref, g_ref, w1_ref, w2_ref, o_ref):
    @pl.when(pl.program_id(1) == 0)
    def _():
        o_ref[...] = x_ref[...]

    h = _norm_mod(x_ref[...], gain_ref[...], sh_ref[...], sc_ref[...]).astype(BF16)
    h1 = jnp.dot(h, w1_ref[...], preferred_element_type=F32)
    h1 = jnp.square(jnp.maximum(h1, 0.0)).astype(BF16)
    o_ref[...] += g_ref[...] * jnp.dot(h1, w2_ref[...], preferred_element_type=F32)


def _mlp(x2d, gain, sh, sc, gate, w1, w2):
    m, d = x2d.shape
    ff = w1.shape[1]
    tm, tf = TM_MLP, TF_MLP
    per_b = SEQ // tm
    vec = lambda: pl.BlockSpec((None, 1, d), lambda i, f: (i // per_b, 0, 0))
    return pl.pallas_call(
        _mlp_kernel,
        grid=(m // tm, ff // tf),
        in_specs=[pl.BlockSpec((tm, d), lambda i, f: (i, 0)),
                  pl.BlockSpec((1, d), lambda i, f: (0, 0)),
                  vec(), vec(), vec(),
                  pl.BlockSpec((d, tf), lambda i, f: (0, f)),
                  pl.BlockSpec((tf, d), lambda i, f: (f, 0))],
        out_specs=pl.BlockSpec((tm, d), lambda i, f: (i, 0)),
        out_shape=jax.ShapeDtypeStruct((m, d), F32),
        compiler_params=_cparams(("parallel", "arbitrary")),
        name="mlp",
    )(x2d, gain, sh, sc, gate, w1, w2)


def _qkv_kernel(x_ref, gq_ref, shq_ref, scq_ref, gkv_ref, shkv_ref, sckv_ref,
                wq_ref, wg_ref, wkv_ref, qg_ref, q_ref, gt_ref, kv_ref):
    x = x_ref[...]
    tm = x.shape[0]
    ms = jnp.mean(x * x, axis=-1, keepdims=True)
    y = x * lax.rsqrt(ms + EPS)
    hq = ((y * gq_ref[...]) * (1.0 + scq_ref[...]) + shq_ref[...]).astype(BF16)
    hkv = ((y * gkv_ref[...]) * (1.0 + sckv_ref[...]) + shkv_ref[...]).astype(BF16)
    q_t = lax.dot_general(wq_ref[...], hq, _NT, preferred_element_type=F32)
    qg = jnp.concatenate([qg_ref[...]] * (tm // 128), axis=1)
    for h in range(N_HEADS):
        xh = q_t[h * HEAD_DIM:(h + 1) * HEAD_DIM, :]
        msh = jnp.mean(xh * xh, axis=0, keepdims=True)
        q_ref[h * HEAD_DIM:(h + 1) * HEAD_DIM, :] = (xh * lax.rsqrt(msh + EPS) * qg * Q_SCALE).astype(BF16)
    gates_t = jax.nn.sigmoid(lax.dot_general(wg_ref[...], hq, _NT, preferred_element_type=F32))
    for g in range(N_KV_GROUPS):
        gt_ref[g] = gates_t[g * GATE_ROWS:(g + 1) * GATE_ROWS, :]
    kv = jnp.dot(hkv, wkv_ref[...], preferred_element_type=F32)
    for r in range(2 * N_BRANCH):
        for g in range(N_KV_GROUPS):
            c0 = (r * N_KV_GROUPS + g) * HEAD_DIM
            kv_ref[r, g] = kv[:, c0:c0 + HEAD_DIM].astype(BF16)


def _qkv_proj(x3d, gq, shq, scq, gkv, shkv, sckv, wq, wg, wkv, qg_b):
    bsz, s, d = x3d.shape
    tm = TM_QKV
    nkv = wkv.shape[1]
    vec = lambda: pl.BlockSpec((None, 1, d), lambda b, i: (b, 0, 0))
    const = lambda shape: pl.BlockSpec(shape, lambda b, i: (0,) * len(shape))
    return pl.pallas_call(
        _qkv_kernel,
        grid=(bsz, s // tm),
        in_specs=[pl.BlockSpec((None, tm, d), lambda b, i: (b, i, 0)),
                  const((1, d)), vec(), vec(),
                  const((1, d)), vec(), vec(),
                  const((d, d)), const(wg.shape), const((d, nkv)), const(qg_b.shape)],
        out_specs=[pl.BlockSpec((None, d, tm), lambda b, i: (b, 0, i)),
                   pl.BlockSpec((None, N_KV_GROUPS, GATE_ROWS, tm), lambda b, i: (b, 0, 0, i)),
                   pl.BlockSpec((2 * N_BRANCH, None, N_KV_GROUPS, tm, HEAD_DIM),
                                lambda b, i: (0, b, 0, i, 0))],
        out_shape=[jax.ShapeDtypeStruct((bsz, d, s), BF16),
                   jax.ShapeDtypeStruct((bsz, N_KV_GROUPS, GATE_ROWS, s), F32),
                   jax.ShapeDtypeStruct((2 * N_BRANCH, bsz, N_KV_GROUPS, s, HEAD_DIM), BF16)],
        compiler_params=_cparams(("parallel", "parallel")),
        name="qkv_proj",
    )(x3d, gq, shq, scq, gkv, shkv, sckv, wq, wg, wkv, qg_b)


def _head_rms(t, gain):
    ms = jnp.mean(t * t, axis=-1, keepdims=True)
    return t * lax.rsqrt(ms + EPS) * gain


def _kvprep_kernel(kcr_ref, vcr_ref, ks_ref, vs_ref, kw_ref, vw_ref,
                   pe_ref, w1_ref, w2_ref, kg_ref, eye_ref,
                   kc_ref, vct_ref, ksa_ref, vst_ref, kwn_ref, vwt_ref, tok_ref):
    half = CMP_STRIDE * HEAD_DIM
    kg = kg_ref[...]
    eye_hd = eye_ref[0:HEAD_DIM, 0:HEAD_DIM]

    def compress(src_ref, idx):
        tok_ref[...] = src_ref[...].astype(F32)
        r = jnp.concatenate([tok_ref[pl.ds(l, N_CMP_PAD, stride=CMP_STRIDE), :] for l in range(CMP_STRIDE)],
                            axis=1).astype(BF16)
        z_lo = jnp.dot(r, w1_ref[idx, 0:half, :], preferred_element_type=F32)
        z_hi = jnp.dot(r, w1_ref[idx, half:2 * half, :], preferred_element_type=F32)
        z_hi = pltpu.roll(z_hi, N_CMP_PAD - 1, 0)
        pe_b = jnp.dot(pe_ref[idx], w1_ref[idx], preferred_element_type=F32)
        hid = jax.nn.gelu(z_lo + z_hi + pe_b[0:1, :]).astype(BF16)
        return jnp.dot(hid, w2_ref[idx], preferred_element_type=F32)

    kc_ref[...] = _head_rms(compress(kcr_ref, 0), kg[0:1, :]).astype(BF16)
    vc = compress(vcr_ref, 1).astype(BF16)
    vct_ref[...] = lax.dot_general(eye_hd, vc, _NT, preferred_element_type=F32).astype(BF16)

    ksn = _head_rms(ks_ref[...].astype(F32), kg[1:2, :]).astype(BF16)
    placed = jnp.dot(ksn, eye_ref[0:HEAD_DIM, 0:K_AUG], preferred_element_type=F32)
    row = lax.broadcasted_iota(jnp.int32, (SEQ, K_AUG), 0)
    col = lax.broadcasted_iota(jnp.int32, (SEQ, K_AUG), 1)
    onehot = (col - HEAD_DIM) == lax.shift_right_logical(row, 6)
    ksa_ref[...] = jnp.where(onehot, 1.0, placed).astype(BF16)
    kwn_ref[...] = _head_rms(kw_ref[...].astype(F32), kg[2:3, :]).astype(BF16)

    ones = jnp.ones((V_ROWS - HEAD_DIM, TK), BF16)
    for src, dst in ((vs_ref, vst_ref), (vw_ref, vwt_ref)):
        for c in range(SEQ // TK):
            blk = src[c * TK:(c + 1) * TK, :]
            dst[c, 0:HEAD_DIM, :] = lax.dot_general(eye_hd, blk, _NT,
                                                    preferred_element_type=F32).astype(BF16)
            dst[c, HEAD_DIM:V_ROWS, :] = ones


def _kv_prep(kv6, pe8, w1, w2, k_gain, eye):
    _, bsz, ng, s, hd = kv6.shape
    nt = s // TK
    sel = lambda r: pl.BlockSpec((None, None, None, s, hd), lambda b, g: (r, b, g, 0, 0))
    const = lambda shape: pl.BlockSpec(shape, lambda b, g: (0,) * len(shape))
    out2 = lambda n, w: pl.BlockSpec((None, None, n, w), lambda b, g: (b, g, 0, 0))
    out3 = pl.BlockSpec((None, None, nt, V_ROWS, TK), lambda b, g: (b, g, 0, 0, 0))
    vt_shape = jax.ShapeDtypeStruct((bsz, ng, nt, V_ROWS, TK), BF16)
    return pl.pallas_call(
        _kvprep_kernel,
        grid=(bsz, ng),
        in_specs=[sel(0), sel(1), sel(2), sel(3), sel(4), sel(5),
                  const(pe8.shape), const(w1.shape), const(w2.shape), const(k_gain.shape),
                  const(eye.shape)],
        out_specs=[out2(N_CMP_PAD, hd), out2(hd, N_CMP_PAD), out2(s, K_AUG), out3, out2(s, hd), out3],
        out_shape=[jax.ShapeDtypeStruct((bsz, ng, N_CMP_PAD, hd), BF16),
                   jax.ShapeDtypeStruct((bsz, ng, hd, N_CMP_PAD), BF16),
                   jax.ShapeDtypeStruct((bsz, ng, s, K_AUG), BF16),
                   vt_shape,
                   jax.ShapeDtypeStruct((bsz, ng, s, hd), BF16),
                   vt_shape],
        scratch_shapes=[pltpu.VMEM((s, hd), F32)],
        compiler_params=_cparams(("parallel", "parallel")),
        name="kv_prep",
    )(kv6, kv6, kv6, kv6, kv6, kv6, pe8, w1, w2, k_gain, eye)


def _nsa_kernel(*refs):
    for part in range(Q_TILES_PER_STEP):
        _nsa_tile(Q_TILES_PER_STEP * pl.program_id(2) + part, part % 2,
                  slice(part * TQ, (part + 1) * TQ), *refs)


def _nsa_tile(i, parity, part, q_ref, gt_ref, kc_ref, vct_ref, cmapt_ref, eye_ref, cbias_ref, tri_ref, wtri_ref,
              ksa_ref, vst_ref, kwn_ref, vwt_ref, o_ref,
              qa_ref, m_ref, a_ref, acc_ref, s_buf, p_buf):
    t0 = i * TQ
    hg = HEADS_PER_GROUP
    lanes = hg * TQ
    eye = eye_ref[...]
    head = lambda k: slice(k * TQ, (k + 1) * TQ)

    for k in range(hg):
        qa_ref[0:HEAD_DIM, head(k)] = q_ref[k * HEAD_DIM:(k + 1) * HEAD_DIM, part]
    qn = qa_ref[0:HEAD_DIM, :]

    lane_t = t0 + (lax.broadcasted_iota(jnp.int32, (1, lanes), 1) & (TQ - 1))

    def k_rows(ref, c, n=TK):
        return ref[pl.ds(pl.multiple_of(c * TK, TK), n), :]

    n_back = WINDOW // TK
    c_w = jnp.maximum(i - n_back, 0)
    sc = jnp.dot(kc_ref[...], qn, preferred_element_type=F32)
    s_w = jnp.dot(k_rows(kwn_ref, c_w, (n_back + 1) * TK), qn, preferred_element_type=F32)

    sc = sc + jnp.concatenate([cbias_ref[i]] * hg, axis=1)
    ec = jnp.exp2(sc - jnp.max(sc, axis=0, keepdims=True))
    any_valid = jnp.where(lane_t >= CMP_BLOCK - 1, 1.0, 0.0)
    pc = ec * (any_valid / jnp.maximum(jnp.sum(ec, axis=0, keepdims=True), 1e-30))
    o_cmp = jnp.dot(vct_ref[...], pc.astype(BF16), preferred_element_type=F32)

    psum = pc[:, head(0)]
    for k in range(1, hg):
        psum = psum + pc[:, head(k)]
    imp = jnp.dot(cmapt_ref[...], psum, preferred_element_type=F32,
                  precision=lax.Precision.HIGHEST)

    s_w = s_w + jnp.concatenate([wtri_ref[jnp.minimum(i, n_back)]] * hg, axis=1)
    p_w = jnp.exp2(s_w - jnp.max(s_w, axis=0, keepdims=True)).astype(BF16)
    acc_w = jnp.dot(vwt_ref[c_w], p_w[0:TK, :], preferred_element_type=F32)
    for j in range(1, n_back + 1):
        acc_w = acc_w + jnp.dot(vwt_ref[c_w + j], p_w[j * TK:(j + 1) * TK, :], preferred_element_type=F32)
    o_win = acc_w[0:HEAD_DIM, :] / jnp.maximum(acc_w[HEAD_DIM:HEAD_DIM + 1, :], 1e-30)

    tq = t0 + lax.broadcasted_iota(jnp.int32, (N_SELB, TQ), 1)
    j_id = lax.broadcasted_iota(jnp.int32, (N_SELB, TQ), 0)
    cur = lax.shift_right_logical(tq, 6)
    forced = (j_id == 0) | (j_id == cur) | (j_id == cur - 1)
    causal = j_id <= cur
    score = jnp.where(forced, BIG, jnp.where(causal, imp, NEG))
    rank = jnp.zeros((N_SELB, TQ), jnp.int32)
    for jp in range(N_SELB):
        other = score[jp:jp + 1, :]
        beats = (other > score) | ((other == score) & (j_id > jp))
        rank = rank + beats.astype(jnp.int32)
    chosen = (rank < N_TOP) & causal
    sel_bias = jnp.where(chosen, 0.0, NEG).astype(BF16)
    for k in range(hg):
        qa_ref[HEAD_DIM:HEAD_DIM + N_SELB, head(k)] = sel_bias
    qa_ref[HEAD_DIM + N_SELB:K_AUG, :] = jnp.zeros((K_AUG - HEAD_DIM - N_SELB, lanes), BF16)

    def sel_scores(c):
        return jnp.dot(k_rows(ksa_ref, c), qa_ref[...], preferred_element_type=F32)

    def softmax_update(s):
        m_old = m_ref[...]
        m_new = jnp.maximum(m_old, jnp.max(s, axis=0, keepdims=True))
        m_ref[...] = m_new
        return jnp.exp2(m_old - m_new), jnp.exp2(s - m_new).astype(BF16)

    m_ref[...] = jnp.full(m_ref.shape, NEG, F32)
    acc_ref[...] = jnp.zeros(acc_ref.shape, F32)
    a_ref[...] = jnp.ones(a_ref.shape, F32)
    p_buf[1] = jnp.zeros((TK, lanes), BF16)
    s_buf[0] = sel_scores(0)

    def pv_prev(c, slot_prev):
        pv = jnp.dot(vst_ref[jnp.maximum(c - 1, 0)], p_buf[slot_prev], preferred_element_type=F32)
        return a_ref[...] * acc_ref[...] + pv

    def sel_step(c, cur):
        nxt = 1 - cur
        acc_new = pv_prev(c, nxt)
        s_buf[nxt] = sel_scores(c + 1)
        alpha, p = softmax_update(s_buf[cur])
        p_buf[cur] = p
        acc_ref[...] = acc_new
        a_ref[...] = alpha

    def sel_pair(j, carry):
        sel_step(2 * j, 0)
        sel_step(2 * j + 1, 1)
        return carry

    lax.fori_loop(0, lax.shift_right_logical(i, 1), sel_pair, 0)

    if parity == 1:
        sel_step(i - 1, 0)

    acc_prev = pv_prev(i, 1 - parity)
    alpha, p = softmax_update(s_buf[parity] + jnp.concatenate([tri_ref[...]] * hg, axis=1))
    acc_sel = alpha * acc_prev + jnp.dot(vst_ref[i], p, preferred_element_type=F32)
    o_sel = acc_sel[0:HEAD_DIM, :] / jnp.maximum(acc_sel[HEAD_DIM:HEAD_DIM + 1, :], 1e-30)

    gt = gt_ref[:, part]
    parts = []
    for k in range(hg):
        parts.append(gt[3 * k:3 * k + 1, :] * o_cmp[:, head(k)]
                     + gt[3 * k + 1:3 * k + 2, :] * o_sel[:, head(k)]
                     + gt[3 * k + 2:3 * k + 3, :] * o_win[:, head(k)])
    o_t = jnp.concatenate(parts, axis=0).astype(BF16)
    o_ref[part, :] = lax.dot_general(eye, o_t, _NT, preferred_element_type=F32).astype(BF16)


def _nsa(q_t, gates_t, kc, vct, cmapt, eye, cbias, tri, wtri, ksa, vst, kwn, vwt):
    bsz, d, s = q_t.shape
    ng = N_KV_GROUPS
    gw = HEADS_PER_GROUP * HEAD_DIM
    lanes = HEADS_PER_GROUP * TQ
    nt = s // TK
    tq_step = Q_TILES_PER_STEP * TQ
    assert TQ == TK and WINDOW % TK == 0 and gw == TQ and Q_TILES_PER_STEP % 2 == 0
    per_bg = lambda n, w: pl.BlockSpec((None, None, n, w), lambda b, g, i: (b, g, 0, 0))
    vt = lambda: pl.BlockSpec((None, None, nt, V_ROWS, TK), lambda b, g, i: (b, g, 0, 0, 0))
    const = lambda shape: pl.BlockSpec(shape, lambda b, g, i: (0,) * len(shape))
    return pl.pallas_call(
        _nsa_kernel,
        grid=(bsz, ng, s // tq_step),
        in_specs=[pl.BlockSpec((None, gw, tq_step), lambda b, g, i: (b, g, i)),
                  pl.BlockSpec((None, None, GATE_ROWS, tq_step), lambda b, g, i: (b, g, 0, i)),
                  per_bg(N_CMP_PAD, HEAD_DIM), per_bg(HEAD_DIM, N_CMP_PAD),
                  const(cmapt.shape), const(eye.shape), const(cbias.shape), const(tri.shape),
                  const(wtri.shape),
                  per_bg(s, K_AUG), vt(), per_bg(s, HEAD_DIM), vt()],
        out_specs=pl.BlockSpec((None, tq_step, gw), lambda b, g, i: (b, i, g)),
        out_shape=jax.ShapeDtypeStruct((bsz, s, d), BF16),
        scratch_shapes=[pltpu.VMEM((K_AUG, lanes), BF16),
                        pltpu.VMEM((1, lanes), F32),
                        pltpu.VMEM((1, lanes), F32),
                        pltpu.VMEM((V_ROWS, lanes), F32),
                        pltpu.VMEM((2, TK, lanes), F32),
                        pltpu.VMEM((2, TK, lanes), BF16)],
        compiler_params=_cparams(("parallel", "parallel", "arbitrary")),
        name="nsa",
    )(q_t, gates_t, kc, vct, cmapt, eye, cbias, tri, wtri, ksa, vst, kwn, vwt)


def _outproj_kernel(a_ref, w_ref, x_ref, g_ref, o_ref):
    mix = jnp.dot(a_ref[...], w_ref[...], preferred_element_type=F32)
    o_ref[...] = x_ref[...] + g_ref[...] * mix


def _out_proj(a2d, w, x2d, gate):
    m, d = x2d.shape
    tm = TM_PROJ
    per_b = SEQ // tm
    return pl.pallas_call(
        _outproj_kernel,
        grid=(m // tm,),
        in_specs=[pl.BlockSpec((tm, d), lambda i: (i, 0)),
                  pl.BlockSpec((d, d), lambda i: (0, 0)),
                  pl.BlockSpec((tm, d), lambda i: (i, 0)),
                  pl.BlockSpec((None, 1, d), lambda i: (i // per_b, 0, 0))],
        out_specs=pl.BlockSpec((tm, d), lambda i: (i, 0)),
        out_shape=jax.ShapeDtypeStruct((m, d), F32),
        compiler_params=_cparams(("parallel",)),
        name="out_proj",
    )(a2d, w, x2d, gate)


def _cmp_to_sel_t():
    c0 = np.arange(N_CMP_PAD)[None, :] * CMP_STRIDE
    s0 = np.arange(N_SELB)[:, None] * SEL_BLOCK
    ov = np.minimum(c0 + CMP_BLOCK, s0 + SEL_BLOCK) - np.maximum(c0, s0)
    m = (np.clip(ov, 0, None) / CMP_BLOCK).astype(np.float32)
    m[:, N_CMP:] = 0.0
    return m


def _cmp_bias():
    n_end = np.arange(N_CMP_PAD)[None, :, None] * CMP_STRIDE + (CMP_BLOCK - 1)
    t = np.arange(SEQ // TQ)[:, None, None] * TQ + np.arange(TQ)[None, None, :]
    return np.where(n_end <= t, 0.0, NEG).astype(np.float32)


def _tri_bias():
    r = np.arange(TK)[:, None]
    c = np.arange(TQ)[None, :]
    return np.where(r <= c, 0.0, NEG).astype(np.float32)


def _window_bias():
    n_back = WINDOW // TK
    r = np.arange(WINDOW + TQ)[:, None]
    c = np.arange(TQ)[None, :]
    tabs = [np.where(r <= i * TQ + c, 0.0, NEG) for i in range(n_back)]
    tabs.append(np.where((r > c) & (r <= c + WINDOW), 0.0, NEG))
    return np.stack(tabs).astype(np.float32)


def kernel(x, c, norm_gain, w_ada, b_ada, w_a_in, conv_w, w_a_out, w_qg, q_gain, w_o,
           kv_norm_gain, w_ada_kv, b_ada_kv, w_kv, k_gain, cmp_pe, cmp_w1, cmp_w2,
           w_mlp1, w_mlp2):
    bsz, s, d = x.shape
    assert (s, d) == (SEQ, D_MODEL)
    m = bsz * s
    x2d = x.reshape(m, d)

    def split_mod(mod, n):
        return [mod[:, k * d:(k + 1) * d].reshape(bsz, 1, d) for k in range(n)]

    sh1, sc1, g1, sh2, sc2, g2 = split_mod(_ada(c, w_ada, b_ada, 0), 6)
    gb, v = _in_proj(x2d, norm_gain[0, 0].reshape(1, d), sh1, sc1, w_a_in[0].astype(BF16))
    x2d = _conv_out(v, gb, conv_w[0], w_a_out[0].astype(BF16), x2d, g1)
    x2d = _mlp(x2d, norm_gain[0, 1].reshape(1, d), sh2, sc2, g2,
               w_mlp1[0].astype(BF16), w_mlp2[0].astype(BF16))

    sh1, sc1, g1, sh2, sc2, g2 = split_mod(_ada(c, w_ada, b_ada, 1), 6)
    sh_kv, sc_kv = split_mod(_ada(c, w_ada_kv[None], b_ada_kv[None], 0), 2)
    nq = N_HEADS * HEAD_DIM
    wq_t = w_qg[0][:, :nq].T.astype(BF16)
    per_g = HEADS_PER_GROUP * N_BRANCH
    wg = w_qg[0][:, nq:].reshape(d, N_KV_GROUPS, per_g)
    wg = jnp.pad(wg, ((0, 0), (0, 0), (0, GATE_ROWS - per_g))).reshape(d, N_KV_GROUPS * GATE_ROWS)
    qg_b = jnp.broadcast_to(q_gain[0].reshape(HEAD_DIM, 1), (HEAD_DIM, 128))
    q_t, gates_t, kv6 = _qkv_proj(x2d.reshape(bsz, s, d), norm_gain[1, 0].reshape(1, d), sh1, sc1,
                                  kv_norm_gain.reshape(1, d), sh_kv, sc_kv, wq_t, wg.T.astype(BF16),
                                  w_kv.astype(BF16), qg_b)
    eye = jnp.eye(TQ, dtype=BF16)
    pe8 = jnp.broadcast_to(cmp_pe.reshape(2, 1, CMP_BLOCK * HEAD_DIM), (2, 8, CMP_BLOCK * HEAD_DIM))
    kc, vct, ksa, vst, kwn, vwt = _kv_prep(kv6, pe8.astype(BF16), cmp_w1.astype(BF16),
                                           cmp_w2.astype(BF16), k_gain, eye)
    attn = _nsa(q_t, gates_t, kc, vct, jnp.asarray(_cmp_to_sel_t()), eye,
                jnp.asarray(_cmp_bias()), jnp.asarray(_tri_bias()),
                jnp.asarray(_window_bias()), ksa, vst, kwn, vwt)
    x2d = _out_proj(attn.reshape(m, d), w_o[0].astype(BF16), x2d, g1)
    x2d = _mlp(x2d, norm_gain[1, 1].reshape(1, d), sh2, sc2, g2,
               w_mlp1[1].astype(BF16), w_mlp2[1].astype(BF16))
    return x2d.reshape(bsz, s, d)
```

```python
import functools

import numpy as np
import jax
import jax.numpy as jnp
from jax import lax
from jax.experimental import pallas as pl
from jax.experimental.pallas import tpu as pltpu

D_MODEL = 1024
SEQ = 2048
DEPTH = 2
N_A_LAYERS = DEPTH // 2
CONV_WIDTH = 3
D_FF = 4 * D_MODEL
HEAD_DIM = 64
N_HEADS = D_MODEL // HEAD_DIM
N_KV_GROUPS = 4
HEADS_PER_GROUP = N_HEADS // N_KV_GROUPS
N_BRANCH = 3
CMP_BLOCK = 32
CMP_STRIDE = 16
CMP_HIDDEN = 4 * HEAD_DIM
SEL_BLOCK = 64
N_SEL = 16
WINDOW = 512
EPS = 1e-6
NEG = -1e30
BIG = 1e30

N_CMP = (SEQ - CMP_BLOCK) // CMP_STRIDE + 1
N_CMP_PAD = 128
N_SELB = SEQ // SEL_BLOCK
N_TOP = min(N_SEL, N_SELB)

F32 = jnp.float32
BF16 = jnp.bfloat16

VMEM_LIMIT_BYTES = 56 * 1024 * 1024

TM_PROJ = 1024
TM_QKV = 512
TM_MLP = 1024
TF_MLP = 1024
TQ = 256
TK = 256
Q_TILES_PER_STEP = 4
HALO = 8
GATE_ROWS = 16
V_ROWS = HEAD_DIM + 16
K_AUG = 128

_NT = (((1,), (1,)), ((), ()))

Q_SCALE = HEAD_DIM ** -0.5 * float(np.log2(np.e))


def _cparams(sem):
    return pltpu.CompilerParams(dimension_semantics=sem, vmem_limit_bytes=VMEM_LIMIT_BYTES)


def _norm_mod(x, gain, shift, scale):
    ms = jnp.mean(x * x, axis=-1, keepdims=True)
    y = x * lax.rsqrt(ms + EPS) * gain
    return y * (1.0 + scale) + shift


def _ada_kernel(c_ref, w_ref, b_ref, o_ref):
    c = c_ref[...]
    ca = c * jax.nn.sigmoid(c)
    o_ref[...] = jnp.dot(ca.astype(BF16), w_ref[...].astype(BF16),
                         preferred_element_type=F32) + b_ref[...]


def _ada(c, w, b, layer):
    bsz, d = c.shape
    n = w.shape[2]
    tn = 1024
    return pl.pallas_call(
        _ada_kernel,
        grid=(n // tn,),
        in_specs=[pl.BlockSpec((bsz, d), lambda j: (0, 0)),
                  pl.BlockSpec((None, d, tn), lambda j: (layer, 0, j)),
                  pl.BlockSpec((None, 1, tn), lambda j: (layer, 0, j))],
        out_specs=pl.BlockSpec((bsz, tn), lambda j: (0, j)),
        out_shape=jax.ShapeDtypeStruct((bsz, n), F32),
        compiler_params=_cparams(("arbitrary",)),
        name="ada",
    )(c, w, b.reshape(b.shape[0], 1, n))


def _inproj_kernel(x_ref, gain_ref, sh_ref, sc_ref, wb_ref, wc_ref, wu_ref,
                   gb_ref, v_ref):
    h = _norm_mod(x_ref[...], gain_ref[...], sh_ref[...], sc_ref[...]).astype(BF16)
    gb = jnp.dot(h, wb_ref[...], preferred_element_type=F32)
    gc = jnp.dot(h, wc_ref[...], preferred_element_type=F32)
    u = jnp.dot(h, wu_ref[...], preferred_element_type=F32)
    gb_ref[...] = gb.astype(BF16)
    v_ref[...] = (gc * u).astype(BF16)


def _in_proj(x2d, gain, sh, sc, w_in):
    m, d = x2d.shape
    tm = TM_PROJ
    per_b = SEQ // tm
    vec = lambda: pl.BlockSpec((None, 1, d), lambda i: (i // per_b, 0, 0))
    wcol = lambda k: pl.BlockSpec((d, d), lambda i: (0, k))
    return pl.pallas_call(
        _inproj_kernel,
        grid=(m // tm,),
        in_specs=[pl.BlockSpec((tm, d), lambda i: (i, 0)),
                  pl.BlockSpec((1, d), lambda i: (0, 0)),
                  vec(), vec(), wcol(0), wcol(1), wcol(2)],
        out_specs=[pl.BlockSpec((tm, d), lambda i: (i, 0)),
                   pl.BlockSpec((tm, d), lambda i: (i, 0))],
        out_shape=[jax.ShapeDtypeStruct((m, d), BF16), jax.ShapeDtypeStruct((m, d), BF16)],
        compiler_params=_cparams(("parallel",)),
        name="in_proj",
    )(x2d, gain, sh, sc, w_in, w_in, w_in)


def _convout_kernel(v_ref, halo_ref, gb_ref, cw_ref, w_ref, x_ref, g_ref, o_ref, vs_ref, *, per_b):
    tm = v_ref.shape[0]
    first = (pl.program_id(0) % per_b) == 0
    halo = halo_ref[...].astype(F32)
    vs_ref[0:HALO, :] = jnp.where(first, 0.0, halo)
    vs_ref[HALO:HALO + tm, :] = v_ref[...].astype(F32)
    cw = cw_ref[...]
    conv = (cw[2:3, :] * vs_ref[HALO:HALO + tm, :]
            + cw[1:2, :] * vs_ref[HALO - 1:HALO - 1 + tm, :]
            + cw[0:1, :] * vs_ref[HALO - 2:HALO - 2 + tm, :])
    a = (gb_ref[...].astype(F32) * conv).astype(BF16)
    mix = jnp.dot(a, w_ref[...], preferred_element_type=F32)
    o_ref[...] = x_ref[...] + g_ref[...] * mix


def _conv_out(v, gb, conv_w, w_out, x2d, gate):
    m, d = x2d.shape
    tm = TM_PROJ
    per_b = SEQ // tm
    hb = tm // HALO
    return pl.pallas_call(
        functools.partial(_convout_kernel, per_b=per_b),
        grid=(m // tm,),
        in_specs=[pl.BlockSpec((tm, d), lambda i: (i, 0)),
                  pl.BlockSpec((HALO, d), lambda i: (jnp.maximum(i * hb - 1, 0), 0)),
                  pl.BlockSpec((tm, d), lambda i: (i, 0)),
                  pl.BlockSpec((CONV_WIDTH, d), lambda i: (0, 0)),
                  pl.BlockSpec((d, d), lambda i: (0, 0)),
                  pl.BlockSpec((tm, d), lambda i: (i, 0)),
                  pl.BlockSpec((None, 1, d), lambda i: (i // per_b, 0, 0))],
        out_specs=pl.BlockSpec((tm, d), lambda i: (i, 0)),
        out_shape=jax.ShapeDtypeStruct((m, d), F32),
        scratch_shapes=[pltpu.VMEM((tm + HALO, d), F32)],
        compiler_params=_cparams(("parallel",)),
        name="conv_out",
    )(v, v, gb, conv_w, w_out, x2d, gate)


def _mlp_kernel(x_ref, gain_ref, sh_ref, sc_ref, g_ref, w1_ref, w2_ref, o_ref):
    @pl.when(pl.program_id(1) == 0)
    def _():
        o_ref[...] = x_ref[...]

    h = _norm_mod(x_ref[...], gain_ref[...], sh_ref[...], sc_ref[...]).astype(BF16)
    h1 = jnp.dot(h, w1_ref[...], preferred_element_type=F32)
    h1 = jnp.square(jnp.maximum(h1, 0.0)).astype(BF16)
    o_ref[...] += g_ref[...] * jnp.dot(h1, w2_ref[...], preferred_element_type=F32)


def _mlp(x2d, gain, sh, sc, gate, w1, w2):
    m, d = x2d.shape
    ff = w1.shape[1]
    tm, tf = TM_MLP, TF_MLP
    per_b = SEQ // tm
    vec = lambda: pl.BlockSpec((None, 1, d), lambda i, f: (i // per_b, 0, 0))
    return pl.pallas_call(
        _mlp_kernel,
        grid=(m // tm, ff // tf),
        in_specs=[pl.BlockSpec((tm, d), lambda i, f: (i, 0)),
                  pl.BlockSpec((1, d), lambda i, f: (0, 0)),
                  vec(), vec(), vec(),
                  pl.BlockSpec((d, tf), lambda i, f: (0, f)),
                  pl.BlockSpec((tf, d), lambda i, f: (f, 0))],
        out_specs=pl.BlockSpec((tm, d), lambda i, f: (i, 0)),
        out_shape=jax.ShapeDtypeStruct((m, d), F32),
        compiler_params=_cparams(("parallel", "arbitrary")),
        name="mlp",
    )(x2d, gain, sh, sc, gate, w1, w2)


def _qkv_kernel(x_ref, gq_ref, shq_ref, scq_ref, gkv_ref, shkv_ref, sckv_ref,
                wq_ref, wg_ref, wkv_ref, qg_ref, q_ref, gt_ref, kv_ref):
    x = x_ref[...]
    tm = x.shape[0]
    ms = jnp.mean(x * x, axis=-1, keepdims=True)
    y = x * lax.rsqrt(ms + EPS)
    hq = ((y * gq_ref[...]) * (1.0 + scq_ref[...]) + shq_ref[...]).astype(BF16)
    hkv = ((y * gkv_ref[...]) * (1.0 + sckv_ref[...]) + shkv_ref[...]).astype(BF16)
    q_t = lax.dot_general(wq_ref[...], hq, _NT, preferred_element_type=F32)
    qg = jnp.concatenate([qg_ref[...]] * (tm // 128), axis=1)
    for h in range(N_HEADS):
        xh = q_t[h * HEAD_DIM:(h + 1) * HEAD_DIM, :]
        msh = jnp.mean(xh * xh, axis=0, keepdims=True)
        q_ref[h * HEAD_DIM:(h + 1) * HEAD_DIM, :] = (xh * lax.rsqrt(msh + EPS) * qg * Q_SCALE).astype(BF16)
    gates_t = jax.nn.sigmoid(lax.dot_general(wg_ref[...], hq, _NT, preferred_element_type=F32))
    for g in range(N_KV_GROUPS):
        gt_ref[g] = gates_t[g * GATE_ROWS:(g + 1) * GATE_ROWS, :]
    kv = jnp.dot(hkv, wkv_ref[...], preferred_element_type=F32)
    for r in range(2 * N_BRANCH):
        for g in range(N_KV_GROUPS):
            c0 = (r * N_KV_GROUPS + g) * HEAD_DIM
            kv_ref[r, g] = kv[:, c0:c0 + HEAD_DIM].astype(BF16)


def _qkv_proj(x3d, gq, shq, scq, gkv, shkv, sckv, wq, wg, wkv, qg_b):
    bsz, s, d = x3d.shape
    tm = TM_QKV
    nkv = wkv.shape[1]
    vec = lambda: pl.BlockSpec((None, 1, d), lambda b, i: (b, 0, 0))
    const = lambda shape: pl.BlockSpec(shape, lambda b, i: (0,) * len(shape))
    return pl.pallas_call(
        _qkv_kernel,
        grid=(bsz, s // tm),
        in_specs=[pl.BlockSpec((None, tm, d), lambda b, i: (b, i, 0)),
                  const((1, d)), vec(), vec(),
                  const((1, d)), vec(), vec(),
                  const((d, d)), const(wg.shape), const((d, nkv)), const(qg_b.shape)],
        out_specs=[pl.BlockSpec((None, d, tm), lambda b, i: (b, 0, i)),
                   pl.BlockSpec((None, N_KV_GROUPS, GATE_ROWS, tm), lambda b, i: (b, 0, 0, i)),
                   pl.BlockSpec((2 * N_BRANCH, None, N_KV_GROUPS, tm, HEAD_DIM),
                                lambda b, i: (0, b, 0, i, 0))],
        out_shape=[jax.ShapeDtypeStruct((bsz, d, s), BF16),
                   jax.ShapeDtypeStruct((bsz, N_KV_GROUPS, GATE_ROWS, s), F32),
                   jax.ShapeDtypeStruct((2 * N_BRANCH, bsz, N_KV_GROUPS, s, HEAD_DIM), BF16)],
        compiler_params=_cparams(("parallel", "parallel")),
        name="qkv_proj",
    )(x3d, gq, shq, scq, gkv, shkv, sckv, wq, wg, wkv, qg_b)


def _head_rms(t, gain):
    ms = jnp.mean(t * t, axis=-1, keepdims=True)
    return t * lax.rsqrt(ms + EPS) * gain


def _kvprep_kernel(kcr_ref, vcr_ref, ks_ref, vs_ref, kw_ref, vw_ref,
                   pe_ref, w1_ref, w2_ref, kg_ref, eye_ref,
                   kc_ref, vct_ref, ksa_ref, vst_ref, kwn_ref, vwt_ref, tok_ref):
    half = CMP_STRIDE * HEAD_DIM
    kg = kg_ref[...]
    eye_hd = eye_ref[0:HEAD_DIM, 0:HEAD_DIM]

    def compress(src_ref, idx):
        tok_ref[...] = src_ref[...].astype(F32)
        r = jnp.concatenate([tok_ref[pl.ds(l, N_CMP_PAD, stride=CMP_STRIDE), :] for l in range(CMP_STRIDE)],
                            axis=1).astype(BF16)
        z_lo = jnp.dot(r, w1_ref[idx, 0:half, :], preferred_element_type=F32)
        z_hi = jnp.dot(r, w1_ref[idx, half:2 * half, :], preferred_element_type=F32)
        z_hi = pltpu.roll(z_hi, N_CMP_PAD - 1, 0)
        pe_b = jnp.dot(pe_ref[idx], w1_ref[idx], preferred_element_type=F32)
        hid = jax.nn.gelu(z_lo + z_hi + pe_b[0:1, :]).astype(BF16)
        return jnp.dot(hid, w2_ref[idx], preferred_element_type=F32)

    kc_ref[...] = _head_rms(compress(kcr_ref, 0), kg[0:1, :]).astype(BF16)
    vc = compress(vcr_ref, 1).astype(BF16)
    vct_ref[...] = lax.dot_general(eye_hd, vc, _NT, preferred_element_type=F32).astype(BF16)

    ksn = _head_rms(ks_ref[...].astype(F32), kg[1:2, :]).astype(BF16)
    placed = jnp.dot(ksn, eye_ref[0:HEAD_DIM, 0:K_AUG], preferred_element_type=F32)
    row = lax.broadcasted_iota(jnp.int32, (SEQ, K_AUG), 0)
    col = lax.broadcasted_iota(jnp.int32, (SEQ, K_AUG), 1)
    onehot = (col - HEAD_DIM) == lax.shift_right_logical(row, 6)
    ksa_ref[...] = jnp.where(onehot, 1.0, placed).astype(BF16)
    kwn_ref[...] = _head_rms(kw_ref[...].astype(F32), kg[2:3, :]).astype(BF16)

    ones = jnp.ones((V_ROWS - HEAD_DIM, TK), BF16)
    for src, dst in ((vs_ref, vst_ref), (vw_ref, vwt_ref)):
        for c in range(SEQ // TK):
            blk = src[c * TK:(c + 1) * TK, :]
            dst[c, 0:HEAD_DIM, :] = lax.dot_general(eye_hd, blk, _NT,
                                                    preferred_element_type=F32).astype(BF16)
            dst[c, HEAD_DIM:V_ROWS, :] = ones


def _kv_prep(kv6, pe8, w1, w2, k_gain, eye):
    _, bsz, ng, s, hd = kv6.shape
    nt = s // TK
    sel = lambda r: pl.BlockSpec((None, None, None, s, hd), lambda b, g: (r, b, g, 0, 0))
    const = lambda shape: pl.BlockSpec(shape, lambda b, g: (0,) * len(shape))
    out2 = lambda n, w: pl.BlockSpec((None, None, n, w), lambda b, g: (b, g, 0, 0))
    out3 = pl.BlockSpec((None, None, nt, V_ROWS, TK), lambda b, g: (b, g, 0, 0, 0))
    vt_shape = jax.ShapeDtypeStruct((bsz, ng, nt, V_ROWS, TK), BF16)
    return pl.pallas_call(
        _kvprep_kernel,
        grid=(bsz, ng),
        in_specs=[sel(0), sel(1), sel(2), sel(3), sel(4), sel(5),
                  const(pe8.shape), const(w1.shape), const(w2.shape), const(k_gain.shape),
                  const(eye.shape)],
        out_specs=[out2(N_CMP_PAD, hd), out2(hd, N_CMP_PAD), out2(s, K_AUG), out3, out2(s, hd), out3],
        out_shape=[jax.ShapeDtypeStruct((bsz, ng, N_CMP_PAD, hd), BF16),
                   jax.ShapeDtypeStruct((bsz, ng, hd, N_CMP_PAD), BF16),
                   jax.ShapeDtypeStruct((bsz, ng, s, K_AUG), BF16),
                   vt_shape,
                   jax.ShapeDtypeStruct((bsz, ng, s, hd), BF16),
                   vt_shape],
        scratch_shapes=[pltpu.VMEM((s, hd), F32)],
        compiler_params=_cparams(("parallel", "parallel")),
        name="kv_prep",
    )(kv6, kv6, kv6, kv6, kv6, kv6, pe8, w1, w2, k_gain, eye)


def _nsa_kernel(*refs):
    for part in range(Q_TILES_PER_STEP):
        _nsa_tile(Q_TILES_PER_STEP * pl.program_id(2) + part, part % 2,
                  slice(part * TQ, (part + 1) * TQ), *refs)


def _nsa_tile(i, parity, part, q_ref, gt_ref, kc_ref, vct_ref, cmapt_ref, eye_ref, cbias_ref, tri_ref, wtri_ref,
              ksa_ref, vst_ref, kwn_ref, vwt_ref, o_ref,
              qa_ref, m_ref, a_ref, acc_ref, s_buf, p_buf, sw_ref):
    t0 = i * TQ
    hg = HEADS_PER_GROUP
    lanes = hg * TQ
    eye = eye_ref[...]
    head = lambda k: slice(k * TQ, (k + 1) * TQ)

    for k in range(hg):
        qa_ref[0:HEAD_DIM, head(k)] = q_ref[k * HEAD_DIM:(k + 1) * HEAD_DIM, part]
    qn = qa_ref[0:HEAD_DIM, :]

    lane_t = t0 + (lax.broadcasted_iota(jnp.int32, (1, lanes), 1) & (TQ - 1))

    def k_rows(ref, c, n=TK):
        return ref[pl.ds(pl.multiple_of(c * TK, TK), n), :]

    n_back = WINDOW // TK
    c_w = jnp.maximum(i - n_back, 0)
    sc = jnp.dot(kc_ref[...], qn, preferred_element_type=F32)
    w_bias = wtri_ref.at[jnp.minimum(i, n_back)]
    m_w = None
    for j in range(n_back + 1):
        s_j = (jnp.dot(k_rows(kwn_ref, c_w + j), qn, preferred_element_type=F32)
               + jnp.concatenate([w_bias[j * TK:(j + 1) * TK, :]] * hg, axis=1))
        sw_ref[j] = s_j
        m_j = jnp.max(s_j, axis=0, keepdims=True)
        m_w = m_j if m_w is None else jnp.maximum(m_w, m_j)

    sc = sc + jnp.concatenate([cbias_ref[i]] * hg, axis=1)
    ec = jnp.exp2(sc - jnp.max(sc, axis=0, keepdims=True))
    any_valid = jnp.where(lane_t >= CMP_BLOCK - 1, 1.0, 0.0)
    pc = ec * (any_valid / jnp.maximum(jnp.sum(ec, axis=0, keepdims=True), 1e-30))
    o_cmp = jnp.dot(vct_ref[...], pc.astype(BF16), preferred_element_type=F32)

    psum = pc[:, head(0)]
    for k in range(1, hg):
        psum = psum + pc[:, head(k)]
    imp = jnp.dot(cmapt_ref[...], psum, preferred_element_type=F32,
                  precision=lax.Precision.HIGHEST)

    acc_w = jnp.zeros((V_ROWS, lanes), F32)
    for j in range(n_back + 1):
        acc_w = acc_w + jnp.dot(vwt_ref[c_w + j], jnp.exp2(sw_ref[j] - m_w).astype(BF16),
                                preferred_element_type=F32)
    o_win = acc_w[0:HEAD_DIM, :] / jnp.maximum(acc_w[HEAD_DIM:HEAD_DIM + 1, :], 1e-30)

    tq = t0 + lax.broadcasted_iota(jnp.int32, (N_SELB, TQ), 1)
    j_id = lax.broadcasted_iota(jnp.int32, (N_SELB, TQ), 0)
    cur = lax.shift_right_logical(tq, 6)
    forced = (j_id == 0) | (j_id == cur) | (j_id == cur - 1)
    causal = j_id <= cur
    score = jnp.where(forced, BIG, jnp.where(causal, imp, NEG))
    rank = jnp.zeros((N_SELB, TQ), jnp.int32)
    for jp in range(N_SELB):
        other = score[jp:jp + 1, :]
        beats = (other > score) | ((other == score) & (j_id > jp))
        rank = rank + beats.astype(jnp.int32)
    chosen = (rank < N_TOP) & causal
    sel_bias = jnp.where(chosen, 0.0, NEG).astype(BF16)
    for k in range(hg):
        qa_ref[HEAD_DIM:HEAD_DIM + N_SELB, head(k)] = sel_bias
    qa_ref[HEAD_DIM + N_SELB:K_AUG, :] = jnp.zeros((K_AUG - HEAD_DIM - N_SELB, lanes), BF16)

    def sel_scores(c):
        return jnp.dot(k_rows(ksa_ref, c), qa_ref[...], preferred_element_type=F32)

    def softmax_update(s):
        m_old = m_ref[...]
        m_new = jnp.maximum(m_old, jnp.max(s, axis=0, keepdims=True))
        m_ref[...] = m_new
        return jnp.exp2(m_old - m_new), jnp.exp2(s - m_new).astype(BF16)

    m_ref[...] = jnp.full(m_ref.shape, NEG, F32)
    acc_ref[...] = jnp.zeros(acc_ref.shape, F32)
    a_ref[...] = jnp.ones(a_ref.shape, F32)
    p_buf[1] = jnp.zeros((TK, lanes), BF16)
    s_buf[0] = sel_scores(0)

    def pv_prev(c, slot_prev):
        pv = jnp.dot(vst_ref[jnp.maximum(c - 1, 0)], p_buf[slot_prev], preferred_element_type=F32)
        return a_ref[...] * acc_ref[...] + pv

    def sel_step(c, cur):
        nxt = 1 - cur
        acc_new = pv_prev(c, nxt)
        s_buf[nxt] = sel_scores(c + 1)
        alpha, p = softmax_update(s_buf[cur])
        p_buf[cur] = p
        acc_ref[...] = acc_new
        a_ref[...] = alpha

    def sel_pair(j, carry):
        sel_step(2 * j, 0)
        sel_step(2 * j + 1, 1)
        return carry

    lax.fori_loop(0, lax.shift_right_logical(i, 1), sel_pair, 0)

    if parity == 1:
        sel_step(i - 1, 0)

    acc_prev = pv_prev(i, 1 - parity)
    alpha, p = softmax_update(s_buf[parity] + jnp.concatenate([tri_ref[...]] * hg, axis=1))
    acc_sel = alpha * acc_prev + jnp.dot(vst_ref[i], p, preferred_element_type=F32)
    o_sel = acc_sel[0:HEAD_DIM, :] / jnp.maximum(acc_sel[HEAD_DIM:HEAD_DIM + 1, :], 1e-30)

    gt = gt_ref[:, part]
    parts = []
    for k in range(hg):
        parts.append(gt[3 * k:3 * k + 1, :] * o_cmp[:, head(k)]
                     + gt[3 * k + 1:3 * k + 2, :] * o_sel[:, head(k)]
                     + gt[3 * k + 2:3 * k + 3, :] * o_win[:, head(k)])
    o_t = jnp.concatenate(parts, axis=0).astype(BF16)
    o_ref[part, :] = lax.dot_general(eye, o_t, _NT, preferred_element_type=F32).astype(BF16)


def _nsa(q_t, gates_t, kc, vct, cmapt, eye, cbias, tri, wtri, ksa, vst, kwn, vwt):
    bsz, d, s = q_t.shape
    ng = N_KV_GROUPS
    gw = HEADS_PER_GROUP * HEAD_DIM
    lanes = HEADS_PER_GROUP * TQ
    nt = s // TK
    tq_step = Q_TILES_PER_STEP * TQ
    assert TQ == TK and WINDOW % TK == 0 and gw == TQ and Q_TILES_PER_STEP % 2 == 0
    per_bg = lambda n, w: pl.BlockSpec((None, None, n, w), lambda b, g, i: (b, g, 0, 0))
    vt = lambda: pl.BlockSpec((None, None, nt, V_ROWS, TK), lambda b, g, i: (b, g, 0, 0, 0))
    const = lambda shape: pl.BlockSpec(shape, lambda b, g, i: (0,) * len(shape))
    return pl.pallas_call(
        _nsa_kernel,
        grid=(bsz, ng, s // tq_step),
        in_specs=[pl.BlockSpec((None, gw, tq_step), lambda b, g, i: (b, g, i)),
                  pl.BlockSpec((None, None, GATE_ROWS, tq_step), lambda b, g, i: (b, g, 0, i)),
                  per_bg(N_CMP_PAD, HEAD_DIM), per_bg(HEAD_DIM, N_CMP_PAD),
                  const(cmapt.shape), const(eye.shape), const(cbias.shape), const(tri.shape),
                  const(wtri.shape),
                  per_bg(s, K_AUG), vt(), per_bg(s, HEAD_DIM), vt()],
        out_specs=pl.BlockSpec((None, tq_step, gw), lambda b, g, i: (b, i, g)),
        out_shape=jax.ShapeDtypeStruct((bsz, s, d), BF16),
        scratch_shapes=[pltpu.VMEM((K_AUG, lanes), BF16),
                        pltpu.VMEM((1, lanes), F32),
                        pltpu.VMEM((1, lanes), F32),
                        pltpu.VMEM((V_ROWS, lanes), F32),
                        pltpu.VMEM((2, TK, lanes), F32),
                        pltpu.VMEM((2, TK, lanes), BF16),
                        pltpu.VMEM((WINDOW // TK + 1, TK, lanes), F32)],
        compiler_params=_cparams(("parallel", "parallel", "arbitrary")),
        name="nsa",
    )(q_t, gates_t, kc, vct, cmapt, eye, cbias, tri, wtri, ksa, vst, kwn, vwt)


def _outproj_kernel(a_ref, w_ref, x_ref, g_ref, o_ref):
    mix = jnp.dot(a_ref[...], w_ref[...], preferred_element_type=F32)
    o_ref[...] = x_ref[...] + g_ref[...] * mix


def _out_proj(a2d, w, x2d, gate):
    m, d = x2d.shape
    tm = TM_PROJ
    per_b = SEQ // tm
    return pl.pallas_call(
        _outproj_kernel,
        grid=(m // tm,),
        in_specs=[pl.BlockSpec((tm, d), lambda i: (i, 0)),
                  pl.BlockSpec((d, d), lambda i: (0, 0)),
                  pl.BlockSpec((tm, d), lambda i: (i, 0)),
                  pl.BlockSpec((None, 1, d), lambda i: (i // per_b, 0, 0))],
        out_specs=pl.BlockSpec((tm, d), lambda i: (i, 0)),
        out_shape=jax.ShapeDtypeStruct((m, d), F32),
        compiler_params=_cparams(("parallel",)),
        name="out_proj",
    )(a2d, w, x2d, gate)


def _cmp_to_sel_t():
    c0 = np.arange(N_CMP_PAD)[None, :] * CMP_STRIDE
    s0 = np.arange(N_SELB)[:, None] * SEL_BLOCK
    ov = np.minimum(c0 + CMP_BLOCK, s0 + SEL_BLOCK) - np.maximum(c0, s0)
    m = (np.clip(ov, 0, None) / CMP_BLOCK).astype(np.float32)
    m[:, N_CMP:] = 0.0
    return m


def _cmp_bias():
    n_end = np.arange(N_CMP_PAD)[None, :, None] * CMP_STRIDE + (CMP_BLOCK - 1)
    t = np.arange(SEQ // TQ)[:, None, None] * TQ + np.arange(TQ)[None, None, :]
    return np.where(n_end <= t, 0.0, NEG).astype(np.float32)


def _tri_bias():
    r = np.arange(TK)[:, None]
    c = np.arange(TQ)[None, :]
    return np.where(r <= c, 0.0, NEG).astype(np.float32)


def _window_bias():
    n_back = WINDOW // TK
    r = np.arange(WINDOW + TQ)[:, None]
    c = np.arange(TQ)[None, :]
    tabs = [np.where(r <= i * TQ + c, 0.0, NEG) for i in range(n_back)]
    tabs.append(np.where((r > c) & (r <= c + WINDOW), 0.0, NEG))
    return np.stack(tabs).astype(np.float32)


def kernel(x, c, norm_gain, w_ada, b_ada, w_a_in, conv_w, w_a_out, w_qg, q_gain, w_o,
           kv_norm_gain, w_ada_kv, b_ada_kv, w_kv, k_gain, cmp_pe, cmp_w1, cmp_w2,
           w_mlp1, w_mlp2):
    bsz, s, d = x.shape
    assert (s, d) == (SEQ, D_MODEL)
    m = bsz * s
    x2d = x.reshape(m, d)

    def split_mod(mod, n):
        return [mod[:, k * d:(k + 1) * d].reshape(bsz, 1, d) for k in range(n)]

    sh1, sc1, g1, sh2, sc2, g2 = split_mod(_ada(c, w_ada, b_ada, 0), 6)
    gb, v = _in_proj(x2d, norm_gain[0, 0].reshape(1, d), sh1, sc1, w_a_in[0].astype(BF16))
    x2d = _conv_out(v, gb, conv_w[0], w_a_out[0].astype(BF16), x2d, g1)
    x2d = _mlp(x2d, norm_gain[0, 1].reshape(1, d), sh2, sc2, g2,
               w_mlp1[0].astype(BF16), w_mlp2[0].astype(BF16))

    sh1, sc1, g1, sh2, sc2, g2 = split_mod(_ada(c, w_ada, b_ada, 1), 6)
    sh_kv, sc_kv = split_mod(_ada(c, w_ada_kv[None], b_ada_kv[None], 0), 2)
    nq = N_HEADS * HEAD_DIM
    wq_t = w_qg[0][:, :nq].T.astype(BF16)
    per_g = HEADS_PER_GROUP * N_BRANCH
    wg = w_qg[0][:, nq:].reshape(d, N_KV_GROUPS, per_g)
    wg = jnp.pad(wg, ((0, 0), (0, 0), (0, GATE_ROWS - per_g))).reshape(d, N_KV_GROUPS * GATE_ROWS)
    qg_b = jnp.broadcast_to(q_gain[0].reshape(HEAD_DIM, 1), (HEAD_DIM, 128))
    q_t, gates_t, kv6 = _qkv_proj(x2d.reshape(bsz, s, d), norm_gain[1, 0].reshape(1, d), sh1, sc1,
                                  kv_norm_gain.reshape(1, d), sh_kv, sc_kv, wq_t, wg.T.astype(BF16),
                                  w_kv.astype(BF16), qg_b)
    eye = jnp.eye(TQ, dtype=BF16)
    pe8 = jnp.broadcast_to(cmp_pe.reshape(2, 1, CMP_BLOCK * HEAD_DIM), (2, 8, CMP_BLOCK * HEAD_DIM))
    kc, vct, ksa, vst, kwn, vwt = _kv_prep(kv6, pe8.astype(BF16), cmp_w1.astype(BF16),
                                           cmp_w2.astype(BF16), k_gain, eye)
    attn = _nsa(q_t, gates_t, kc, vct, jnp.asarray(_cmp_to_sel_t()), eye,
                jnp.asarray(_cmp_bias()), jnp.asarray(_tri_bias()),
                jnp.asarray(_window_bias()), ksa, vst, kwn, vwt)
    x2d = _out_proj(attn.reshape(m, d), w_o[0].astype(BF16), x2d, g1)
    x2d = _mlp(x2d, norm_gain[1, 1].reshape(1, d), sh2, sc2, g2,
               w_mlp1[1].astype(BF16), w_mlp2[1].astype(BF16))
    return x2d.reshape(bsz, s, d)
```

```python
import functools

import numpy as np
import jax
import jax.numpy as jnp
from jax import lax
from jax.experimental import pallas as pl
from jax.experimental.pallas import tpu as pltpu

D_MODEL = 1024
SEQ = 2048
DEPTH = 2
N_A_LAYERS = DEPTH // 2
CONV_WIDTH = 3
D_FF = 4 * D_MODEL
HEAD_DIM = 64
N_HEADS = D_MODEL // HEAD_DIM
N_KV_GROUPS = 4
HEADS_PER_GROUP = N_HEADS // N_KV_GROUPS
N_BRANCH = 3
CMP_BLOCK = 32
CMP_STRIDE = 16
CMP_HIDDEN = 4 * HEAD_DIM
SEL_BLOCK = 64
N_SEL = 16
WINDOW = 512
EPS = 1e-6
NEG = -1e30
BIG = 1e30

N_CMP = (SEQ - CMP_BLOCK) // CMP_STRIDE + 1
N_CMP_PAD = 128
N_SELB = SEQ // SEL_BLOCK
N_TOP = min(N_SEL, N_SELB)

F32 = jnp.float32
BF16 = jnp.bfloat16

VMEM_LIMIT_BYTES = 56 * 1024 * 1024

TM_PROJ = 1024
TM_QKV = 512
TM_MLP = 1024
TF_MLP = 1024
TQ = 256
TK = 256
Q_TILES_PER_STEP = 4
HALO = 8
GATE_ROWS = 16
V_ROWS = HEAD_DIM + 16
K_AUG = 128

_NT = (((1,), (1,)), ((), ()))

Q_SCALE = HEAD_DIM ** -0.5 * float(np.log2(np.e))


def _cparams(sem):
    return pltpu.CompilerParams(dimension_semantics=sem, vmem_limit_bytes=VMEM_LIMIT_BYTES)


def _norm_mod(x, gain, shift, scale):
    ms = jnp.mean(x * x, axis=-1, keepdims=True)
    y = x * lax.rsqrt(ms + EPS) * gain
    return y * (1.0 + scale) + shift


def _ada_kernel(c_ref, w_ref, b_ref, o_ref):
    c = c_ref[...]
    ca = c * jax.nn.sigmoid(c)
    o_ref[...] = jnp.dot(ca.astype(BF16), w_ref[...].astype(BF16),
                         preferred_element_type=F32) + b_ref[...]


def _ada(c, w, b, layer):
    bsz, d = c.shape
    n = w.shape[2]
    tn = 1024
    return pl.pallas_call(
        _ada_kernel,
        grid=(n // tn,),
        in_specs=[pl.BlockSpec((bsz, d), lambda j: (0, 0)),
                  pl.BlockSpec((None, d, tn), lambda j: (layer, 0, j)),
                  pl.BlockSpec((None, 1, tn), lambda j: (layer, 0, j))],
        out_specs=pl.BlockSpec((bsz, tn), lambda j: (0, j)),
        out_shape=jax.ShapeDtypeStruct((bsz, n), F32),
        compiler_params=_cparams(("arbitrary",)),
        name="ada",
    )(c, w, b.reshape(b.shape[0], 1, n))


def _inproj_kernel(x_ref, gain_ref, sh_ref, sc_ref, wb_ref, wc_ref, wu_ref,
                   gb_ref, v_ref):
    h = _norm_mod(x_ref[...], gain_ref[...], sh_ref[...], sc_ref[...]).astype(BF16)
    gb = jnp.dot(h, wb_ref[...], preferred_element_type=F32)
    gc = jnp.dot(h, wc_ref[...], preferred_element_type=F32)
    u = jnp.dot(h, wu_ref[...], preferred_element_type=F32)
    gb_ref[...] = gb.astype(BF16)
    v_ref[...] = (gc * u).astype(BF16)


def _in_proj(x2d, gain, sh, sc, w_in):
    m, d = x2d.shape
    tm = TM_PROJ
    per_b = SEQ // tm
    vec = lambda: pl.BlockSpec((None, 1, d), lambda i: (i // per_b, 0, 0))
    wcol = lambda k: pl.BlockSpec((d, d), lambda i: (0, k))
    return pl.pallas_call(
        _inproj_kernel,
        grid=(m // tm,),
        in_specs=[pl.BlockSpec((tm, d), lambda i: (i, 0)),
                  pl.BlockSpec((1, d), lambda i: (0, 0)),
                  vec(), vec(), wcol(0), wcol(1), wcol(2)],
        out_specs=[pl.BlockSpec((tm, d), lambda i: (i, 0)),
                   pl.BlockSpec((tm, d), lambda i: (i, 0))],
        out_shape=[jax.ShapeDtypeStruct((m, d), BF16), jax.ShapeDtypeStruct((m, d), BF16)],
        compiler_params=_cparams(("parallel",)),
        name="in_proj",
    )(x2d, gain, sh, sc, w_in, w_in, w_in)


def _convout_kernel(v_ref, halo_ref, gb_ref, cw_ref, w_ref, x_ref, g_ref, o_ref, vs_ref, *, per_b):
    tm = v_ref.shape[0]
    first = (pl.program_id(0) % per_b) == 0
    halo = halo_ref[...].astype(F32)
    vs_ref[0:HALO, :] = jnp.where(first, 0.0, halo)
    vs_ref[HALO:HALO + tm, :] = v_ref[...].astype(F32)
    cw = cw_ref[...]
    conv = (cw[2:3, :] * vs_ref[HALO:HALO + tm, :]
            + cw[1:2, :] * vs_ref[HALO - 1:HALO - 1 + tm, :]
            + cw[0:1, :] * vs_ref[HALO - 2:HALO - 2 + tm, :])
    a = (gb_ref[...].astype(F32) * conv).astype(BF16)
    mix = jnp.dot(a, w_ref[...], preferred_element_type=F32)
    o_ref[...] = x_ref[...] + g_ref[...] * mix


def _conv_out(v, gb, conv_w, w_out, x2d, gate):
    m, d = x2d.shape
    tm = TM_PROJ
    per_b = SEQ // tm
    hb = tm // HALO
    return pl.pallas_call(
        functools.partial(_convout_kernel, per_b=per_b),
        grid=(m // tm,),
        in_specs=[pl.BlockSpec((tm, d), lambda i: (i, 0)),
                  pl.BlockSpec((HALO, d), lambda i: (jnp.maximum(i * hb - 1, 0), 0)),
                  pl.BlockSpec((tm, d), lambda i: (i, 0)),
                  pl.BlockSpec((CONV_WIDTH, d), lambda i: (0, 0)),
                  pl.BlockSpec((d, d), lambda i: (0, 0)),
                  pl.BlockSpec((tm, d), lambda i: (i, 0)),
                  pl.BlockSpec((None, 1, d), lambda i: (i // per_b, 0, 0))],
        out_specs=pl.BlockSpec((tm, d), lambda i: (i, 0)),
        out_shape=jax.ShapeDtypeStruct((m, d), F32),
        scratch_shapes=[pltpu.VMEM((tm + HALO, d), F32)],
        compiler_params=_cparams(("parallel",)),
        name="conv_out",
    )(v, v, gb, conv_w, w_out, x2d, gate)


def _mlp_kernel(*refs, with_mixer_out):
    if with_mixer_out:
        a_ref, wo_ref, g1_ref, x_ref, gain_ref, sh_ref, sc_ref, g_ref, w1_ref, w2_ref, o_ref = refs
        x = x_ref[...] + g1_ref[...] * jnp.dot(a_ref[...], wo_ref[...], preferred_element_type=F32)
    else:
        x_ref, gain_ref, sh_ref, sc_ref, g_ref, w1_ref, w2_ref, o_ref = refs
        x = x_ref[...]
    h = _norm_mod(x, gain_ref[...], sh_ref[...], sc_ref[...]).astype(BF16)
    n_chunk = w1_ref.shape[1] // TF_MLP
    up = lambda k: jnp.dot(h, w1_ref[:, k * TF_MLP:(k + 1) * TF_MLP], preferred_element_type=F32)
    acc = None
    nxt = up(0)
    for k in range(n_chunk):
        cur = nxt
        if k + 1 < n_chunk:
            nxt = up(k + 1)
        h1 = jnp.square(jnp.maximum(cur, 0.0)).astype(BF16)
        part = jnp.dot(h1, w2_ref[k * TF_MLP:(k + 1) * TF_MLP, :], preferred_element_type=F32)
        acc = part if acc is None else acc + part
    o_ref[...] = x + g_ref[...] * acc


def _mlp(x2d, gain, sh, sc, gate, w1, w2, mixer_out=None):
    m, d = x2d.shape
    ff = w1.shape[1]
    tm = TM_MLP
    per_b = SEQ // tm
    rows = lambda: pl.BlockSpec((tm, d), lambda i: (i, 0))
    vec = lambda: pl.BlockSpec((None, 1, d), lambda i: (i // per_b, 0, 0))
    resident = lambda shape: pl.BlockSpec(shape, lambda i: (0, 0), pipeline_mode=pl.Buffered(1))
    args = [x2d, gain, sh, sc, gate, w1, w2]
    in_specs = [rows(), pl.BlockSpec((1, d), lambda i: (0, 0)), vec(), vec(), vec(),
                resident((d, ff)), resident((ff, d))]
    if mixer_out is not None:
        a2d, w_out, gate1 = mixer_out
        args = [a2d, w_out, gate1] + args
        in_specs = [rows(), resident((d, d)), vec()] + in_specs
    return pl.pallas_call(
        functools.partial(_mlp_kernel, with_mixer_out=mixer_out is not None),
        grid=(m // tm,),
        in_specs=in_specs,
        out_specs=rows(),
        out_shape=jax.ShapeDtypeStruct((m, d), F32),
        compiler_params=_cparams(("parallel",)),
        name="mlp",
    )(*args)


def _qkv_kernel(x_ref, gq_ref, shq_ref, scq_ref, gkv_ref, shkv_ref, sckv_ref,
                wq_ref, wg_ref, wkv_ref, qg_ref, q_ref, gt_ref, kv_ref):
    x = x_ref[...]
    tm = x.shape[0]
    ms = jnp.mean(x * x, axis=-1, keepdims=True)
    y = x * lax.rsqrt(ms + EPS)
    hq = ((y * gq_ref[...]) * (1.0 + scq_ref[...]) + shq_ref[...]).astype(BF16)
    hkv = ((y * gkv_ref[...]) * (1.0 + sckv_ref[...]) + shkv_ref[...]).astype(BF16)
    q_t = lax.dot_general(wq_ref[...], hq, _NT, preferred_element_type=F32)
    qg = jnp.concatenate([qg_ref[...]] * (tm // 128), axis=1)
    for h in range(N_HEADS):
        xh = q_t[h * HEAD_DIM:(h + 1) * HEAD_DIM, :]
        msh = jnp.mean(xh * xh, axis=0, keepdims=True)
        q_ref[h * HEAD_DIM:(h + 1) * HEAD_DIM, :] = (xh * lax.rsqrt(msh + EPS) * qg * Q_SCALE).astype(BF16)
    gates_t = jax.nn.sigmoid(lax.dot_general(wg_ref[...], hq, _NT, preferred_element_type=F32))
    for g in range(N_KV_GROUPS):
        gt_ref[g] = gates_t[g * GATE_ROWS:(g + 1) * GATE_ROWS, :]
    kv = jnp.dot(hkv, wkv_ref[...], preferred_element_type=F32)
    for r in range(2 * N_BRANCH):
        for g in range(N_KV_GROUPS):
            c0 = (r * N_KV_GROUPS + g) * HEAD_DIM
            kv_ref[r, g] = kv[:, c0:c0 + HEAD_DIM].astype(BF16)


def _qkv_proj(x3d, gq, shq, scq, gkv, shkv, sckv, wq, wg, wkv, qg_b):
    bsz, s, d = x3d.shape
    tm = TM_QKV
    nkv = wkv.shape[1]
    vec = lambda: pl.BlockSpec((None, 1, d), lambda b, i: (b, 0, 0))
    const = lambda shape: pl.BlockSpec(shape, lambda b, i: (0,) * len(shape))
    return pl.pallas_call(
        _qkv_kernel,
        grid=(bsz, s // tm),
        in_specs=[pl.BlockSpec((None, tm, d), lambda b, i: (b, i, 0)),
                  const((1, d)), vec(), vec(),
                  const((1, d)), vec(), vec(),
                  const((d, d)), const(wg.shape), const((d, nkv)), const(qg_b.shape)],
        out_specs=[pl.BlockSpec((None, d, tm), lambda b, i: (b, 0, i)),
                   pl.BlockSpec((None, N_KV_GROUPS, GATE_ROWS, tm), lambda b, i: (b, 0, 0, i)),
                   pl.BlockSpec((2 * N_BRANCH, None, N_KV_GROUPS, tm, HEAD_DIM),
                                lambda b, i: (0, b, 0, i, 0))],
        out_shape=[jax.ShapeDtypeStruct((bsz, d, s), BF16),
                   jax.ShapeDtypeStruct((bsz, N_KV_GROUPS, GATE_ROWS, s), F32),
                   jax.ShapeDtypeStruct((2 * N_BRANCH, bsz, N_KV_GROUPS, s, HEAD_DIM), BF16)],
        compiler_params=_cparams(("parallel", "parallel")),
        name="qkv_proj",
    )(x3d, gq, shq, scq, gkv, shkv, sckv, wq, wg, wkv, qg_b)


def _head_rms(t, gain):
    ms = jnp.mean(t * t, axis=-1, keepdims=True)
    return t * lax.rsqrt(ms + EPS) * gain


def _kvprep_kernel(kcr_ref, vcr_ref, ks_ref, vs_ref, kw_ref, vw_ref,
                   pe_ref, w1_ref, w2_ref, kg_ref, eye_ref,
                   kc_ref, vct_ref, ksa_ref, vst_ref, kwn_ref, vwt_ref, tok_ref):
    half = CMP_STRIDE * HEAD_DIM
    kg = kg_ref[...]
    eye_hd = eye_ref[0:HEAD_DIM, 0:HEAD_DIM]

    def compress(src_ref, idx):
        tok_ref[...] = src_ref[...].astype(F32)
        r = jnp.concatenate([tok_ref[pl.ds(l, N_CMP_PAD, stride=CMP_STRIDE), :] for l in range(CMP_STRIDE)],
                            axis=1).astype(BF16)
        z_lo = jnp.dot(r, w1_ref[idx, 0:half, :], preferred_element_type=F32)
        z_hi = jnp.dot(r, w1_ref[idx, half:2 * half, :], preferred_element_type=F32)
        z_hi = pltpu.roll(z_hi, N_CMP_PAD - 1, 0)
        pe_b = jnp.dot(pe_ref[idx], w1_ref[idx], preferred_element_type=F32)
        hid = jax.nn.gelu(z_lo + z_hi + pe_b[0:1, :]).astype(BF16)
        return jnp.dot(hid, w2_ref[idx], preferred_element_type=F32)

    kc_ref[...] = _head_rms(compress(kcr_ref, 0), kg[0:1, :]).astype(BF16)
    vc = compress(vcr_ref, 1).astype(BF16)
    vct_ref[...] = lax.dot_general(eye_hd, vc, _NT, preferred_element_type=F32).astype(BF16)

    ksn = _head_rms(ks_ref[...].astype(F32), kg[1:2, :]).astype(BF16)
    placed = jnp.dot(ksn, eye_ref[0:HEAD_DIM, 0:K_AUG], preferred_element_type=F32)
    row = lax.broadcasted_iota(jnp.int32, (SEQ, K_AUG), 0)
    col = lax.broadcasted_iota(jnp.int32, (SEQ, K_AUG), 1)
    onehot = (col - HEAD_DIM) == lax.shift_right_logical(row, 6)
    ksa_ref[...] = jnp.where(onehot, 1.0, placed).astype(BF16)
    kwn_ref[...] = _head_rms(kw_ref[...].astype(F32), kg[2:3, :]).astype(BF16)

    ones = jnp.ones((V_ROWS - HEAD_DIM, TK), BF16)
    for src, dst in ((vs_ref, vst_ref), (vw_ref, vwt_ref)):
        for c in range(SEQ // TK):
            blk = src[c * TK:(c + 1) * TK, :]
            dst[c, 0:HEAD_DIM, :] = lax.dot_general(eye_hd, blk, _NT,
                                                    preferred_element_type=F32).astype(BF16)
            dst[c, HEAD_DIM:V_ROWS, :] = ones


def _kv_prep(kv6, pe8, w1, w2, k_gain, eye):
    _, bsz, ng, s, hd = kv6.shape
    nt = s // TK
    sel = lambda r: pl.BlockSpec((None, None, None, s, hd), lambda b, g: (r, b, g, 0, 0))
    const = lambda shape: pl.BlockSpec(shape, lambda b, g: (0,) * len(shape))
    out2 = lambda n, w: pl.BlockSpec((None, None, n, w), lambda b, g: (b, g, 0, 0))
    out3 = pl.BlockSpec((None, None, nt, V_ROWS, TK), lambda b, g: (b, g, 0, 0, 0))
    vt_shape = jax.ShapeDtypeStruct((bsz, ng, nt, V_ROWS, TK), BF16)
    return pl.pallas_call(
        _kvprep_kernel,
        grid=(bsz, ng),
        in_specs=[sel(0), sel(1), sel(2), sel(3), sel(4), sel(5),
                  const(pe8.shape), const(w1.shape), const(w2.shape), const(k_gain.shape),
                  const(eye.shape)],
        out_specs=[out2(N_CMP_PAD, hd), out2(hd, N_CMP_PAD), out2(s, K_AUG), out3, out2(s, hd), out3],
        out_shape=[jax.ShapeDtypeStruct((bsz, ng, N_CMP_PAD, hd), BF16),
                   jax.ShapeDtypeStruct((bsz, ng, hd, N_CMP_PAD), BF16),
                   jax.ShapeDtypeStruct((bsz, ng, s, K_AUG), BF16),
                   vt_shape,
                   jax.ShapeDtypeStruct((bsz, ng, s, hd), BF16),
                   vt_shape],
        scratch_shapes=[pltpu.VMEM((s, hd), F32)],
        compiler_params=_cparams(("parallel", "parallel")),
        name="kv_prep",
    )(kv6, kv6, kv6, kv6, kv6, kv6, pe8, w1, w2, k_gain, eye)


def _nsa_kernel(*refs):
    for part in range(Q_TILES_PER_STEP):
        _nsa_tile(Q_TILES_PER_STEP * pl.program_id(2) + part, part % 2,
                  slice(part * TQ, (part + 1) * TQ), *refs)


def _nsa_tile(i, parity, part, q_ref, gt_ref, kc_ref, vct_ref, cmapt_ref, eye_ref, cbias_ref, tri_ref, wtri_ref,
              ksa_ref, vst_ref, kwn_ref, vwt_ref, o_ref,
              qa_ref, m_ref, a_ref, acc_ref, s_buf, p_buf, sw_ref):
    t0 = i * TQ
    hg = HEADS_PER_GROUP
    lanes = hg * TQ
    eye = eye_ref[...]
    head = lambda k: slice(k * TQ, (k + 1) * TQ)

    for k in range(hg):
        qa_ref[0:HEAD_DIM, head(k)] = q_ref[k * HEAD_DIM:(k + 1) * HEAD_DIM, part]
    qn = qa_ref[0:HEAD_DIM, :]

    lane_t = t0 + (lax.broadcasted_iota(jnp.int32, (1, lanes), 1) & (TQ - 1))

    def k_rows(ref, c, n=TK):
        return ref[pl.ds(pl.multiple_of(c * TK, TK), n), :]

    n_back = WINDOW // TK
    c_w = jnp.maximum(i - n_back, 0)
    sc = jnp.dot(kc_ref[...], qn, preferred_element_type=F32)
    w_bias = wtri_ref.at[jnp.minimum(i, n_back)]
    m_w = None
    for j in range(n_back + 1):
        s_j = (jnp.dot(k_rows(kwn_ref, c_w + j), qn, preferred_element_type=F32)
               + jnp.concatenate([w_bias[j * TK:(j + 1) * TK, :]] * hg, axis=1))
        sw_ref[j] = s_j
        m_j = jnp.max(s_j, axis=0, keepdims=True)
        m_w = m_j if m_w is None else jnp.maximum(m_w, m_j)

    sc = sc + jnp.concatenate([cbias_ref[i]] * hg, axis=1)
    ec = jnp.exp2(sc - jnp.max(sc, axis=0, keepdims=True))
    any_valid = jnp.where(lane_t >= CMP_BLOCK - 1, 1.0, 0.0)
    pc = ec * (any_valid / jnp.maximum(jnp.sum(ec, axis=0, keepdims=True), 1e-30))
    o_cmp = jnp.dot(vct_ref[...], pc.astype(BF16), preferred_element_type=F32)

    psum = pc[:, head(0)]
    for k in range(1, hg):
        psum = psum + pc[:, head(k)]
    imp = jnp.dot(cmapt_ref[...], psum, preferred_element_type=F32,
                  precision=lax.Precision.HIGHEST)

    acc_w = jnp.zeros((V_ROWS, lanes), F32)
    for j in range(n_back + 1):
        acc_w = acc_w + jnp.dot(vwt_ref[c_w + j], jnp.exp2(sw_ref[j] - m_w).astype(BF16),
                                preferred_element_type=F32)
    o_win = acc_w[0:HEAD_DIM, :] / jnp.maximum(acc_w[HEAD_DIM:HEAD_DIM + 1, :], 1e-30)

    tq = t0 + lax.broadcasted_iota(jnp.int32, (N_SELB, TQ), 1)
    j_id = lax.broadcasted_iota(jnp.int32, (N_SELB, TQ), 0)
    cur = lax.shift_right_logical(tq, 6)
    forced = (j_id == 0) | (j_id == cur) | (j_id == cur - 1)
    causal = j_id <= cur
    score = jnp.where(forced, BIG, jnp.where(causal, imp, NEG))
    rank = jnp.zeros((N_SELB, TQ), jnp.int32)
    for jp in range(N_SELB):
        other = score[jp:jp + 1, :]
        beats = (other > score) | ((other == score) & (j_id > jp))
        rank = rank + beats.astype(jnp.int32)
    chosen = (rank < N_TOP) & causal
    sel_bias = jnp.where(chosen, 0.0, NEG).astype(BF16)
    for k in range(hg):
        qa_ref[HEAD_DIM:HEAD_DIM + N_SELB, head(k)] = sel_bias
    qa_ref[HEAD_DIM + N_SELB:K_AUG, :] = jnp.zeros((K_AUG - HEAD_DIM - N_SELB, lanes), BF16)

    def sel_scores(c):
        return jnp.dot(k_rows(ksa_ref, c), qa_ref[...], preferred_element_type=F32)

    def softmax_update(scores):
        m_old = m_ref[...]
        m_new = jnp.maximum(m_old, jnp.max(scores(), axis=0, keepdims=True))
        m_ref[...] = m_new
        return jnp.exp2(m_old - m_new), jnp.exp2(scores() - m_new).astype(BF16)

    m_ref[...] = jnp.full(m_ref.shape, NEG, F32)
    acc_ref[...] = jnp.zeros(acc_ref.shape, F32)
    a_ref[...] = jnp.ones(a_ref.shape, F32)
    p_buf[1] = jnp.zeros((TK, lanes), BF16)
    s_buf[0] = sel_scores(0)

    def pv_prev(c, slot_prev):
        pv = jnp.dot(vst_ref[jnp.maximum(c - 1, 0)], p_buf[slot_prev], preferred_element_type=F32)
        return a_ref[...] * acc_ref[...] + pv

    def sel_step(c, cur):
        nxt = 1 - cur
        acc_ref[...] = pv_prev(c, nxt)
        s_buf[nxt] = sel_scores(c + 1)
        alpha, p = softmax_update(lambda: s_buf[cur])
        p_buf[cur] = p
        a_ref[...] = alpha

    def sel_pair(j, carry):
        sel_step(2 * j, 0)
        sel_step(2 * j + 1, 1)
        return carry

    lax.fori_loop(0, lax.shift_right_logical(i, 1), sel_pair, 0)

    if parity == 1:
        sel_step(i - 1, 0)

    acc_prev = pv_prev(i, 1 - parity)
    alpha, p = softmax_update(lambda: s_buf[parity] + jnp.concatenate([tri_ref[...]] * hg, axis=1))
    acc_sel = alpha * acc_prev + jnp.dot(vst_ref[i], p, preferred_element_type=F32)
    o_sel = acc_sel[0:HEAD_DIM, :] / jnp.maximum(acc_sel[HEAD_DIM:HEAD_DIM + 1, :], 1e-30)

    gt = gt_ref[:, part]
    parts = []
    for k in range(hg):
        parts.append(gt[3 * k:3 * k + 1, :] * o_cmp[:, head(k)]
                     + gt[3 * k + 1:3 * k + 2, :] * o_sel[:, head(k)]
                     + gt[3 * k + 2:3 * k + 3, :] * o_win[:, head(k)])
    o_t = jnp.concatenate(parts, axis=0).astype(BF16)
    o_ref[part, :] = lax.dot_general(eye, o_t, _NT, preferred_element_type=F32).astype(BF16)


def _nsa(q_t, gates_t, kc, vct, cmapt, eye, cbias, tri, wtri, ksa, vst, kwn, vwt):
    bsz, d, s = q_t.shape
    ng = N_KV_GROUPS
    gw = HEADS_PER_GROUP * HEAD_DIM
    lanes = HEADS_PER_GROUP * TQ
    nt = s // TK
    tq_step = Q_TILES_PER_STEP * TQ
    assert TQ == TK and WINDOW % TK == 0 and gw == TQ and Q_TILES_PER_STEP % 2 == 0
    per_bg = lambda n, w: pl.BlockSpec((None, None, n, w), lambda b, g, i: (b, g, 0, 0))
    vt = lambda: pl.BlockSpec((None, None, nt, V_ROWS, TK), lambda b, g, i: (b, g, 0, 0, 0))
    const = lambda shape: pl.BlockSpec(shape, lambda b, g, i: (0,) * len(shape))
    return pl.pallas_call(
        _nsa_kernel,
        grid=(bsz, ng, s // tq_step),
        in_specs=[pl.BlockSpec((None, gw, tq_step), lambda b, g, i: (b, g, i)),
                  pl.BlockSpec((None, None, GATE_ROWS, tq_step), lambda b, g, i: (b, g, 0, i)),
                  per_bg(N_CMP_PAD, HEAD_DIM), per_bg(HEAD_DIM, N_CMP_PAD),
                  const(cmapt.shape), const(eye.shape), const(cbias.shape), const(tri.shape),
                  const(wtri.shape),
                  per_bg(s, K_AUG), vt(), per_bg(s, HEAD_DIM), vt()],
        out_specs=pl.BlockSpec((None, tq_step, gw), lambda b, g, i: (b, i, g)),
        out_shape=jax.ShapeDtypeStruct((bsz, s, d), BF16),
        scratch_shapes=[pltpu.VMEM((K_AUG, lanes), BF16),
                        pltpu.VMEM((1, lanes), F32),
                        pltpu.VMEM((1, lanes), F32),
                        pltpu.VMEM((V_ROWS, lanes), F32),
                        pltpu.VMEM((2, TK, lanes), F32),
                        pltpu.VMEM((2, TK, lanes), BF16),
                        pltpu.VMEM((WINDOW // TK + 1, TK, lanes), F32)],
        compiler_params=_cparams(("parallel", "parallel", "arbitrary")),
        name="nsa",
    )(q_t, gates_t, kc, vct, cmapt, eye, cbias, tri, wtri, ksa, vst, kwn, vwt)


def _cmp_to_sel_t():
    c0 = np.arange(N_CMP_PAD)[None, :] * CMP_STRIDE
    s0 = np.arange(N_SELB)[:, None] * SEL_BLOCK
    ov = np.minimum(c0 + CMP_BLOCK, s0 + SEL_BLOCK) - np.maximum(c0, s0)
    m = (np.clip(ov, 0, None) / CMP_BLOCK).astype(np.float32)
    m[:, N_CMP:] = 0.0
    return m


def _cmp_bias():
    n_end = np.arange(N_CMP_PAD)[None, :, None] * CMP_STRIDE + (CMP_BLOCK - 1)
    t = np.arange(SEQ // TQ)[:, None, None] * TQ + np.arange(TQ)[None, None, :]
    return np.where(n_end <= t, 0.0, NEG).astype(np.float32)


def _tri_bias():
    r = np.arange(TK)[:, None]
    c = np.arange(TQ)[None, :]
    return np.where(r <= c, 0.0, NEG).astype(np.float32)


def _window_bias():
    n_back = WINDOW // TK
    r = np.arange(WINDOW + TQ)[:, None]
    c = np.arange(TQ)[None, :]
    tabs = [np.where(r <= i * TQ + c, 0.0, NEG) for i in range(n_back)]
    tabs.append(np.where((r > c) & (r <= c + WINDOW), 0.0, NEG))
    return np.stack(tabs).astype(np.float32)


def kernel(x, c, norm_gain, w_ada, b_ada, w_a_in, conv_w, w_a_out, w_qg, q_gain, w_o,
           kv_norm_gain, w_ada_kv, b_ada_kv, w_kv, k_gain, cmp_pe, cmp_w1, cmp_w2,
           w_mlp1, w_mlp2):
    bsz, s, d = x.shape
    assert (s, d) == (SEQ, D_MODEL)
    m = bsz * s
    x2d = x.reshape(m, d)

    def split_mod(mod, n):
        return [mod[:, k * d:(k + 1) * d].reshape(bsz, 1, d) for k in range(n)]

    sh1, sc1, g1, sh2, sc2, g2 = split_mod(_ada(c, w_ada, b_ada, 0), 6)
    gb, v = _in_proj(x2d, norm_gain[0, 0].reshape(1, d), sh1, sc1, w_a_in[0].astype(BF16))
    x2d = _conv_out(v, gb, conv_w[0], w_a_out[0].astype(BF16), x2d, g1)
    x2d = _mlp(x2d, norm_gain[0, 1].reshape(1, d), sh2, sc2, g2,
               w_mlp1[0].astype(BF16), w_mlp2[0].astype(BF16))

    sh1, sc1, g1, sh2, sc2, g2 = split_mod(_ada(c, w_ada, b_ada, 1), 6)
    sh_kv, sc_kv = split_mod(_ada(c, w_ada_kv[None], b_ada_kv[None], 0), 2)
    nq = N_HEADS * HEAD_DIM
    wq_t = w_qg[0][:, :nq].T.astype(BF16)
    per_g = HEADS_PER_GROUP * N_BRANCH
    wg = w_qg[0][:, nq:].reshape(d, N_KV_GROUPS, per_g)
    wg = jnp.pad(wg, ((0, 0), (0, 0), (0, GATE_ROWS - per_g))).reshape(d, N_KV_GROUPS * GATE_ROWS)
    qg_b = jnp.broadcast_to(q_gain[0].reshape(HEAD_DIM, 1), (HEAD_DIM, 128))
    q_t, gates_t, kv6 = _qkv_proj(x2d.reshape(bsz, s, d), norm_gain[1, 0].reshape(1, d), sh1, sc1,
                                  kv_norm_gain.reshape(1, d), sh_kv, sc_kv, wq_t, wg.T.astype(BF16),
                                  w_kv.astype(BF16), qg_b)
    eye = jnp.eye(TQ, dtype=BF16)
    pe8 = jnp.broadcast_to(cmp_pe.reshape(2, 1, CMP_BLOCK * HEAD_DIM), (2, 8, CMP_BLOCK * HEAD_DIM))
    kc, vct, ksa, vst, kwn, vwt = _kv_prep(kv6, pe8.astype(BF16), cmp_w1.astype(BF16),
                                           cmp_w2.astype(BF16), k_gain, eye)
    attn = _nsa(q_t, gates_t, kc, vct, jnp.asarray(_cmp_to_sel_t()), eye,
                jnp.asarray(_cmp_bias()), jnp.asarray(_tri_bias()),
                jnp.asarray(_window_bias()), ksa, vst, kwn, vwt)
    x2d = _mlp(x2d, norm_gain[1, 1].reshape(1, d), sh2, sc2, g2,
               w_mlp1[1].astype(BF16), w_mlp2[1].astype(BF16),
               mixer_out=(attn.reshape(m, d), w_o[0].astype(BF16), g1))
    return x2d.reshape(bsz, s, d)
```

```python
import functools

import numpy as np
import jax
import jax.numpy as jnp
from jax import lax
from jax.experimental import pallas as pl
from jax.experimental.pallas import tpu as pltpu

D_MODEL = 1024
SEQ = 2048
DEPTH = 2
N_A_LAYERS = DEPTH // 2
CONV_WIDTH = 3
D_FF = 4 * D_MODEL
HEAD_DIM = 64
N_HEADS = D_MODEL // HEAD_DIM
N_KV_GROUPS = 4
HEADS_PER_GROUP = N_HEADS // N_KV_GROUPS
N_BRANCH = 3
CMP_BLOCK = 32
CMP_STRIDE = 16
CMP_HIDDEN = 4 * HEAD_DIM
SEL_BLOCK = 64
N_SEL = 16
WINDOW = 512
EPS = 1e-6
NEG = -1e30
BIG = 1e30

N_CMP = (SEQ - CMP_BLOCK) // CMP_STRIDE + 1
N_CMP_PAD = 128
N_SELB = SEQ // SEL_BLOCK
N_TOP = min(N_SEL, N_SELB)

F32 = jnp.float32
BF16 = jnp.bfloat16

VMEM_LIMIT_BYTES = 56 * 1024 * 1024

TM_PROJ = 1024
TN_SCONV = 512
TM_QKV = 512
TM_MLP = 1024
TF_MLP = 1024
TQ = 256
TK = 256
Q_TILES_PER_STEP = 4
HALO = 8
GATE_ROWS = 16
V_ROWS = HEAD_DIM + 16
K_AUG = 128

_NT = (((1,), (1,)), ((), ()))

Q_SCALE = HEAD_DIM ** -0.5 * float(np.log2(np.e))


def _cparams(sem):
    return pltpu.CompilerParams(dimension_semantics=sem, vmem_limit_bytes=VMEM_LIMIT_BYTES)


def _norm_mod(x, gain, shift, scale):
    ms = jnp.mean(x * x, axis=-1, keepdims=True)
    y = x * lax.rsqrt(ms + EPS) * gain
    return y * (1.0 + scale) + shift


def _ada_kernel(c_ref, w_ref, b_ref, o_ref):
    c = c_ref[...]
    ca = c * jax.nn.sigmoid(c)
    o_ref[...] = jnp.dot(ca.astype(BF16), w_ref[...].astype(BF16),
                         preferred_element_type=F32) + b_ref[...]


def _ada(c, w, b, layer):
    bsz, d = c.shape
    n = w.shape[2]
    tn = 1024
    return pl.pallas_call(
        _ada_kernel,
        grid=(n // tn,),
        in_specs=[pl.BlockSpec((bsz, d), lambda j: (0, 0)),
                  pl.BlockSpec((None, d, tn), lambda j: (layer, 0, j)),
                  pl.BlockSpec((None, 1, tn), lambda j: (layer, 0, j))],
        out_specs=pl.BlockSpec((bsz, tn), lambda j: (0, j)),
        out_shape=jax.ShapeDtypeStruct((bsz, n), F32),
        compiler_params=_cparams(("arbitrary",)),
        name="ada",
    )(c, w, b.reshape(b.shape[0], 1, n))


def _sconv_kernel(x_ref, gain_ref, sh_ref, sc_ref, wb_ref, wc_ref, wu_ref, cw_ref, wo_ref, g_ref,
                  o_ref, vs_ref, *, per_b):
    tm, d = x_ref.shape
    x = x_ref[...]
    h = _norm_mod(x, gain_ref[...], sh_ref[...], sc_ref[...]).astype(BF16)

    @pl.when(pl.program_id(0) % per_b == 0)
    def _():
        vs_ref[0:HALO, :] = jnp.zeros((HALO, d), F32)

    cw = cw_ref[...]
    n_chunk = d // TN_SCONV
    proj = lambda w_ref, c: jnp.dot(h, w_ref[:, c * TN_SCONV:(c + 1) * TN_SCONV], preferred_element_type=F32)
    trip = lambda c: (proj(wb_ref, c), proj(wc_ref, c), proj(wu_ref, c))
    mix = None
    nxt = trip(0)
    for c in range(n_chunk):
        gb, gc, u = nxt
        if c + 1 < n_chunk:
            nxt = trip(c + 1)
        cols = slice(c * TN_SCONV, (c + 1) * TN_SCONV)
        vs_ref[HALO:HALO + tm, cols] = gc * u
        conv = (cw[2:3, cols] * vs_ref[HALO:HALO + tm, cols]
                + cw[1:2, cols] * vs_ref[HALO - 1:HALO - 1 + tm, cols]
                + cw[0:1, cols] * vs_ref[HALO - 2:HALO - 2 + tm, cols])
        part = jnp.dot((gb * conv).astype(BF16), wo_ref[cols, :], preferred_element_type=F32)
        mix = part if mix is None else mix + part
    o_ref[...] = x + g_ref[...] * mix
    vs_ref[0:HALO, :] = vs_ref[tm:tm + HALO, :]


def _short_conv(x2d, gain, sh, sc, w_in, conv_w, w_out, gate):
    m, d = x2d.shape
    tm = TM_PROJ
    per_b = SEQ // tm
    rows = lambda: pl.BlockSpec((tm, d), lambda i: (i, 0))
    vec = lambda: pl.BlockSpec((None, 1, d), lambda i: (i // per_b, 0, 0))
    wcol = lambda k: pl.BlockSpec((d, d), lambda i: (0, k), pipeline_mode=pl.Buffered(1))
    return pl.pallas_call(
        functools.partial(_sconv_kernel, per_b=per_b),
        grid=(m // tm,),
        in_specs=[rows(), pl.BlockSpec((1, d), lambda i: (0, 0)), vec(), vec(),
                  wcol(0), wcol(1), wcol(2),
                  pl.BlockSpec((CONV_WIDTH, d), lambda i: (0, 0)),
                  pl.BlockSpec((d, d), lambda i: (0, 0), pipeline_mode=pl.Buffered(1)),
                  vec()],
        out_specs=rows(),
        out_shape=jax.ShapeDtypeStruct((m, d), F32),
        scratch_shapes=[pltpu.VMEM((tm + HALO, d), F32)],
        compiler_params=_cparams(("arbitrary",)),
        name="short_conv",
    )(x2d, gain, sh, sc, w_in, w_in, w_in, conv_w, w_out, gate)


def _mlp_kernel(*refs, with_mixer_out):
    if with_mixer_out:
        a_ref, wo_ref, g1_ref, x_ref, gain_ref, sh_ref, sc_ref, g_ref, w1_ref, w2_ref, o_ref = refs
        x = x_ref[...] + g1_ref[...] * jnp.dot(a_ref[...], wo_ref[...], preferred_element_type=F32)
    else:
        x_ref, gain_ref, sh_ref, sc_ref, g_ref, w1_ref, w2_ref, o_ref = refs
        x = x_ref[...]
    h = _norm_mod(x, gain_ref[...], sh_ref[...], sc_ref[...]).astype(BF16)
    n_chunk = w1_ref.shape[1] // TF_MLP
    up = lambda k: jnp.dot(h, w1_ref[:, k * TF_MLP:(k + 1) * TF_MLP], preferred_element_type=F32)
    acc = None
    nxt = up(0)
    for k in range(n_chunk):
        cur = nxt
        if k + 1 < n_chunk:
            nxt = up(k + 1)
        h1 = jnp.square(jnp.maximum(cur, 0.0)).astype(BF16)
        part = jnp.dot(h1, w2_ref[k * TF_MLP:(k + 1) * TF_MLP, :], preferred_element_type=F32)
        acc = part if acc is None else acc + part
    o_ref[...] = x + g_ref[...] * acc


def _mlp(x2d, gain, sh, sc, gate, w1, w2, mixer_out=None):
    m, d = x2d.shape
    ff = w1.shape[1]
    tm = TM_MLP
    per_b = SEQ // tm
    rows = lambda: pl.BlockSpec((tm, d), lambda i: (i, 0))
    vec = lambda: pl.BlockSpec((None, 1, d), lambda i: (i // per_b, 0, 0))
    resident = lambda shape: pl.BlockSpec(shape, lambda i: (0, 0), pipeline_mode=pl.Buffered(1))
    args = [x2d, gain, sh, sc, gate, w1, w2]
    in_specs = [rows(), pl.BlockSpec((1, d), lambda i: (0, 0)), vec(), vec(), vec(),
                resident((d, ff)), resident((ff, d))]
    if mixer_out is not None:
        a2d, w_out, gate1 = mixer_out
        args = [a2d, w_out, gate1] + args
        in_specs = [rows(), resident((d, d)), vec()] + in_specs
    return pl.pallas_call(
        functools.partial(_mlp_kernel, with_mixer_out=mixer_out is not None),
        grid=(m // tm,),
        in_specs=in_specs,
        out_specs=rows(),
        out_shape=jax.ShapeDtypeStruct((m, d), F32),
        compiler_params=_cparams(("parallel",)),
        name="mlp",
    )(*args)


def _qkv_kernel(x_ref, gq_ref, shq_ref, scq_ref, gkv_ref, shkv_ref, sckv_ref,
                wq_ref, wg_ref, wkv_ref, qg_ref, q_ref, gt_ref, kv_ref):
    x = x_ref[...]
    tm = x.shape[0]
    ms = jnp.mean(x * x, axis=-1, keepdims=True)
    y = x * lax.rsqrt(ms + EPS)
    hq = ((y * gq_ref[...]) * (1.0 + scq_ref[...]) + shq_ref[...]).astype(BF16)
    hkv = ((y * gkv_ref[...]) * (1.0 + sckv_ref[...]) + shkv_ref[...]).astype(BF16)
    q_t = lax.dot_general(wq_ref[...], hq, _NT, preferred_element_type=F32)
    qg = jnp.concatenate([qg_ref[...]] * (tm // 128), axis=1)
    for h in range(N_HEADS):
        xh = q_t[h * HEAD_DIM:(h + 1) * HEAD_DIM, :]
        msh = jnp.mean(xh * xh, axis=0, keepdims=True)
        q_ref[h * HEAD_DIM:(h + 1) * HEAD_DIM, :] = (xh * lax.rsqrt(msh + EPS) * qg * Q_SCALE).astype(BF16)
    gates_t = jax.nn.sigmoid(lax.dot_general(wg_ref[...], hq, _NT, preferred_element_type=F32))
    for g in range(N_KV_GROUPS):
        gt_ref[g] = gates_t[g * GATE_ROWS:(g + 1) * GATE_ROWS, :]
    kv = jnp.dot(hkv, wkv_ref[...], preferred_element_type=F32)
    for r in range(2 * N_BRANCH):
        for g in range(N_KV_GROUPS):
            c0 = (r * N_KV_GROUPS + g) * HEAD_DIM
            kv_ref[r, g] = kv[:, c0:c0 + HEAD_DIM].astype(BF16)


def _qkv_proj(x3d, gq, shq, scq, gkv, shkv, sckv, wq, wg, wkv, qg_b):
    bsz, s, d = x3d.shape
    tm = TM_QKV
    nkv = wkv.shape[1]
    vec = lambda: pl.BlockSpec((None, 1, d), lambda b, i: (b, 0, 0))
    const = lambda shape: pl.BlockSpec(shape, lambda b, i: (0,) * len(shape))
    return pl.pallas_call(
        _qkv_kernel,
        grid=(bsz, s // tm),
        in_specs=[pl.BlockSpec((None, tm, d), lambda b, i: (b, i, 0)),
                  const((1, d)), vec(), vec(),
                  const((1, d)), vec(), vec(),
                  const((d, d)), const(wg.shape), const((d, nkv)), const(qg_b.shape)],
        out_specs=[pl.BlockSpec((None, d, tm), lambda b, i: (b, 0, i)),
                   pl.BlockSpec((None, N_KV_GROUPS, GATE_ROWS, tm), lambda b, i: (b, 0, 0, i)),
                   pl.BlockSpec((2 * N_BRANCH, None, N_KV_GROUPS, tm, HEAD_DIM),
                                lambda b, i: (0, b, 0, i, 0))],
        out_shape=[jax.ShapeDtypeStruct((bsz, d, s), BF16),
                   jax.ShapeDtypeStruct((bsz, N_KV_GROUPS, GATE_ROWS, s), F32),
                   jax.ShapeDtypeStruct((2 * N_BRANCH, bsz, N_KV_GROUPS, s, HEAD_DIM), BF16)],
        compiler_params=_cparams(("parallel", "parallel")),
        name="qkv_proj",
    )(x3d, gq, shq, scq, gkv, shkv, sckv, wq, wg, wkv, qg_b)


def _head_rms(t, gain):
    ms = jnp.mean(t * t, axis=-1, keepdims=True)
    return t * lax.rsqrt(ms + EPS) * gain


def _kvprep_kernel(kcr_ref, vcr_ref, ks_ref, vs_ref, kw_ref, vw_ref,
                   pe_ref, w1_ref, w2_ref, kg_ref, eye_ref,
                   kc_ref, vct_ref, ksa_ref, vst_ref, kwn_ref, vwt_ref, tok_ref):
    half = CMP_STRIDE * HEAD_DIM
    kg = kg_ref[...]
    eye_hd = eye_ref[0:HEAD_DIM, 0:HEAD_DIM]

    def compress(src_ref, idx):
        tok_ref[...] = src_ref[...].astype(F32)
        r = jnp.concatenate([tok_ref[pl.ds(l, N_CMP_PAD, stride=CMP_STRIDE), :] for l in range(CMP_STRIDE)],
                            axis=1).astype(BF16)
        z_lo = jnp.dot(r, w1_ref[idx, 0:half, :], preferred_element_type=F32)
        z_hi = jnp.dot(r, w1_ref[idx, half:2 * half, :], preferred_element_type=F32)
        z_hi = pltpu.roll(z_hi, N_CMP_PAD - 1, 0)
        pe_b = jnp.dot(pe_ref[idx], w1_ref[idx], preferred_element_type=F32)
        hid = jax.nn.gelu(z_lo + z_hi + pe_b[0:1, :]).astype(BF16)
        return jnp.dot(hid, w2_ref[idx], preferred_element_type=F32)

    kc_ref[...] = _head_rms(compress(kcr_ref, 0), kg[0:1, :]).astype(BF16)
    vc = compress(vcr_ref, 1).astype(BF16)
    vct_ref[...] = lax.dot_general(eye_hd, vc, _NT, preferred_element_type=F32).astype(BF16)

    ksn = _head_rms(ks_ref[...].astype(F32), kg[1:2, :]).astype(BF16)
    placed = jnp.dot(ksn, eye_ref[0:HEAD_DIM, 0:K_AUG], preferred_element_type=F32)
    row = lax.broadcasted_iota(jnp.int32, (SEQ, K_AUG), 0)
    col = lax.broadcasted_iota(jnp.int32, (SEQ, K_AUG), 1)
    onehot = (col - HEAD_DIM) == lax.shift_right_logical(row, 6)
    ksa_ref[...] = jnp.where(onehot, 1.0, placed).astype(BF16)
    kwn_ref[...] = _head_rms(kw_ref[...].astype(F32), kg[2:3, :]).astype(BF16)

    ones = jnp.ones((V_ROWS - HEAD_DIM, TK), BF16)
    for src, dst in ((vs_ref, vst_ref), (vw_ref, vwt_ref)):
        for c in range(SEQ // TK):
            blk = src[c * TK:(c + 1) * TK, :]
            dst[c, 0:HEAD_DIM, :] = lax.dot_general(eye_hd, blk, _NT,
                                                    preferred_element_type=F32).astype(BF16)
            dst[c, HEAD_DIM:V_ROWS, :] = ones


def _kv_prep(kv6, pe8, w1, w2, k_gain, eye):
    _, bsz, ng, s, hd = kv6.shape
    nt = s // TK
    sel = lambda r: pl.BlockSpec((None, None, None, s, hd), lambda b, g: (r, b, g, 0, 0))
    const = lambda shape: pl.BlockSpec(shape, lambda b, g: (0,) * len(shape))
    out2 = lambda n, w: pl.BlockSpec((None, None, n, w), lambda b, g: (b, g, 0, 0))
    out3 = pl.BlockSpec((None, None, nt, V_ROWS, TK), lambda b, g: (b, g, 0, 0, 0))
    vt_shape = jax.ShapeDtypeStruct((bsz, ng, nt, V_ROWS, TK), BF16)
    return pl.pallas_call(
        _kvprep_kernel,
        grid=(bsz, ng),
        in_specs=[sel(0), sel(1), sel(2), sel(3), sel(4), sel(5),
                  const(pe8.shape), const(w1.shape), const(w2.shape), const(k_gain.shape),
                  const(eye.shape)],
        out_specs=[out2(N_CMP_PAD, hd), out2(hd, N_CMP_PAD), out2(s, K_AUG), out3, out2(s, hd), out3],
        out_shape=[jax.ShapeDtypeStruct((bsz, ng, N_CMP_PAD, hd), BF16),
                   jax.ShapeDtypeStruct((bsz, ng, hd, N_CMP_PAD), BF16),
                   jax.ShapeDtypeStruct((bsz, ng, s, K_AUG), BF16),
                   vt_shape,
                   jax.ShapeDtypeStruct((bsz, ng, s, hd), BF16),
                   vt_shape],
        scratch_shapes=[pltpu.VMEM((s, hd), F32)],
        compiler_params=_cparams(("parallel", "parallel")),
        name="kv_prep",
    )(kv6, kv6, kv6, kv6, kv6, kv6, pe8, w1, w2, k_gain, eye)


def _nsa_kernel(*refs):
    for part in range(Q_TILES_PER_STEP):
        _nsa_tile(Q_TILES_PER_STEP * pl.program_id(2) + part, part % 2,
                  slice(part * TQ, (part + 1) * TQ), *refs)


def _nsa_tile(i, parity, part, q_ref, gt_ref, kc_ref, vct_ref, cmapt_ref, eye_ref, cbias_ref, tri_ref, wtri_ref,
              ksa_ref, vst_ref, kwn_ref, vwt_ref, o_ref,
              qa_ref, m_ref, a_ref, acc_ref, s_buf, p_buf, sw_ref):
    t0 = i * TQ
    hg = HEADS_PER_GROUP
    lanes = hg * TQ
    eye = eye_ref[...]
    head = lambda k: slice(k * TQ, (k + 1) * TQ)

    for k in range(hg):
        qa_ref[0:HEAD_DIM, head(k)] = q_ref[k * HEAD_DIM:(k + 1) * HEAD_DIM, part]
    qn = qa_ref[0:HEAD_DIM, :]

    lane_t = t0 + (lax.broadcasted_iota(jnp.int32, (1, lanes), 1) & (TQ - 1))

    def k_rows(ref, c, n=TK):
        return ref[pl.ds(pl.multiple_of(c * TK, TK), n), :]

    n_back = WINDOW // TK
    c_w = jnp.maximum(i - n_back, 0)
    sc = jnp.dot(kc_ref[...], qn, preferred_element_type=F32)
    w_bias = wtri_ref.at[jnp.minimum(i, n_back)]
    m_w = None
    for j in range(n_back + 1):
        s_j = (jnp.dot(k_rows(kwn_ref, c_w + j), qn, preferred_element_type=F32)
               + jnp.concatenate([w_bias[j * TK:(j + 1) * TK, :]] * hg, axis=1))
        sw_ref[j] = s_j
        m_j = jnp.max(s_j, axis=0, keepdims=True)
        m_w = m_j if m_w is None else jnp.maximum(m_w, m_j)

    sc = sc + jnp.concatenate([cbias_ref[i]] * hg, axis=1)
    ec = jnp.exp2(sc - jnp.max(sc, axis=0, keepdims=True))
    any_valid = jnp.where(lane_t >= CMP_BLOCK - 1, 1.0, 0.0)
    pc = ec * (any_valid / jnp.maximum(jnp.sum(ec, axis=0, keepdims=True), 1e-30))
    o_cmp = jnp.dot(vct_ref[...], pc.astype(BF16), preferred_element_type=F32)

    psum = pc[:, head(0)]
    for k in range(1, hg):
        psum = psum + pc[:, head(k)]
    imp = jnp.dot(cmapt_ref[...], psum, preferred_element_type=F32,
                  precision=lax.Precision.HIGHEST)

    acc_w = jnp.zeros((V_ROWS, lanes), F32)
    for j in range(n_back + 1):
        acc_w = acc_w + jnp.dot(vwt_ref[c_w + j], jnp.exp2(sw_ref[j] - m_w).astype(BF16),
                                preferred_element_type=F32)
    o_win = acc_w[0:HEAD_DIM, :] / jnp.maximum(acc_w[HEAD_DIM:HEAD_DIM + 1, :], 1e-30)

    tq = t0 + lax.broadcasted_iota(jnp.int32, (N_SELB, TQ), 1)
    j_id = lax.broadcasted_iota(jnp.int32, (N_SELB, TQ), 0)
    cur = lax.shift_right_logical(tq, 6)
    forced = (j_id == 0) | (j_id == cur) | (j_id == cur - 1)
    causal = j_id <= cur
    score = jnp.where(forced, BIG, jnp.where(causal, imp, NEG))
    rank = jnp.zeros((N_SELB, TQ), jnp.int32)
    for jp in range(N_SELB):
        other = score[jp:jp + 1, :]
        beats = (other > score) | ((other == score) & (j_id > jp))
        rank = rank + beats.astype(jnp.int32)
    chosen = (rank < N_TOP) & causal
    sel_bias = jnp.where(chosen, 0.0, NEG).astype(BF16)
    for k in range(hg):
        qa_ref[HEAD_DIM:HEAD_DIM + N_SELB, head(k)] = sel_bias
    qa_ref[HEAD_DIM + N_SELB:K_AUG, :] = jnp.zeros((K_AUG - HEAD_DIM - N_SELB, lanes), BF16)

    def sel_scores(c):
        return jnp.dot(k_rows(ksa_ref, c), qa_ref[...], preferred_element_type=F32)

    def softmax_update(scores):
        m_old = m_ref[...]
        m_new = jnp.maximum(m_old, jnp.max(scores(), axis=0, keepdims=True))
        m_ref[...] = m_new
        return jnp.exp2(m_old - m_new), jnp.exp2(scores() - m_new).astype(BF16)

    m_ref[...] = jnp.full(m_ref.shape, NEG, F32)
    acc_ref[...] = jnp.zeros(acc_ref.shape, F32)
    a_ref[...] = jnp.ones(a_ref.shape, F32)
    p_buf[1] = jnp.zeros((TK, lanes), BF16)
    s_buf[0] = sel_scores(0)

    def pv_prev(c, slot_prev):
        pv = jnp.dot(vst_ref[jnp.maximum(c - 1, 0)], p_buf[slot_prev], preferred_element_type=F32)
        return a_ref[...] * acc_ref[...] + pv

    def sel_step(c, cur):
        nxt = 1 - cur
        acc_ref[...] = pv_prev(c, nxt)
        s_buf[nxt] = sel_scores(c + 1)
        alpha, p = softmax_update(lambda: s_buf[cur])
        p_buf[cur] = p
        a_ref[...] = alpha

    def sel_pair(j, carry):
        sel_step(2 * j, 0)
        sel_step(2 * j + 1, 1)
        return carry

    lax.fori_loop(0, lax.shift_right_logical(i, 1), sel_pair, 0)

    if parity == 1:
        sel_step(i - 1, 0)

    acc_prev = pv_prev(i, 1 - parity)
    alpha, p = softmax_update(lambda: s_buf[parity] + jnp.concatenate([tri_ref[...]] * hg, axis=1))
    acc_sel = alpha * acc_prev + jnp.dot(vst_ref[i], p, preferred_element_type=F32)
    o_sel = acc_sel[0:HEAD_DIM, :] / jnp.maximum(acc_sel[HEAD_DIM:HEAD_DIM + 1, :], 1e-30)

    gt = gt_ref[:, part]
    parts = []
    for k in range(hg):
        parts.append(gt[3 * k:3 * k + 1, :] * o_cmp[:, head(k)]
                     + gt[3 * k + 1:3 * k + 2, :] * o_sel[:, head(k)]
                     + gt[3 * k + 2:3 * k + 3, :] * o_win[:, head(k)])
    o_t = jnp.concatenate(parts, axis=0).astype(BF16)
    o_ref[part, :] = lax.dot_general(eye, o_t, _NT, preferred_element_type=F32).astype(BF16)


def _nsa(q_t, gates_t, kc, vct, cmapt, eye, cbias, tri, wtri, ksa, vst, kwn, vwt):
    bsz, d, s = q_t.shape
    ng = N_KV_GROUPS
    gw = HEADS_PER_GROUP * HEAD_DIM
    lanes = HEADS_PER_GROUP * TQ
    nt = s // TK
    tq_step = Q_TILES_PER_STEP * TQ
    assert TQ == TK and WINDOW % TK == 0 and gw == TQ and Q_TILES_PER_STEP % 2 == 0
    per_bg = lambda n, w: pl.BlockSpec((None, None, n, w), lambda b, g, i: (b, g, 0, 0))
    vt = lambda: pl.BlockSpec((None, None, nt, V_ROWS, TK), lambda b, g, i: (b, g, 0, 0, 0))
    const = lambda shape: pl.BlockSpec(shape, lambda b, g, i: (0,) * len(shape))
    return pl.pallas_call(
        _nsa_kernel,
        grid=(bsz, ng, s // tq_step),
        in_specs=[pl.BlockSpec((None, gw, tq_step), lambda b, g, i: (b, g, i)),
                  pl.BlockSpec((None, None, GATE_ROWS, tq_step), lambda b, g, i: (b, g, 0, i)),
                  per_bg(N_CMP_PAD, HEAD_DIM), per_bg(HEAD_DIM, N_CMP_PAD),
                  const(cmapt.shape), const(eye.shape), const(cbias.shape), const(tri.shape),
                  const(wtri.shape),
                  per_bg(s, K_AUG), vt(), per_bg(s, HEAD_DIM), vt()],
        out_specs=pl.BlockSpec((None, tq_step, gw), lambda b, g, i: (b, i, g)),
        out_shape=jax.ShapeDtypeStruct((bsz, s, d), BF16),
        scratch_shapes=[pltpu.VMEM((K_AUG, lanes), BF16),
                        pltpu.VMEM((1, lanes), F32),
                        pltpu.VMEM((1, lanes), F32),
                        pltpu.VMEM((V_ROWS, lanes), F32),
                        pltpu.VMEM((2, TK, lanes), F32),
                        pltpu.VMEM((2, TK, lanes), BF16),
                        pltpu.VMEM((WINDOW // TK + 1, TK, lanes), F32)],
        compiler_params=_cparams(("parallel", "parallel", "arbitrary")),
        name="nsa",
    )(q_t, gates_t, kc, vct, cmapt, eye, cbias, tri, wtri, ksa, vst, kwn, vwt)


def _cmp_to_sel_t():
    c0 = np.arange(N_CMP_PAD)[None, :] * CMP_STRIDE
    s0 = np.arange(N_SELB)[:, None] * SEL_BLOCK
    ov = np.minimum(c0 + CMP_BLOCK, s0 + SEL_BLOCK) - np.maximum(c0, s0)
    m = (np.clip(ov, 0, None) / CMP_BLOCK).astype(np.float32)
    m[:, N_CMP:] = 0.0
    return m


def _cmp_bias():
    n_end = np.arange(N_CMP_PAD)[None, :, None] * CMP_STRIDE + (CMP_BLOCK - 1)
    t = np.arange(SEQ // TQ)[:, None, None] * TQ + np.arange(TQ)[None, None, :]
    return np.where(n_end <= t, 0.0, NEG).astype(np.float32)


def _tri_bias():
    r = np.arange(TK)[:, None]
    c = np.arange(TQ)[None, :]
    return np.where(r <= c, 0.0, NEG).astype(np.float32)


def _window_bias():
    n_back = WINDOW // TK
    r = np.arange(WINDOW + TQ)[:, None]
    c = np.arange(TQ)[None, :]
    tabs = [np.where(r <= i * TQ + c, 0.0, NEG) for i in range(n_back)]
    tabs.append(np.where((r > c) & (r <= c + WINDOW), 0.0, NEG))
    return np.stack(tabs).astype(np.float32)


def kernel(x, c, norm_gain, w_ada, b_ada, w_a_in, conv_w, w_a_out, w_qg, q_gain, w_o,
           kv_norm_gain, w_ada_kv, b_ada_kv, w_kv, k_gain, cmp_pe, cmp_w1, cmp_w2,
           w_mlp1, w_mlp2):
    bsz, s, d = x.shape
    assert (s, d) == (SEQ, D_MODEL)
    m = bsz * s
    x2d = x.reshape(m, d)

    def split_mod(mod, n):
        return [mod[:, k * d:(k + 1) * d].reshape(bsz, 1, d) for k in range(n)]

    sh1, sc1, g1, sh2, sc2, g2 = split_mod(_ada(c, w_ada, b_ada, 0), 6)
    x2d = _short_conv(x2d, norm_gain[0, 0].reshape(1, d), sh1, sc1, w_a_in[0].astype(BF16),
                      conv_w[0], w_a_out[0].astype(BF16), g1)
    x2d = _mlp(x2d, norm_gain[0, 1].reshape(1, d), sh2, sc2, g2,
               w_mlp1[0].astype(BF16), w_mlp2[0].astype(BF16))

    sh1, sc1, g1, sh2, sc2, g2 = split_mod(_ada(c, w_ada, b_ada, 1), 6)
    sh_kv, sc_kv = split_mod(_ada(c, w_ada_kv[None], b_ada_kv[None], 0), 2)
    nq = N_HEADS * HEAD_DIM
    wq_t = w_qg[0][:, :nq].T.astype(BF16)
    per_g = HEADS_PER_GROUP * N_BRANCH
    wg = w_qg[0][:, nq:].reshape(d, N_KV_GROUPS, per_g)
    wg = jnp.pad(wg, ((0, 0), (0, 0), (0, GATE_ROWS - per_g))).reshape(d, N_KV_GROUPS * GATE_ROWS)
    qg_b = jnp.broadcast_to(q_gain[0].reshape(HEAD_DIM, 1), (HEAD_DIM, 128))
    q_t, gates_t, kv6 = _qkv_proj(x2d.reshape(bsz, s, d), norm_gain[1, 0].reshape(1, d), sh1, sc1,
                                  kv_norm_gain.reshape(1, d), sh_kv, sc_kv, wq_t, wg.T.astype(BF16),
                                  w_kv.astype(BF16), qg_b)
    eye = jnp.eye(TQ, dtype=BF16)
    pe8 = jnp.broadcast_to(cmp_pe.reshape(2, 1, CMP_BLOCK * HEAD_DIM), (2, 8, CMP_BLOCK * HEAD_DIM))
    kc, vct, ksa, vst, kwn, vwt = _kv_prep(kv6, pe8.astype(BF16), cmp_w1.astype(BF16),
                                           cmp_w2.astype(BF16), k_gain, eye)
    attn = _nsa(q_t, gates_t, kc, vct, jnp.asarray(_cmp_to_sel_t()), eye,
                jnp.asarray(_cmp_bias()), jnp.asarray(_tri_bias()),
                jnp.asarray(_window_bias()), ksa, vst, kwn, vwt)
    x2d = _mlp(x2d, norm_gain[1, 1].reshape(1, d), sh2, sc2, g2,
               w_mlp1[1].astype(BF16), w_mlp2[1].astype(BF16),
               mixer_out=(attn.reshape(m, d), w_o[0].astype(BF16), g1))
    return x2d.reshape(bsz, s, d)
```

```python
import functools

import numpy as np
import jax
import jax.numpy as jnp
from jax import lax
from jax.experimental import pallas as pl
from jax.experimental.pallas import tpu as pltpu

D_MODEL = 1024
SEQ = 2048
DEPTH = 2
N_A_LAYERS = DEPTH // 2
CONV_WIDTH = 3
D_FF = 4 * D_MODEL
HEAD_DIM = 64
N_HEADS = D_MODEL // HEAD_DIM
N_KV_GROUPS = 4
HEADS_PER_GROUP = N_HEADS // N_KV_GROUPS
N_BRANCH = 3
CMP_BLOCK = 32
CMP_STRIDE = 16
CMP_HIDDEN = 4 * HEAD_DIM
SEL_BLOCK = 64
N_SEL = 16
WINDOW = 512
EPS = 1e-6
NEG = -1e30
BIG = 1e30

N_CMP = (SEQ - CMP_BLOCK) // CMP_STRIDE + 1
N_CMP_PAD = 128
N_SELB = SEQ // SEL_BLOCK
N_TOP = min(N_SEL, N_SELB)

F32 = jnp.float32
BF16 = jnp.bfloat16

VMEM_LIMIT_BYTES = 56 * 1024 * 1024

TM_PROJ = 1024
TN_SCONV = 512
TM_QKV = 512
TM_MLP = 1024
TF_MLP = 1024
TQ = 256
TK = 256
N_Q_TILES = SEQ // TQ
HALO = 8
GATE_ROWS = 16
V_ROWS = HEAD_DIM + 16
K_AUG = 128

_NT = (((1,), (1,)), ((), ()))

Q_SCALE = HEAD_DIM ** -0.5 * float(np.log2(np.e))


def _cparams(sem):
    return pltpu.CompilerParams(dimension_semantics=sem, vmem_limit_bytes=VMEM_LIMIT_BYTES)


def _norm_mod(x, gain, shift, scale):
    ms = jnp.mean(x * x, axis=-1, keepdims=True)
    y = x * lax.rsqrt(ms + EPS) * gain
    return y * (1.0 + scale) + shift


def _ada_kernel(c_ref, w_ref, b_ref, o_ref):
    c = c_ref[...]
    ca = c * jax.nn.sigmoid(c)
    o_ref[...] = jnp.dot(ca.astype(BF16), w_ref[...].astype(BF16),
                         preferred_element_type=F32) + b_ref[...]


def _ada(c, w, b, layer):
    bsz, d = c.shape
    n = w.shape[2]
    tn = 1024
    return pl.pallas_call(
        _ada_kernel,
        grid=(n // tn,),
        in_specs=[pl.BlockSpec((bsz, d), lambda j: (0, 0)),
                  pl.BlockSpec((None, d, tn), lambda j: (layer, 0, j)),
                  pl.BlockSpec((None, 1, tn), lambda j: (layer, 0, j))],
        out_specs=pl.BlockSpec((bsz, tn), lambda j: (0, j)),
        out_shape=jax.ShapeDtypeStruct((bsz, n), F32),
        compiler_params=_cparams(("arbitrary",)),
        name="ada",
    )(c, w, b.reshape(b.shape[0], 1, n))


def _sconv_kernel(x_ref, gain_ref, sh_ref, sc_ref, wb_ref, wc_ref, wu_ref, cw_ref, wo_ref, g_ref,
                  o_ref, vs_ref, *, per_b):
    tm, d = x_ref.shape
    x = x_ref[...]
    h = _norm_mod(x, gain_ref[...], sh_ref[...], sc_ref[...]).astype(BF16)

    @pl.when(pl.program_id(0) % per_b == 0)
    def _():
        vs_ref[0:HALO, :] = jnp.zeros((HALO, d), F32)

    cw = cw_ref[...]
    n_chunk = d // TN_SCONV
    proj = lambda w_ref, c: jnp.dot(h, w_ref[:, c * TN_SCONV:(c + 1) * TN_SCONV], preferred_element_type=F32)
    trip = lambda c: (proj(wb_ref, c), proj(wc_ref, c), proj(wu_ref, c))
    mix = None
    nxt = trip(0)
    for c in range(n_chunk):
        gb, gc, u = nxt
        if c + 1 < n_chunk:
            nxt = trip(c + 1)
        cols = slice(c * TN_SCONV, (c + 1) * TN_SCONV)
        vs_ref[HALO:HALO + tm, cols] = gc * u
        conv = (cw[2:3, cols] * vs_ref[HALO:HALO + tm, cols]
                + cw[1:2, cols] * vs_ref[HALO - 1:HALO - 1 + tm, cols]
                + cw[0:1, cols] * vs_ref[HALO - 2:HALO - 2 + tm, cols])
        part = jnp.dot((gb * conv).astype(BF16), wo_ref[cols, :], preferred_element_type=F32)
        mix = part if mix is None else mix + part
    o_ref[...] = x + g_ref[...] * mix
    vs_ref[0:HALO, :] = vs_ref[tm:tm + HALO, :]


def _short_conv(x2d, gain, sh, sc, w_in, conv_w, w_out, gate):
    m, d = x2d.shape
    tm = TM_PROJ
    per_b = SEQ // tm
    rows = lambda: pl.BlockSpec((tm, d), lambda i: (i, 0))
    vec = lambda: pl.BlockSpec((None, 1, d), lambda i: (i // per_b, 0, 0))
    wcol = lambda k: pl.BlockSpec((d, d), lambda i: (0, k), pipeline_mode=pl.Buffered(1))
    return pl.pallas_call(
        functools.partial(_sconv_kernel, per_b=per_b),
        grid=(m // tm,),
        in_specs=[rows(), pl.BlockSpec((1, d), lambda i: (0, 0)), vec(), vec(),
                  wcol(0), wcol(1), wcol(2),
                  pl.BlockSpec((CONV_WIDTH, d), lambda i: (0, 0)),
                  pl.BlockSpec((d, d), lambda i: (0, 0), pipeline_mode=pl.Buffered(1)),
                  vec()],
        out_specs=rows(),
        out_shape=jax.ShapeDtypeStruct((m, d), F32),
        scratch_shapes=[pltpu.VMEM((tm + HALO, d), F32)],
        compiler_params=_cparams(("arbitrary",)),
        name="short_conv",
    )(x2d, gain, sh, sc, w_in, w_in, w_in, conv_w, w_out, gate)


def _mlp_kernel(*refs, with_mixer_out):
    if with_mixer_out:
        a_ref, wo_ref, g1_ref, x_ref, gain_ref, sh_ref, sc_ref, g_ref, w1_ref, w2_ref, o_ref = refs
        x = x_ref[...] + g1_ref[...] * jnp.dot(a_ref[...], wo_ref[...], preferred_element_type=F32)
    else:
        x_ref, gain_ref, sh_ref, sc_ref, g_ref, w1_ref, w2_ref, o_ref = refs
        x = x_ref[...]
    h = _norm_mod(x, gain_ref[...], sh_ref[...], sc_ref[...]).astype(BF16)
    n_chunk = w1_ref.shape[1] // TF_MLP
    up = lambda k: jnp.dot(h, w1_ref[:, k * TF_MLP:(k + 1) * TF_MLP], preferred_element_type=F32)
    acc = None
    nxt = up(0)
    for k in range(n_chunk):
        cur = nxt
        if k + 1 < n_chunk:
            nxt = up(k + 1)
        h1 = jnp.square(jnp.maximum(cur, 0.0)).astype(BF16)
        part = jnp.dot(h1, w2_ref[k * TF_MLP:(k + 1) * TF_MLP, :], preferred_element_type=F32)
        acc = part if acc is None else acc + part
    o_ref[...] = x + g_ref[...] * acc


def _mlp(x2d, gain, sh, sc, gate, w1, w2, mixer_out=None):
    m, d = x2d.shape
    ff = w1.shape[1]
    tm = TM_MLP
    per_b = SEQ // tm
    rows = lambda: pl.BlockSpec((tm, d), lambda i: (i, 0))
    vec = lambda: pl.BlockSpec((None, 1, d), lambda i: (i // per_b, 0, 0))
    resident = lambda shape: pl.BlockSpec(shape, lambda i: (0, 0), pipeline_mode=pl.Buffered(1))
    args = [x2d, gain, sh, sc, gate, w1, w2]
    in_specs = [rows(), pl.BlockSpec((1, d), lambda i: (0, 0)), vec(), vec(), vec(),
                resident((d, ff)), resident((ff, d))]
    if mixer_out is not None:
        a2d, w_out, gate1 = mixer_out
        args = [a2d, w_out, gate1] + args
        in_specs = [rows(), resident((d, d)), vec()] + in_specs
    return pl.pallas_call(
        functools.partial(_mlp_kernel, with_mixer_out=mixer_out is not None),
        grid=(m // tm,),
        in_specs=in_specs,
        out_specs=rows(),
        out_shape=jax.ShapeDtypeStruct((m, d), F32),
        compiler_params=_cparams(("parallel",)),
        name="mlp",
    )(*args)


def _qkv_kernel(x_ref, gq_ref, shq_ref, scq_ref, gkv_ref, shkv_ref, sckv_ref,
                wq_ref, wg_ref, wkv_ref, qg_ref, q_ref, gt_ref, kv_ref):
    x = x_ref[...]
    tm = x.shape[0]
    ms = jnp.mean(x * x, axis=-1, keepdims=True)
    y = x * lax.rsqrt(ms + EPS)
    hq = ((y * gq_ref[...]) * (1.0 + scq_ref[...]) + shq_ref[...]).astype(BF16)
    hkv = ((y * gkv_ref[...]) * (1.0 + sckv_ref[...]) + shkv_ref[...]).astype(BF16)
    q_t = lax.dot_general(wq_ref[...], hq, _NT, preferred_element_type=F32)
    qg = jnp.concatenate([qg_ref[...]] * (tm // 128), axis=1)
    for h in range(N_HEADS):
        xh = q_t[h * HEAD_DIM:(h + 1) * HEAD_DIM, :]
        msh = jnp.mean(xh * xh, axis=0, keepdims=True)
        q_ref[h * HEAD_DIM:(h + 1) * HEAD_DIM, :] = (xh * lax.rsqrt(msh + EPS) * qg * Q_SCALE).astype(BF16)
    gates_t = jax.nn.sigmoid(lax.dot_general(wg_ref[...], hq, _NT, preferred_element_type=F32))
    for g in range(N_KV_GROUPS):
        gt_ref[g] = gates_t[g * GATE_ROWS:(g + 1) * GATE_ROWS, :]
    kv = jnp.dot(hkv, wkv_ref[...], preferred_element_type=F32)
    for r in range(2 * N_BRANCH):
        for g in range(N_KV_GROUPS):
            c0 = (r * N_KV_GROUPS + g) * HEAD_DIM
            kv_ref[r, g] = kv[:, c0:c0 + HEAD_DIM].astype(BF16)


def _qkv_proj(x3d, gq, shq, scq, gkv, shkv, sckv, wq, wg, wkv, qg_b):
    bsz, s, d = x3d.shape
    tm = TM_QKV
    nkv = wkv.shape[1]
    vec = lambda: pl.BlockSpec((None, 1, d), lambda b, i: (b, 0, 0))
    const = lambda shape: pl.BlockSpec(shape, lambda b, i: (0,) * len(shape))
    return pl.pallas_call(
        _qkv_kernel,
        grid=(bsz, s // tm),
        in_specs=[pl.BlockSpec((None, tm, d), lambda b, i: (b, i, 0)),
                  const((1, d)), vec(), vec(),
                  const((1, d)), vec(), vec(),
                  const((d, d)), const(wg.shape), const((d, nkv)), const(qg_b.shape)],
        out_specs=[pl.BlockSpec((None, d, tm), lambda b, i: (b, 0, i)),
                   pl.BlockSpec((None, N_KV_GROUPS, GATE_ROWS, tm), lambda b, i: (b, 0, 0, i)),
                   pl.BlockSpec((2 * N_BRANCH, None, N_KV_GROUPS, tm, HEAD_DIM),
                                lambda b, i: (0, b, 0, i, 0))],
        out_shape=[jax.ShapeDtypeStruct((bsz, d, s), BF16),
                   jax.ShapeDtypeStruct((bsz, N_KV_GROUPS, GATE_ROWS, s), F32),
                   jax.ShapeDtypeStruct((2 * N_BRANCH, bsz, N_KV_GROUPS, s, HEAD_DIM), BF16)],
        compiler_params=_cparams(("parallel", "parallel")),
        name="qkv_proj",
    )(x3d, gq, shq, scq, gkv, shkv, sckv, wq, wg, wkv, qg_b)


def _head_rms(t, gain):
    ms = jnp.mean(t * t, axis=-1, keepdims=True)
    return t * lax.rsqrt(ms + EPS) * gain


def _kvprep_kernel(kcr_ref, vcr_ref, ks_ref, vs_ref, kw_ref, vw_ref,
                   pe_ref, w1_ref, w2_ref, kg_ref, eye_ref,
                   kc_ref, vct_ref, ksa_ref, vst_ref, kwn_ref, vwt_ref, tok_ref):
    half = CMP_STRIDE * HEAD_DIM
    kg = kg_ref[...]
    eye_hd = eye_ref[0:HEAD_DIM, 0:HEAD_DIM]

    def compress(src_ref, idx):
        tok_ref[...] = src_ref[...].astype(F32)
        r = jnp.concatenate([tok_ref[pl.ds(l, N_CMP_PAD, stride=CMP_STRIDE), :] for l in range(CMP_STRIDE)],
                            axis=1).astype(BF16)
        z_lo = jnp.dot(r, w1_ref[idx, 0:half, :], preferred_element_type=F32)
        z_hi = jnp.dot(r, w1_ref[idx, half:2 * half, :], preferred_element_type=F32)
        z_hi = pltpu.roll(z_hi, N_CMP_PAD - 1, 0)
        pe_b = jnp.dot(pe_ref[idx], w1_ref[idx], preferred_element_type=F32)
        hid = jax.nn.gelu(z_lo + z_hi + pe_b[0:1, :]).astype(BF16)
        return jnp.dot(hid, w2_ref[idx], preferred_element_type=F32)

    kc_ref[...] = _head_rms(compress(kcr_ref, 0), kg[0:1, :]).astype(BF16)
    vc = compress(vcr_ref, 1).astype(BF16)
    vct_ref[...] = lax.dot_general(eye_hd, vc, _NT, preferred_element_type=F32).astype(BF16)

    ksn = _head_rms(ks_ref[...].astype(F32), kg[1:2, :]).astype(BF16)
    placed = jnp.dot(ksn, eye_ref[0:HEAD_DIM, 0:K_AUG], preferred_element_type=F32)
    row = lax.broadcasted_iota(jnp.int32, (SEQ, K_AUG), 0)
    col = lax.broadcasted_iota(jnp.int32, (SEQ, K_AUG), 1)
    onehot = (col - HEAD_DIM) == lax.shift_right_logical(row, 6)
    ksa_ref[...] = jnp.where(onehot, 1.0, placed).astype(BF16)
    kwn_ref[...] = _head_rms(kw_ref[...].astype(F32), kg[2:3, :]).astype(BF16)

    ones = jnp.ones((V_ROWS - HEAD_DIM, TK), BF16)
    for src, dst in ((vs_ref, vst_ref), (vw_ref, vwt_ref)):
        for c in range(SEQ // TK):
            blk = src[c * TK:(c + 1) * TK, :]
            dst[c, 0:HEAD_DIM, :] = lax.dot_general(eye_hd, blk, _NT,
                                                    preferred_element_type=F32).astype(BF16)
            dst[c, HEAD_DIM:V_ROWS, :] = ones


def _kv_prep(kv6, pe8, w1, w2, k_gain, eye):
    _, bsz, ng, s, hd = kv6.shape
    nt = s // TK
    sel = lambda r: pl.BlockSpec((None, None, None, s, hd), lambda b, g: (r, b, g, 0, 0))
    const = lambda shape: pl.BlockSpec(shape, lambda b, g: (0,) * len(shape))
    out2 = lambda n, w: pl.BlockSpec((None, None, n, w), lambda b, g: (b, g, 0, 0))
    out3 = pl.BlockSpec((None, None, nt, V_ROWS, TK), lambda b, g: (b, g, 0, 0, 0))
    vt_shape = jax.ShapeDtypeStruct((bsz, ng, nt, V_ROWS, TK), BF16)
    return pl.pallas_call(
        _kvprep_kernel,
        grid=(bsz, ng),
        in_specs=[sel(0), sel(1), sel(2), sel(3), sel(4), sel(5),
                  const(pe8.shape), const(w1.shape), const(w2.shape), const(k_gain.shape),
                  const(eye.shape)],
        out_specs=[out2(N_CMP_PAD, hd), out2(hd, N_CMP_PAD), out2(s, K_AUG), out3, out2(s, hd), out3],
        out_shape=[jax.ShapeDtypeStruct((bsz, ng, N_CMP_PAD, hd), BF16),
                   jax.ShapeDtypeStruct((bsz, ng, hd, N_CMP_PAD), BF16),
                   jax.ShapeDtypeStruct((bsz, ng, s, K_AUG), BF16),
                   vt_shape,
                   jax.ShapeDtypeStruct((bsz, ng, s, hd), BF16),
                   vt_shape],
        scratch_shapes=[pltpu.VMEM((s, hd), F32)],
        compiler_params=_cparams(("parallel", "parallel")),
        name="kv_prep",
    )(kv6, kv6, kv6, kv6, kv6, kv6, pe8, w1, w2, k_gain, eye)


def _nsa_kernel(*refs):
    for i in range(N_Q_TILES):
        _nsa_tile(i, *refs)


def _nsa_tile(i, q_ref, gt_ref, kc_ref, vct_ref, cmapt_ref, eye_ref, cbias_ref, tri_ref, wtri_ref,
              ksa_ref, vst_ref, kwn_ref, vwt_ref, o_ref,
              qa_ref, m_ref, a_ref, acc_ref, s_buf, p_buf, sw_ref):
    part = slice(i * TQ, (i + 1) * TQ)
    parity = i % 2
    t0 = i * TQ
    hg = HEADS_PER_GROUP
    lanes = hg * TQ
    eye = eye_ref[...]
    head = lambda k: slice(k * TQ, (k + 1) * TQ)

    for k in range(hg):
        qa_ref[0:HEAD_DIM, head(k)] = q_ref[k * HEAD_DIM:(k + 1) * HEAD_DIM, part]
    qn = qa_ref[0:HEAD_DIM, :]

    lane_t = t0 + (lax.broadcasted_iota(jnp.int32, (1, lanes), 1) & (TQ - 1))

    def k_rows(ref, c):
        return ref[c * TK:(c + 1) * TK, :]

    n_back = WINDOW // TK
    c_w = max(i - n_back, 0)
    n_win = i - c_w + 1
    sc = jnp.dot(kc_ref[...], qn, preferred_element_type=F32)
    w_bias = wtri_ref.at[min(i, n_back)]
    m_w = None
    for j in range(n_win):
        s_j = (jnp.dot(k_rows(kwn_ref, c_w + j), qn, preferred_element_type=F32)
               + jnp.concatenate([w_bias[j * TK:(j + 1) * TK, :]] * hg, axis=1))
        sw_ref[j] = s_j
        m_j = jnp.max(s_j, axis=0, keepdims=True)
        m_w = m_j if m_w is None else jnp.maximum(m_w, m_j)

    sc = sc + jnp.concatenate([cbias_ref[i]] * hg, axis=1)
    ec = jnp.exp2(sc - jnp.max(sc, axis=0, keepdims=True))
    any_valid = jnp.where(lane_t >= CMP_BLOCK - 1, 1.0, 0.0)
    pc = ec * (any_valid / jnp.maximum(jnp.sum(ec, axis=0, keepdims=True), 1e-30))
    o_cmp = jnp.dot(vct_ref[...], pc.astype(BF16), preferred_element_type=F32)

    psum = pc[:, head(0)]
    for k in range(1, hg):
        psum = psum + pc[:, head(k)]
    imp = jnp.dot(cmapt_ref[...], psum, preferred_element_type=F32,
                  precision=lax.Precision.HIGHEST)

    acc_w = jnp.zeros((V_ROWS, lanes), F32)
    for j in range(n_win):
        acc_w = acc_w + jnp.dot(vwt_ref[c_w + j], jnp.exp2(sw_ref[j] - m_w).astype(BF16),
                                preferred_element_type=F32)
    o_win = acc_w[0:HEAD_DIM, :] / jnp.maximum(acc_w[HEAD_DIM:HEAD_DIM + 1, :], 1e-30)

    tq = t0 + lax.broadcasted_iota(jnp.int32, (N_SELB, TQ), 1)
    j_id = lax.broadcasted_iota(jnp.int32, (N_SELB, TQ), 0)
    cur = lax.shift_right_logical(tq, 6)
    forced = (j_id == 0) | (j_id == cur) | (j_id == cur - 1)
    causal = j_id <= cur
    score = jnp.where(forced, BIG, jnp.where(causal, imp, NEG))
    rank = jnp.zeros((N_SELB, TQ), jnp.int32)
    for jp in range(N_SELB):
        other = score[jp:jp + 1, :]
        beats = (other > score) | ((other == score) & (j_id > jp))
        rank = rank + beats.astype(jnp.int32)
    chosen = (rank < N_TOP) & causal
    sel_bias = jnp.where(chosen, 0.0, NEG).astype(BF16)
    for k in range(hg):
        qa_ref[HEAD_DIM:HEAD_DIM + N_SELB, head(k)] = sel_bias
    qa_ref[HEAD_DIM + N_SELB:K_AUG, :] = jnp.zeros((K_AUG - HEAD_DIM - N_SELB, lanes), BF16)

    def sel_scores(c):
        return jnp.dot(k_rows(ksa_ref, c), qa_ref[...], preferred_element_type=F32)

    def softmax_update(scores):
        m_old = m_ref[...]
        m_new = jnp.maximum(m_old, jnp.max(scores(), axis=0, keepdims=True))
        m_ref[...] = m_new
        return jnp.exp2(m_old - m_new), jnp.exp2(scores() - m_new).astype(BF16)

    m_ref[...] = jnp.full(m_ref.shape, NEG, F32)
    acc_ref[...] = jnp.zeros(acc_ref.shape, F32)
    s_buf[0] = sel_scores(0)

    def pv_prev(c):
        if c == 0:
            return acc_ref[...]
        pv = jnp.dot(vst_ref[c - 1], p_buf[(c - 1) % 2], preferred_element_type=F32)
        return a_ref[...] * acc_ref[...] + pv

    for c in range(i):
        cur = c % 2
        if c >= 1:
            acc_ref[...] = pv_prev(c)
        s_buf[1 - cur] = sel_scores(c + 1)
        alpha, p = softmax_update(lambda: s_buf[cur])
        p_buf[cur] = p
        a_ref[...] = alpha

    acc_prev = pv_prev(i)
    alpha, p = softmax_update(lambda: s_buf[parity] + jnp.concatenate([tri_ref[...]] * hg, axis=1))
    acc_sel = alpha * acc_prev + jnp.dot(vst_ref[i], p, preferred_element_type=F32)
    o_sel = acc_sel[0:HEAD_DIM, :] / jnp.maximum(acc_sel[HEAD_DIM:HEAD_DIM + 1, :], 1e-30)

    gt = gt_ref[:, part]
    parts = []
    for k in range(hg):
        parts.append(gt[3 * k:3 * k + 1, :] * o_cmp[:, head(k)]
                     + gt[3 * k + 1:3 * k + 2, :] * o_sel[:, head(k)]
                     + gt[3 * k + 2:3 * k + 3, :] * o_win[:, head(k)])
    o_t = jnp.concatenate(parts, axis=0).astype(BF16)
    o_ref[part, :] = lax.dot_general(eye, o_t, _NT, preferred_element_type=F32).astype(BF16)


def _nsa(q_t, gates_t, kc, vct, cmapt, eye, cbias, tri, wtri, ksa, vst, kwn, vwt):
    bsz, d, s = q_t.shape
    ng = N_KV_GROUPS
    gw = HEADS_PER_GROUP * HEAD_DIM
    lanes = HEADS_PER_GROUP * TQ
    nt = s // TK
    assert TQ == TK and WINDOW % TK == 0 and gw == TQ and s == SEQ
    per_bg = lambda n, w: pl.BlockSpec((None, None, n, w), lambda b, g: (b, g, 0, 0))
    vt = lambda: pl.BlockSpec((None, None, nt, V_ROWS, TK), lambda b, g: (b, g, 0, 0, 0))
    const = lambda shape: pl.BlockSpec(shape, lambda b, g: (0,) * len(shape))
    return pl.pallas_call(
        _nsa_kernel,
        grid=(bsz, ng),
        in_specs=[pl.BlockSpec((None, gw, s), lambda b, g: (b, g, 0)),
                  per_bg(GATE_ROWS, s),
                  per_bg(N_CMP_PAD, HEAD_DIM), per_bg(HEAD_DIM, N_CMP_PAD),
                  const(cmapt.shape), const(eye.shape), const(cbias.shape), const(tri.shape),
                  const(wtri.shape),
                  per_bg(s, K_AUG), vt(), per_bg(s, HEAD_DIM), vt()],
        out_specs=pl.BlockSpec((None, s, gw), lambda b, g: (b, 0, g)),
        out_shape=jax.ShapeDtypeStruct((bsz, s, d), BF16),
        scratch_shapes=[pltpu.VMEM((K_AUG, lanes), BF16),
                        pltpu.VMEM((1, lanes), F32),
                        pltpu.VMEM((1, lanes), F32),
                        pltpu.VMEM((V_ROWS, lanes), F32),
                        pltpu.VMEM((2, TK, lanes), F32),
                        pltpu.VMEM((2, TK, lanes), BF16),
                        pltpu.VMEM((WINDOW // TK + 1, TK, lanes), F32)],
        compiler_params=_cparams(("parallel", "parallel")),
        name="nsa",
    )(q_t, gates_t, kc, vct, cmapt, eye, cbias, tri, wtri, ksa, vst, kwn, vwt)


def _cmp_to_sel_t():
    c0 = np.arange(N_CMP_PAD)[None, :] * CMP_STRIDE
    s0 = np.arange(N_SELB)[:, None] * SEL_BLOCK
    ov = np.minimum(c0 + CMP_BLOCK, s0 + SEL_BLOCK) - np.maximum(c0, s0)
    m = (np.clip(ov, 0, None) / CMP_BLOCK).astype(np.float32)
    m[:, N_CMP:] = 0.0
    return m


def _cmp_bias():
    n_end = np.arange(N_CMP_PAD)[None, :, None] * CMP_STRIDE + (CMP_BLOCK - 1)
    t = np.arange(SEQ // TQ)[:, None, None] * TQ + np.arange(TQ)[None, None, :]
    return np.where(n_end <= t, 0.0, NEG).astype(np.float32)


def _tri_bias():
    r = np.arange(TK)[:, None]
    c = np.arange(TQ)[None, :]
    return np.where(r <= c, 0.0, NEG).astype(np.float32)


def _window_bias():
    n_back = WINDOW // TK
    r = np.arange(WINDOW + TQ)[:, None]
    c = np.arange(TQ)[None, :]
    tabs = [np.where(r <= i * TQ + c, 0.0, NEG) for i in range(n_back)]
    tabs.append(np.where((r > c) & (r <= c + WINDOW), 0.0, NEG))
    return np.stack(tabs).astype(np.float32)


def kernel(x, c, norm_gain, w_ada, b_ada, w_a_in, conv_w, w_a_out, w_qg, q_gain, w_o,
           kv_norm_gain, w_ada_kv, b_ada_kv, w_kv, k_gain, cmp_pe, cmp_w1, cmp_w2,
           w_mlp1, w_mlp2):
    bsz, s, d = x.shape
    assert (s, d) == (SEQ, D_MODEL)
    m = bsz * s
    x2d = x.reshape(m, d)

    def split_mod(mod, n):
        return [mod[:, k * d:(k + 1) * d].reshape(bsz, 1, d) for k in range(n)]

    sh1, sc1, g1, sh2, sc2, g2 = split_mod(_ada(c, w_ada, b_ada, 0), 6)
    x2d = _short_conv(x2d, norm_gain[0, 0].reshape(1, d), sh1, sc1, w_a_in[0].astype(BF16),
                      conv_w[0], w_a_out[0].astype(BF16), g1)
    x2d = _mlp(x2d, norm_gain[0, 1].reshape(1, d), sh2, sc2, g2,
               w_mlp1[0].astype(BF16), w_mlp2[0].astype(BF16))

    sh1, sc1, g1, sh2, sc2, g2 = split_mod(_ada(c, w_ada, b_ada, 1), 6)
    sh_kv, sc_kv = split_mod(_ada(c, w_ada_kv[None], b_ada_kv[None], 0), 2)
    nq = N_HEADS * HEAD_DIM
    wq_t = w_qg[0][:, :nq].T.astype(BF16)
    per_g = HEADS_PER_GROUP * N_BRANCH
    wg = w_qg[0][:, nq:].reshape(d, N_KV_GROUPS, per_g)
    wg = jnp.pad(wg, ((0, 0), (0, 0), (0, GATE_ROWS - per_g))).reshape(d, N_KV_GROUPS * GATE_ROWS)
    qg_b = jnp.broadcast_to(q_gain[0].reshape(HEAD_DIM, 1), (HEAD_DIM, 128))
    q_t, gates_t, kv6 = _qkv_proj(x2d.reshape(bsz, s, d), norm_gain[1, 0].reshape(1, d), sh1, sc1,
                                  kv_norm_gain.reshape(1, d), sh_kv, sc_kv, wq_t, wg.T.astype(BF16),
                                  w_kv.astype(BF16), qg_b)
    eye = jnp.eye(TQ, dtype=BF16)
    pe8 = jnp.broadcast_to(cmp_pe.reshape(2, 1, CMP_BLOCK * HEAD_DIM), (2, 8, CMP_BLOCK * HEAD_DIM))
    kc, vct, ksa, vst, kwn, vwt = _kv_prep(kv6, pe8.astype(BF16), cmp_w1.astype(BF16),
                                           cmp_w2.astype(BF16), k_gain, eye)
    attn = _nsa(q_t, gates_t, kc, vct, jnp.asarray(_cmp_to_sel_t()), eye,
                jnp.asarray(_cmp_bias()), jnp.asarray(_tri_bias()),
                jnp.asarray(_window_bias()), ksa, vst, kwn, vwt)
    x2d = _mlp(x2d, norm_gain[1, 1].reshape(1, d), sh2, sc2, g2,
               w_mlp1[1].astype(BF16), w_mlp2[1].astype(BF16),
               mixer_out=(attn.reshape(m, d), w_o[0].astype(BF16), g1))
    return x2d.reshape(bsz, s, d)
```

```python
import functools

import numpy as np
import jax
import jax.numpy as jnp
from jax import lax
from jax.experimental import pallas as pl
from jax.experimental.pallas import tpu as pltpu

D_MODEL = 1024
SEQ = 2048
DEPTH = 2
N_A_LAYERS = DEPTH // 2
CONV_WIDTH = 3
D_FF = 4 * D_MODEL
HEAD_DIM = 64
N_HEADS = D_MODEL // HEAD_DIM
N_KV_GROUPS = 4
HEADS_PER_GROUP = N_HEADS // N_KV_GROUPS
N_BRANCH = 3
CMP_BLOCK = 32
CMP_STRIDE = 16
CMP_HIDDEN = 4 * HEAD_DIM
SEL_BLOCK = 64
N_SEL = 16
WINDOW = 512
EPS = 1e-6
NEG = -1e30
BIG = 1e30

N_CMP = (SEQ - CMP_BLOCK) // CMP_STRIDE + 1
N_CMP_PAD = 128
N_SELB = SEQ // SEL_BLOCK
N_TOP = min(N_SEL, N_SELB)

F32 = jnp.float32
BF16 = jnp.bfloat16

VMEM_LIMIT_BYTES = 56 * 1024 * 1024

TM_PROJ = 1024
TN_SCONV = 512
TM_QKV = 1024
TM_MLP = 1024
TF_MLP = 1024
TQ = 256
TK = 256
N_Q_TILES = SEQ // TQ
HALO = 8
GATE_ROWS = 16
V_ROWS = HEAD_DIM + 16
K_AUG = 128

_NT = (((1,), (1,)), ((), ()))

Q_SCALE = HEAD_DIM ** -0.5 * float(np.log2(np.e))


def _cparams(sem):
    return pltpu.CompilerParams(dimension_semantics=sem, vmem_limit_bytes=VMEM_LIMIT_BYTES)


def _norm_mod(x, gain, shift, scale):
    ms = jnp.mean(x * x, axis=-1, keepdims=True)
    y = x * lax.rsqrt(ms + EPS) * gain
    return y * (1.0 + scale) + shift


def _ada_kernel(c_ref, w_ref, b_ref, o_ref):
    c = c_ref[...]
    ca = c * jax.nn.sigmoid(c)
    o_ref[...] = jnp.dot(ca.astype(BF16), w_ref[...].astype(BF16),
                         preferred_element_type=F32) + b_ref[...]


def _ada(c, w, b, layer):
    bsz, d = c.shape
    n = w.shape[2]
    tn = 1024
    return pl.pallas_call(
        _ada_kernel,
        grid=(n // tn,),
        in_specs=[pl.BlockSpec((bsz, d), lambda j: (0, 0)),
                  pl.BlockSpec((None, d, tn), lambda j: (layer, 0, j)),
                  pl.BlockSpec((None, 1, tn), lambda j: (layer, 0, j))],
        out_specs=pl.BlockSpec((bsz, tn), lambda j: (0, j)),
        out_shape=jax.ShapeDtypeStruct((bsz, n), F32),
        compiler_params=_cparams(("arbitrary",)),
        name="ada",
    )(c, w, b.reshape(b.shape[0], 1, n))


def _sconv_kernel(x_ref, gain_ref, sh_ref, sc_ref, wb_ref, wc_ref, wu_ref, cw_ref, wo_ref, g_ref,
                  o_ref, vs_ref, *, per_b):
    tm, d = x_ref.shape
    x = x_ref[...]
    h = _norm_mod(x, gain_ref[...], sh_ref[...], sc_ref[...]).astype(BF16)

    @pl.when(pl.program_id(0) % per_b == 0)
    def _():
        vs_ref[0:HALO, :] = jnp.zeros((HALO, d), F32)

    cw = cw_ref[...]
    n_chunk = d // TN_SCONV
    proj = lambda w_ref, c: jnp.dot(h, w_ref[:, c * TN_SCONV:(c + 1) * TN_SCONV], preferred_element_type=F32)
    trip = lambda c: (proj(wb_ref, c), proj(wc_ref, c), proj(wu_ref, c))
    mix = None
    nxt = trip(0)
    for c in range(n_chunk):
        gb, gc, u = nxt
        if c + 1 < n_chunk:
            nxt = trip(c + 1)
        cols = slice(c * TN_SCONV, (c + 1) * TN_SCONV)
        vs_ref[HALO:HALO + tm, cols] = gc * u
        conv = (cw[2:3, cols] * vs_ref[HALO:HALO + tm, cols]
                + cw[1:2, cols] * vs_ref[HALO - 1:HALO - 1 + tm, cols]
                + cw[0:1, cols] * vs_ref[HALO - 2:HALO - 2 + tm, cols])
        part = jnp.dot((gb * conv).astype(BF16), wo_ref[cols, :], preferred_element_type=F32)
        mix = part if mix is None else mix + part
    o_ref[...] = x + g_ref[...] * mix
    vs_ref[0:HALO, :] = vs_ref[tm:tm + HALO, :]


def _short_conv(x2d, gain, sh, sc, w_in, conv_w, w_out, gate):
    m, d = x2d.shape
    tm = TM_PROJ
    per_b = SEQ // tm
    rows = lambda: pl.BlockSpec((tm, d), lambda i: (i, 0))
    vec = lambda: pl.BlockSpec((None, 1, d), lambda i: (i // per_b, 0, 0))
    wcol = lambda k: pl.BlockSpec((d, d), lambda i: (0, k), pipeline_mode=pl.Buffered(1))
    return pl.pallas_call(
        functools.partial(_sconv_kernel, per_b=per_b),
        grid=(m // tm,),
        in_specs=[rows(), pl.BlockSpec((1, d), lambda i: (0, 0)), vec(), vec(),
                  wcol(0), wcol(1), wcol(2),
                  pl.BlockSpec((CONV_WIDTH, d), lambda i: (0, 0)),
                  pl.BlockSpec((d, d), lambda i: (0, 0), pipeline_mode=pl.Buffered(1)),
                  vec()],
        out_specs=rows(),
        out_shape=jax.ShapeDtypeStruct((m, d), F32),
        scratch_shapes=[pltpu.VMEM((tm + HALO, d), F32)],
        compiler_params=_cparams(("arbitrary",)),
        name="short_conv",
    )(x2d, gain, sh, sc, w_in, w_in, w_in, conv_w, w_out, gate)


def _mlp_kernel(*refs, with_mixer_out):
    if with_mixer_out:
        a_ref, wo_ref, g1_ref, x_ref, gain_ref, sh_ref, sc_ref, g_ref, w1_ref, w2_ref, o_ref = refs
        x = x_ref[...] + g1_ref[...] * jnp.dot(a_ref[...], wo_ref[...], preferred_element_type=F32)
    else:
        x_ref, gain_ref, sh_ref, sc_ref, g_ref, w1_ref, w2_ref, o_ref = refs
        x = x_ref[...]
    h = _norm_mod(x, gain_ref[...], sh_ref[...], sc_ref[...]).astype(BF16)
    n_chunk = w1_ref.shape[1] // TF_MLP
    up = lambda k: jnp.dot(h, w1_ref[:, k * TF_MLP:(k + 1) * TF_MLP], preferred_element_type=F32)
    acc = None
    nxt = up(0)
    for k in range(n_chunk):
        cur = nxt
        if k + 1 < n_chunk:
            nxt = up(k + 1)
        h1 = jnp.square(jnp.maximum(cur, 0.0)).astype(BF16)
        part = jnp.dot(h1, w2_ref[k * TF_MLP:(k + 1) * TF_MLP, :], preferred_element_type=F32)
        acc = part if acc is None else acc + part
    o_ref[...] = x + g_ref[...] * acc


def _mlp(x2d, gain, sh, sc, gate, w1, w2, mixer_out=None):
    m, d = x2d.shape
    ff = w1.shape[1]
    tm = TM_MLP
    per_b = SEQ // tm
    rows = lambda: pl.BlockSpec((tm, d), lambda i: (i, 0))
    vec = lambda: pl.BlockSpec((None, 1, d), lambda i: (i // per_b, 0, 0))
    resident = lambda shape: pl.BlockSpec(shape, lambda i: (0, 0), pipeline_mode=pl.Buffered(1))
    args = [x2d, gain, sh, sc, gate, w1, w2]
    in_specs = [rows(), pl.BlockSpec((1, d), lambda i: (0, 0)), vec(), vec(), vec(),
                resident((d, ff)), resident((ff, d))]
    if mixer_out is not None:
        a2d, w_out, gate1 = mixer_out
        args = [a2d, w_out, gate1] + args
        in_specs = [rows(), resident((d, d)), vec()] + in_specs
    return pl.pallas_call(
        functools.partial(_mlp_kernel, with_mixer_out=mixer_out is not None),
        grid=(m // tm,),
        in_specs=in_specs,
        out_specs=rows(),
        out_shape=jax.ShapeDtypeStruct((m, d), F32),
        compiler_params=_cparams(("parallel",)),
        name="mlp",
    )(*args)


def _qkv_kernel(x_ref, gq_ref, shq_ref, scq_ref, gkv_ref, shkv_ref, sckv_ref,
                wq_ref, wg_ref, wkv_ref, qg_ref, q_ref, gt_ref, kv_ref):
    x = x_ref[...]
    tm = x.shape[0]
    ms = jnp.mean(x * x, axis=-1, keepdims=True)
    y = x * lax.rsqrt(ms + EPS)
    hq = ((y * gq_ref[...]) * (1.0 + scq_ref[...]) + shq_ref[...]).astype(BF16)
    hkv = ((y * gkv_ref[...]) * (1.0 + sckv_ref[...]) + shkv_ref[...]).astype(BF16)
    q_t = lax.dot_general(wq_ref[...], hq, _NT, preferred_element_type=F32)
    qg = jnp.concatenate([qg_ref[...]] * (tm // 128), axis=1)
    for h in range(N_HEADS):
        xh = q_t[h * HEAD_DIM:(h + 1) * HEAD_DIM, :]
        msh = jnp.mean(xh * xh, axis=0, keepdims=True)
        q_ref[h * HEAD_DIM:(h + 1) * HEAD_DIM, :] = (xh * lax.rsqrt(msh + EPS) * qg * Q_SCALE).astype(BF16)
    gates_t = jax.nn.sigmoid(lax.dot_general(wg_ref[...], hq, _NT, preferred_element_type=F32))
    for g in range(N_KV_GROUPS):
        gt_ref[g] = gates_t[g * GATE_ROWS:(g + 1) * GATE_ROWS, :]
    kv = jnp.dot(hkv, wkv_ref[...], preferred_element_type=F32)
    for r in range(2 * N_BRANCH):
        for g in range(N_KV_GROUPS):
            c0 = (r * N_KV_GROUPS + g) * HEAD_DIM
            kv_ref[r, g] = kv[:, c0:c0 + HEAD_DIM].astype(BF16)


def _qkv_proj(x3d, gq, shq, scq, gkv, shkv, sckv, wq, wg, wkv, qg_b):
    bsz, s, d = x3d.shape
    tm = TM_QKV
    nkv = wkv.shape[1]
    vec = lambda: pl.BlockSpec((None, 1, d), lambda b, i: (b, 0, 0))
    const = lambda shape: pl.BlockSpec(shape, lambda b, i: (0,) * len(shape), pipeline_mode=pl.Buffered(1))
    return pl.pallas_call(
        _qkv_kernel,
        grid=(bsz, s // tm),
        in_specs=[pl.BlockSpec((None, tm, d), lambda b, i: (b, i, 0)),
                  const((1, d)), vec(), vec(),
                  const((1, d)), vec(), vec(),
                  const((d, d)), const(wg.shape), const((d, nkv)), const(qg_b.shape)],
        out_specs=[pl.BlockSpec((None, d, tm), lambda b, i: (b, 0, i)),
                   pl.BlockSpec((None, N_KV_GROUPS, GATE_ROWS, tm), lambda b, i: (b, 0, 0, i)),
                   pl.BlockSpec((2 * N_BRANCH, None, N_KV_GROUPS, tm, HEAD_DIM),
                                lambda b, i: (0, b, 0, i, 0))],
        out_shape=[jax.ShapeDtypeStruct((bsz, d, s), BF16),
                   jax.ShapeDtypeStruct((bsz, N_KV_GROUPS, GATE_ROWS, s), F32),
                   jax.ShapeDtypeStruct((2 * N_BRANCH, bsz, N_KV_GROUPS, s, HEAD_DIM), BF16)],
        compiler_params=_cparams(("parallel", "parallel")),
        name="qkv_proj",
    )(x3d, gq, shq, scq, gkv, shkv, sckv, wq, wg, wkv, qg_b)


def _head_rms(t, gain):
    ms = jnp.mean(t * t, axis=-1, keepdims=True)
    return t * lax.rsqrt(ms + EPS) * gain


def _kvprep_kernel(kcr_ref, vcr_ref, ks_ref, vs_ref, kw_ref, vw_ref,
                   pe_ref, w1_ref, w2_ref, kg_ref, eye_ref,
                   kc_ref, vct_ref, ksa_ref, vst_ref, kwn_ref, vwt_ref, tok_ref):
    half = CMP_STRIDE * HEAD_DIM
    kg = kg_ref[...]
    eye_hd = eye_ref[0:HEAD_DIM, 0:HEAD_DIM]

    def compress(src_ref, idx):
        tok_ref[...] = src_ref[...].astype(F32)
        r = jnp.concatenate([tok_ref[pl.ds(l, N_CMP_PAD, stride=CMP_STRIDE), :] for l in range(CMP_STRIDE)],
                            axis=1).astype(BF16)
        z_lo = jnp.dot(r, w1_ref[idx, 0:half, :], preferred_element_type=F32)
        z_hi = jnp.dot(r, w1_ref[idx, half:2 * half, :], preferred_element_type=F32)
        z_hi = pltpu.roll(z_hi, N_CMP_PAD - 1, 0)
        pe_b = jnp.dot(pe_ref[idx], w1_ref[idx], preferred_element_type=F32)
        hid = jax.nn.gelu(z_lo + z_hi + pe_b[0:1, :]).astype(BF16)
        return jnp.dot(hid, w2_ref[idx], preferred_element_type=F32)

    kc_ref[...] = _head_rms(compress(kcr_ref, 0), kg[0:1, :]).astype(BF16)
    vc = compress(vcr_ref, 1).astype(BF16)
    vct_ref[...] = lax.dot_general(eye_hd, vc, _NT, preferred_element_type=F32).astype(BF16)

    ksn = _head_rms(ks_ref[...].astype(F32), kg[1:2, :]).astype(BF16)
    placed = jnp.dot(ksn, eye_ref[0:HEAD_DIM, 0:K_AUG], preferred_element_type=F32)
    row = lax.broadcasted_iota(jnp.int32, (SEQ, K_AUG), 0)
    col = lax.broadcasted_iota(jnp.int32, (SEQ, K_AUG), 1)
    onehot = (col - HEAD_DIM) == lax.shift_right_logical(row, 6)
    ksa_ref[...] = jnp.where(onehot, 1.0, placed).astype(BF16)
    kwn_ref[...] = _head_rms(kw_ref[...].astype(F32), kg[2:3, :]).astype(BF16)

    ones = jnp.ones((V_ROWS - HEAD_DIM, TK), BF16)
    for src, dst in ((vs_ref, vst_ref), (vw_ref, vwt_ref)):
        for c in range(SEQ // TK):
            blk = src[c * TK:(c + 1) * TK, :]
            dst[c, 0:HEAD_DIM, :] = lax.dot_general(eye_hd, blk, _NT,
                                                    preferred_element_type=F32).astype(BF16)
            dst[c, HEAD_DIM:V_ROWS, :] = ones


def _kv_prep(kv6, pe8, w1, w2, k_gain, eye):
    _, bsz, ng, s, hd = kv6.shape
    nt = s // TK
    sel = lambda r: pl.BlockSpec((None, None, None, s, hd), lambda b, g: (r, b, g, 0, 0))
    const = lambda shape: pl.BlockSpec(shape, lambda b, g: (0,) * len(shape))
    out2 = lambda n, w: pl.BlockSpec((None, None, n, w), lambda b, g: (b, g, 0, 0))
    out3 = pl.BlockSpec((None, None, nt, V_ROWS, TK), lambda b, g: (b, g, 0, 0, 0))
    vt_shape = jax.ShapeDtypeStruct((bsz, ng, nt, V_ROWS, TK), BF16)
    return pl.pallas_call(
        _kvprep_kernel,
        grid=(bsz, ng),
        in_specs=[sel(0), sel(1), sel(2), sel(3), sel(4), sel(5),
                  const(pe8.shape), const(w1.shape), const(w2.shape), const(k_gain.shape),
                  const(eye.shape)],
        out_specs=[out2(N_CMP_PAD, hd), out2(hd, N_CMP_PAD), out2(s, K_AUG), out3, out2(s, hd), out3],
        out_shape=[jax.ShapeDtypeStruct((bsz, ng, N_CMP_PAD, hd), BF16),
                   jax.ShapeDtypeStruct((bsz, ng, hd, N_CMP_PAD), BF16),
                   jax.ShapeDtypeStruct((bsz, ng, s, K_AUG), BF16),
                   vt_shape,
                   jax.ShapeDtypeStruct((bsz, ng, s, hd), BF16),
                   vt_shape],
        scratch_shapes=[pltpu.VMEM((s, hd), F32)],
        compiler_params=_cparams(("parallel", "parallel")),
        name="kv_prep",
    )(kv6, kv6, kv6, kv6, kv6, kv6, pe8, w1, w2, k_gain, eye)


def _nsa_kernel(*refs):
    for _ in _nsa_front(0, *refs):
        pass
    for i in range(N_Q_TILES):
        streams = [_nsa_sweep(i, *refs)]
        if i + 1 < N_Q_TILES:
            streams.append(_nsa_front(i + 1, *refs))
        while streams:
            for st in list(streams):
                try:
                    next(st)
                except StopIteration:
                    streams.remove(st)


def _nsa_front(i, q_ref, gt_ref, kc_ref, vct_ref, cmapt_ref, eye_ref, cbias_ref, tri_ref, wtri_ref,
               ksa_ref, vst_ref, kwn_ref, vwt_ref, o_ref,
               qa_ref, m_ref, a_ref, acc_ref, s_buf, p_buf, sw_ref, ow_ref):
    part = slice(i * TQ, (i + 1) * TQ)
    t0 = i * TQ
    hg = HEADS_PER_GROUP
    lanes = hg * TQ
    head = lambda k: slice(k * TQ, (k + 1) * TQ)
    qa = qa_ref.at[i % 2]
    ow = ow_ref.at[i % 2]

    for k in range(hg):
        qa[0:HEAD_DIM, head(k)] = q_ref[k * HEAD_DIM:(k + 1) * HEAD_DIM, part]
    qn = qa[0:HEAD_DIM, :]

    lane_t = t0 + (lax.broadcasted_iota(jnp.int32, (1, lanes), 1) & (TQ - 1))

    def k_rows(ref, c):
        return ref[c * TK:(c + 1) * TK, :]

    n_back = WINDOW // TK
    c_w = max(i - n_back, 0)
    n_win = i - c_w + 1
    sc = jnp.dot(kc_ref[...], qn, preferred_element_type=F32)
    w_bias = wtri_ref.at[min(i, n_back)]
    m_w = None
    for j in range(n_win):
        s_j = (jnp.dot(k_rows(kwn_ref, c_w + j), qn, preferred_element_type=F32)
               + jnp.concatenate([w_bias[j * TK:(j + 1) * TK, :]] * hg, axis=1))
        sw_ref[j] = s_j
        m_j = jnp.max(s_j, axis=0, keepdims=True)
        m_w = m_j if m_w is None else jnp.maximum(m_w, m_j)
        yield

    sc = sc + jnp.concatenate([cbias_ref[i]] * hg, axis=1)
    ec = jnp.exp2(sc - jnp.max(sc, axis=0, keepdims=True))
    any_valid = jnp.where(lane_t >= CMP_BLOCK - 1, 1.0, 0.0)
    pc = ec * (any_valid / jnp.maximum(jnp.sum(ec, axis=0, keepdims=True), 1e-30))
    ow[0:HEAD_DIM, :] = jnp.dot(vct_ref[...], pc.astype(BF16), preferred_element_type=F32)

    psum = pc[:, head(0)]
    for k in range(1, hg):
        psum = psum + pc[:, head(k)]
    imp = jnp.dot(cmapt_ref[...], psum, preferred_element_type=F32,
                  precision=lax.Precision.HIGHEST)
    yield

    acc_w = jnp.zeros((V_ROWS, lanes), F32)
    for j in range(n_win):
        acc_w = acc_w + jnp.dot(vwt_ref[c_w + j], jnp.exp2(sw_ref[j] - m_w).astype(BF16),
                                preferred_element_type=F32)
        yield
    ow[HEAD_DIM:2 * HEAD_DIM, :] = acc_w[0:HEAD_DIM, :] / jnp.maximum(acc_w[HEAD_DIM:HEAD_DIM + 1, :], 1e-30)

    tq = t0 + lax.broadcasted_iota(jnp.int32, (N_SELB, TQ), 1)
    j_id = lax.broadcasted_iota(jnp.int32, (N_SELB, TQ), 0)
    cur = lax.shift_right_logical(tq, 6)
    forced = (j_id == 0) | (j_id == cur) | (j_id == cur - 1)
    causal = j_id <= cur
    score = jnp.where(forced, BIG, jnp.where(causal, imp, NEG))
    rank = jnp.zeros((N_SELB, TQ), jnp.int32)
    for jp in range(N_SELB):
        other = score[jp:jp + 1, :]
        beats = (other > score) | ((other == score) & (j_id > jp))
        rank = rank + beats.astype(jnp.int32)
    chosen = (rank < N_TOP) & causal
    sel_bias = jnp.where(chosen, 0.0, NEG).astype(BF16)
    for k in range(hg):
        qa[HEAD_DIM:HEAD_DIM + N_SELB, head(k)] = sel_bias
    qa[HEAD_DIM + N_SELB:K_AUG, :] = jnp.zeros((K_AUG - HEAD_DIM - N_SELB, lanes), BF16)


def _nsa_sweep(i, q_ref, gt_ref, kc_ref, vct_ref, cmapt_ref, eye_ref, cbias_ref, tri_ref, wtri_ref,
               ksa_ref, vst_ref, kwn_ref, vwt_ref, o_ref,
               qa_ref, m_ref, a_ref, acc_ref, s_buf, p_buf, sw_ref, ow_ref):
    part = slice(i * TQ, (i + 1) * TQ)
    parity = i % 2
    hg = HEADS_PER_GROUP
    lanes = hg * TQ
    head = lambda k: slice(k * TQ, (k + 1) * TQ)
    qa = qa_ref.at[i % 2]
    ow = ow_ref.at[i % 2]

    def sel_scores(c):
        return jnp.dot(ksa_ref[c * TK:(c + 1) * TK, :], qa[...], preferred_element_type=F32)

    def softmax_update(scores):
        m_old = m_ref[...]
        m_new = jnp.maximum(m_old, jnp.max(scores(), axis=0, keepdims=True))
        m_ref[...] = m_new
        return jnp.exp2(m_old - m_new), jnp.exp2(scores() - m_new).astype(BF16)

    m_ref[...] = jnp.full(m_ref.shape, NEG, F32)
    acc_ref[...] = jnp.zeros(acc_ref.shape, F32)
    s_buf[0] = sel_scores(0)
    yield

    def pv_prev(c):
        if c == 0:
            return acc_ref[...]
        pv = jnp.dot(vst_ref[c - 1], p_buf[(c - 1) % 2], preferred_element_type=F32)
        return a_ref[...] * acc_ref[...] + pv

    for c in range(i):
        cur = c % 2
        if c >= 1:
            acc_ref[...] = pv_prev(c)
        s_buf[1 - cur] = sel_scores(c + 1)
        yield
        alpha, p = softmax_update(lambda: s_buf[cur])
        p_buf[cur] = p
        a_ref[...] = alpha

    acc_prev = pv_prev(i)
    yield
    alpha, p = softmax_update(lambda: s_buf[parity] + jnp.concatenate([tri_ref[...]] * hg, axis=1))
    acc_sel = alpha * acc_prev + jnp.dot(vst_ref[i], p, preferred_element_type=F32)
    yield
    o_sel = acc_sel[0:HEAD_DIM, :] / jnp.maximum(acc_sel[HEAD_DIM:HEAD_DIM + 1, :], 1e-30)

    gt = gt_ref[:, part]
    parts = []
    for k in range(hg):
        parts.append(gt[3 * k:3 * k + 1, :] * ow[0:HEAD_DIM, head(k)]
                     + gt[3 * k + 1:3 * k + 2, :] * o_sel[:, head(k)]
                     + gt[3 * k + 2:3 * k + 3, :] * ow[HEAD_DIM:2 * HEAD_DIM, head(k)])
    o_t = jnp.concatenate(parts, axis=0).astype(BF16)
    o_ref[part, :] = lax.dot_general(eye_ref[...], o_t, _NT, preferred_element_type=F32).astype(BF16)


def _nsa(q_t, gates_t, kc, vct, cmapt, eye, cbias, tri, wtri, ksa, vst, kwn, vwt):
    bsz, d, s = q_t.shape
    ng = N_KV_GROUPS
    gw = HEADS_PER_GROUP * HEAD_DIM
    lanes = HEADS_PER_GROUP * TQ
    nt = s // TK
    assert TQ == TK and WINDOW % TK == 0 and gw == TQ and s == SEQ
    per_bg = lambda n, w: pl.BlockSpec((None, None, n, w), lambda b, g: (b, g, 0, 0))
    vt = lambda: pl.BlockSpec((None, None, nt, V_ROWS, TK), lambda b, g: (b, g, 0, 0, 0))
    const = lambda shape: pl.BlockSpec(shape, lambda b, g: (0,) * len(shape))
    return pl.pallas_call(
        _nsa_kernel,
        grid=(bsz, ng),
        in_specs=[pl.BlockSpec((None, gw, s), lambda b, g: (b, g, 0)),
                  per_bg(GATE_ROWS, s),
                  per_bg(N_CMP_PAD, HEAD_DIM), per_bg(HEAD_DIM, N_CMP_PAD),
                  const(cmapt.shape), const(eye.shape), const(cbias.shape), const(tri.shape),
                  const(wtri.shape),
                  per_bg(s, K_AUG), vt(), per_bg(s, HEAD_DIM), vt()],
        out_specs=pl.BlockSpec((None, s, gw), lambda b, g: (b, 0, g)),
        out_shape=jax.ShapeDtypeStruct((bsz, s, d), BF16),
        scratch_shapes=[pltpu.VMEM((2, K_AUG, lanes), BF16),
                        pltpu.VMEM((1, lanes), F32),
                        pltpu.VMEM((1, lanes), F32),
                        pltpu.VMEM((V_ROWS, lanes), F32),
                        pltpu.VMEM((2, TK, lanes), F32),
                        pltpu.VMEM((2, TK, lanes), BF16),
                        pltpu.VMEM((WINDOW // TK + 1, TK, lanes), F32),
                        pltpu.VMEM((2, 2 * HEAD_DIM, lanes), F32)],
        compiler_params=_cparams(("parallel", "parallel")),
        name="nsa",
    )(q_t, gates_t, kc, vct, cmapt, eye, cbias, tri, wtri, ksa, vst, kwn, vwt)


def _cmp_to_sel_t():
    c0 = np.arange(N_CMP_PAD)[None, :] * CMP_STRIDE
    s0 = np.arange(N_SELB)[:, None] * SEL_BLOCK
    ov = np.minimum(c0 + CMP_BLOCK, s0 + SEL_BLOCK) - np.maximum(c0, s0)
    m = (np.clip(ov, 0, None) / CMP_BLOCK).astype(np.float32)
    m[:, N_CMP:] = 0.0
    return m


def _cmp_bias():
    n_end = np.arange(N_CMP_PAD)[None, :, None] * CMP_STRIDE + (CMP_BLOCK - 1)
    t = np.arange(SEQ // TQ)[:, None, None] * TQ + np.arange(TQ)[None, None, :]
    return np.where(n_end <= t, 0.0, NEG).astype(np.float32)


def _tri_bias():
    r = np.arange(TK)[:, None]
    c = np.arange(TQ)[None, :]
    return np.where(r <= c, 0.0, NEG).astype(np.float32)


def _window_bias():
    n_back = WINDOW // TK
    r = np.arange(WINDOW + TQ)[:, None]
    c = np.arange(TQ)[None, :]
    tabs = [np.where(r <= i * TQ + c, 0.0, NEG) for i in range(n_back)]
    tabs.append(np.where((r > c) & (r <= c + WINDOW), 0.0, NEG))
    return np.stack(tabs).astype(np.float32)


def kernel(x, c, norm_gain, w_ada, b_ada, w_a_in, conv_w, w_a_out, w_qg, q_gain, w_o,
           kv_norm_gain, w_ada_kv, b_ada_kv, w_kv, k_gain, cmp_pe, cmp_w1, cmp_w2,
           w_mlp1, w_mlp2):
    bsz, s, d = x.shape
    assert (s, d) == (SEQ, D_MODEL)
    m = bsz * s
    x2d = x.reshape(m, d)

    def split_mod(mod, n):
        return [mod[:, k * d:(k + 1) * d].reshape(bsz, 1, d) for k in range(n)]

    sh1, sc1, g1, sh2, sc2, g2 = split_mod(_ada(c, w_ada, b_ada, 0), 6)
    x2d = _short_conv(x2d, norm_gain[0, 0].reshape(1, d), sh1, sc1, w_a_in[0].astype(BF16),
                      conv_w[0], w_a_out[0].astype(BF16), g1)
    x2d = _mlp(x2d, norm_gain[0, 1].reshape(1, d), sh2, sc2, g2,
               w_mlp1[0].astype(BF16), w_mlp2[0].astype(BF16))

    sh1, sc1, g1, sh2, sc2, g2 = split_mod(_ada(c, w_ada, b_ada, 1), 6)
    sh_kv, sc_kv = split_mod(_ada(c, w_ada_kv[None], b_ada_kv[None], 0), 2)
    nq = N_HEADS * HEAD_DIM
    wq_t = w_qg[0][:, :nq].T.astype(BF16)
    per_g = HEADS_PER_GROUP * N_BRANCH
    wg = w_qg[0][:, nq:].reshape(d, N_KV_GROUPS, per_g)
    wg = jnp.pad(wg, ((0, 0), (0, 0), (0, GATE_ROWS - per_g))).reshape(d, N_KV_GROUPS * GATE_ROWS)
    qg_b = jnp.broadcast_to(q_gain[0].reshape(HEAD_DIM, 1), (HEAD_DIM, 128))
    q_t, gates_t, kv6 = _qkv_proj(x2d.reshape(bsz, s, d), norm_gain[1, 0].reshape(1, d), sh1, sc1,
                                  kv_norm_gain.reshape(1, d), sh_kv, sc_kv, wq_t, wg.T.astype(BF16),
                                  w_kv.astype(BF16), qg_b)
    eye = jnp.eye(TQ, dtype=BF16)
    pe8 = jnp.broadcast_to(cmp_pe.reshape(2, 1, CMP_BLOCK * HEAD_DIM), (2, 8, CMP_BLOCK * HEAD_DIM))
    kc, vct, ksa, vst, kwn, vwt = _kv_prep(kv6, pe8.astype(BF16), cmp_w1.astype(BF16),
                                           cmp_w2.astype(BF16), k_gain, eye)
    attn = _nsa(q_t, gates_t, kc, vct, jnp.asarray(_cmp_to_sel_t()), eye,
                jnp.asarray(_cmp_bias()), jnp.asarray(_tri_bias()),
                jnp.asarray(_window_bias()), ksa, vst, kwn, vwt)
    x2d = _mlp(x2d, norm_gain[1, 1].reshape(1, d), sh2, sc2, g2,
               w_mlp1[1].astype(BF16), w_mlp2[1].astype(BF16),
               mixer_out=(attn.reshape(m, d), w_o[0].astype(BF16), g1))
    return x2d.reshape(bsz, s, d)
```

```python
import functools

import numpy as np
import jax
import jax.numpy as jnp
from jax import lax
from jax.experimental import pallas as pl
from jax.experimental.pallas import tpu as pltpu

D_MODEL = 1024
SEQ = 2048
CONV_WIDTH = 3
HEAD_DIM = 64
N_HEADS = D_MODEL // HEAD_DIM
N_KV_GROUPS = 4
HEADS_PER_GROUP = N_HEADS // N_KV_GROUPS
N_BRANCH = 3
CMP_BLOCK = 32
CMP_STRIDE = 16
SEL_BLOCK = 64
N_SEL = 16
WINDOW = 512
EPS = 1e-6
NEG = -1e30
BIG = 1e30

N_CMP = (SEQ - CMP_BLOCK) // CMP_STRIDE + 1
N_CMP_PAD = 128
N_SELB = SEQ // SEL_BLOCK
N_TOP = min(N_SEL, N_SELB)
SEL_SHIFT = SEL_BLOCK.bit_length() - 1
assert 1 << SEL_SHIFT == SEL_BLOCK

F32 = jnp.float32
BF16 = jnp.bfloat16

VMEM_LIMIT_BYTES = 56 * 1024 * 1024

TM_PROJ = 1024
SCONV_CHUNK_ENDS = (768, 1024)
TM_QKV = 1024
TM_MLP = 1024
TF_MLP = 1024
TQ = 256
TK = 256
N_Q_TILES = SEQ // TQ
HALO = 8
GATE_ROWS = 16
V_ROWS = HEAD_DIM + 16
K_AUG = 128
N_ROW_STREAMS = 4

_NT = (((1,), (1,)), ((), ()))

Q_SCALE = HEAD_DIM ** -0.5 * float(np.log2(np.e))


def _cparams(sem):
    return pltpu.CompilerParams(dimension_semantics=sem, vmem_limit_bytes=VMEM_LIMIT_BYTES)


def _norm_mod(x, gain, shift, scale):
    ms = jnp.mean(x * x, axis=-1, keepdims=True)
    y = x * lax.rsqrt(ms + EPS) * gain
    return y * (1.0 + scale) + shift


def _ada_kernel(c_ref, w_ref, b_ref, o_ref):
    c = c_ref[...]
    ca = c * jax.nn.sigmoid(c)
    o_ref[...] = jnp.dot(ca.astype(BF16), w_ref[...].astype(BF16),
                         preferred_element_type=F32) + b_ref[...]


def _ada(c, w, b, layer):
    bsz, d = c.shape
    n = w.shape[2]
    tn = 1024
    return pl.pallas_call(
        _ada_kernel,
        grid=(n // tn,),
        in_specs=[pl.BlockSpec((bsz, d), lambda j: (0, 0)),
                  pl.BlockSpec((None, d, tn), lambda j: (layer, 0, j)),
                  pl.BlockSpec((None, 1, tn), lambda j: (layer, 0, j))],
        out_specs=pl.BlockSpec((bsz, tn), lambda j: (0, j)),
        out_shape=jax.ShapeDtypeStruct((bsz, n), F32),
        compiler_params=_cparams(("arbitrary",)),
        name="ada",
    )(c, w, b.reshape(b.shape[0], 1, n))


def _sconv_kernel(x_ref, gain_ref, sh_ref, sc_ref, wb_ref, wc_ref, wu_ref, cw_ref, wo_ref, g_ref,
                  o_ref, vs_ref, *, per_b):
    tm, d = x_ref.shape
    x = x_ref[...]
    h = _norm_mod(x, gain_ref[...], sh_ref[...], sc_ref[...]).astype(BF16)

    @pl.when(pl.program_id(0) % per_b == 0)
    def _():
        vs_ref[0:HALO, :] = jnp.zeros((HALO, d), F32)

    cw = cw_ref[...]
    bounds = [0] + list(SCONV_CHUNK_ENDS)
    n_chunk = len(SCONV_CHUNK_ENDS)
    proj = lambda w_ref, cols: jnp.dot(h, w_ref[:, cols], preferred_element_type=F32)
    trip = lambda c: tuple(proj(w, slice(bounds[c], bounds[c + 1])) for w in (wb_ref, wc_ref, wu_ref))
    mix = None
    nxt = trip(0)
    for c in range(n_chunk):
        gb, gc, u = nxt
        if c + 1 < n_chunk:
            nxt = trip(c + 1)
        cols = slice(bounds[c], bounds[c + 1])
        vs_ref[HALO:HALO + tm, cols] = gc * u
        conv = (cw[2:3, cols] * vs_ref[HALO:HALO + tm, cols]
                + cw[1:2, cols] * vs_ref[HALO - 1:HALO - 1 + tm, cols]
                + cw[0:1, cols] * vs_ref[HALO - 2:HALO - 2 + tm, cols])
        part = jnp.dot((gb * conv).astype(BF16), wo_ref[cols, :], preferred_element_type=F32)
        mix = part if mix is None else mix + part
    o_ref[...] = x + g_ref[...] * mix
    vs_ref[0:HALO, :] = vs_ref[tm:tm + HALO, :]


def _short_conv(x2d, gain, sh, sc, w_in, conv_w, w_out, gate):
    m, d = x2d.shape
    tm = TM_PROJ
    per_b = SEQ // tm
    rows = lambda: pl.BlockSpec((tm, d), lambda i: (i, 0))
    vec = lambda: pl.BlockSpec((None, 1, d), lambda i: (i // per_b, 0, 0))
    wcol = lambda k: pl.BlockSpec((d, d), lambda i: (0, k), pipeline_mode=pl.Buffered(1))
    return pl.pallas_call(
        functools.partial(_sconv_kernel, per_b=per_b),
        grid=(m // tm,),
        in_specs=[rows(), pl.BlockSpec((1, d), lambda i: (0, 0)), vec(), vec(),
                  wcol(0), wcol(1), wcol(2),
                  pl.BlockSpec((CONV_WIDTH, d), lambda i: (0, 0)),
                  pl.BlockSpec((d, d), lambda i: (0, 0), pipeline_mode=pl.Buffered(1)),
                  vec()],
        out_specs=rows(),
        out_shape=jax.ShapeDtypeStruct((m, d), F32),
        scratch_shapes=[pltpu.VMEM((tm + HALO, d), F32)],
        compiler_params=_cparams(("arbitrary",)),
        name="short_conv",
    )(x2d, gain, sh, sc, w_in, w_in, w_in, conv_w, w_out, gate)


def _mlp_kernel(*refs, with_mixer_out):
    if with_mixer_out:
        a_ref, wo_ref, g1_ref, x_ref, gain_ref, sh_ref, sc_ref, g_ref, w1_ref, w2_ref, o_ref = refs
        x = x_ref[...] + g1_ref[...] * jnp.dot(a_ref[...], wo_ref[...], preferred_element_type=F32)
    else:
        x_ref, gain_ref, sh_ref, sc_ref, g_ref, w1_ref, w2_ref, o_ref = refs
        x = x_ref[...]
    h = _norm_mod(x, gain_ref[...], sh_ref[...], sc_ref[...]).astype(BF16)
    n_chunk = w1_ref.shape[1] // TF_MLP
    up = lambda k: jnp.dot(h, w1_ref[:, k * TF_MLP:(k + 1) * TF_MLP], preferred_element_type=F32)
    acc = None
    nxt = up(0)
    for k in range(n_chunk):
        cur = nxt
        if k + 1 < n_chunk:
            nxt = up(k + 1)
        h1 = jnp.square(jnp.maximum(cur, 0.0)).astype(BF16)
        part = jnp.dot(h1, w2_ref[k * TF_MLP:(k + 1) * TF_MLP, :], preferred_element_type=F32)
        acc = part if acc is None else acc + part
    o_ref[...] = x + g_ref[...] * acc


def _mlp(x2d, gain, sh, sc, gate, w1, w2, mixer_out=None):
    m, d = x2d.shape
    ff = w1.shape[1]
    tm = TM_MLP
    per_b = SEQ // tm
    rows = lambda: pl.BlockSpec((tm, d), lambda i: (i, 0))
    vec = lambda: pl.BlockSpec((None, 1, d), lambda i: (i // per_b, 0, 0))
    resident = lambda shape: pl.BlockSpec(shape, lambda i: (0, 0), pipeline_mode=pl.Buffered(1))
    args = [x2d, gain, sh, sc, gate, w1, w2]
    in_specs = [rows(), pl.BlockSpec((1, d), lambda i: (0, 0)), vec(), vec(), vec(),
                resident((d, ff)), resident((ff, d))]
    if mixer_out is not None:
        a2d, w_out, gate1 = mixer_out
        args = [a2d, w_out, gate1] + args
        in_specs = [rows(), resident((d, d)), vec()] + in_specs
    return pl.pallas_call(
        functools.partial(_mlp_kernel, with_mixer_out=mixer_out is not None),
        grid=(m // tm,),
        in_specs=in_specs,
        out_specs=rows(),
        out_shape=jax.ShapeDtypeStruct((m, d), F32),
        compiler_params=_cparams(("parallel",)),
        name="mlp",
    )(*args)


def _qkv_kernel(x_ref, gq_ref, shq_ref, scq_ref, gkv_ref, shkv_ref, sckv_ref,
                wq_ref, wg_ref, wkk_ref, wvt_ref, qg_ref, q_ref, gt_ref, kv_ref, vt_ref):
    x = x_ref[...]
    tm = x.shape[0]
    ms = jnp.mean(x * x, axis=-1, keepdims=True)
    y = x * lax.rsqrt(ms + EPS)
    hq = ((y * gq_ref[...]) * (1.0 + scq_ref[...]) + shq_ref[...]).astype(BF16)
    hkv = ((y * gkv_ref[...]) * (1.0 + sckv_ref[...]) + shkv_ref[...]).astype(BF16)
    q_t = lax.dot_general(wq_ref[...], hq, _NT, preferred_element_type=F32)
    qg = jnp.concatenate([qg_ref[...]] * (tm // 128), axis=1)
    for h in range(N_HEADS):
        xh = q_t[h * HEAD_DIM:(h + 1) * HEAD_DIM, :]
        msh = jnp.mean(xh * xh, axis=0, keepdims=True)
        q_ref[h * HEAD_DIM:(h + 1) * HEAD_DIM, :] = (xh * lax.rsqrt(msh + EPS) * qg * Q_SCALE).astype(BF16)
    gates_t = jax.nn.sigmoid(lax.dot_general(wg_ref[...], hq, _NT, preferred_element_type=F32))
    for g in range(N_KV_GROUPS):
        gt_ref[g] = gates_t[g * GATE_ROWS:(g + 1) * GATE_ROWS, :]
    kk = jnp.dot(hkv, wkk_ref[...], preferred_element_type=F32)
    for r in range(N_ROW_STREAMS):
        for g in range(N_KV_GROUPS):
            c0 = (r * N_KV_GROUPS + g) * HEAD_DIM
            kv_ref[r, g] = kk[:, c0:c0 + HEAD_DIM].astype(BF16)
    v_t = lax.dot_general(wvt_ref[...], hkv, _NT, preferred_element_type=F32)
    ones = jnp.ones((V_ROWS - HEAD_DIM, TK), BF16)
    for s in range(2):
        for g in range(N_KV_GROUPS):
            r0 = (s * N_KV_GROUPS + g) * HEAD_DIM
            for t in range(tm // TK):
                vt_ref[s, g, t, 0:HEAD_DIM, :] = v_t[r0:r0 + HEAD_DIM, t * TK:(t + 1) * TK].astype(BF16)
                vt_ref[s, g, t, HEAD_DIM:V_ROWS, :] = ones


def _qkv_proj(x3d, gq, shq, scq, gkv, shkv, sckv, wq, wg, wkk, wvt, qg_b):
    bsz, s, d = x3d.shape
    tm = TM_QKV
    vec = lambda: pl.BlockSpec((None, 1, d), lambda b, i: (b, 0, 0))
    const = lambda shape: pl.BlockSpec(shape, lambda b, i: (0,) * len(shape), pipeline_mode=pl.Buffered(1))
    return pl.pallas_call(
        _qkv_kernel,
        grid=(bsz, s // tm),
        in_specs=[pl.BlockSpec((None, tm, d), lambda b, i: (b, i, 0)),
                  const((1, d)), vec(), vec(),
                  const((1, d)), vec(), vec(),
                  const((d, d)), const(wg.shape), const(wkk.shape), const(wvt.shape), const(qg_b.shape)],
        out_specs=[pl.BlockSpec((None, d, tm), lambda b, i: (b, 0, i)),
                   pl.BlockSpec((None, N_KV_GROUPS, GATE_ROWS, tm), lambda b, i: (b, 0, 0, i)),
                   pl.BlockSpec((N_ROW_STREAMS, None, N_KV_GROUPS, tm, HEAD_DIM),
                                lambda b, i: (0, b, 0, i, 0)),
                   pl.BlockSpec((2, None, N_KV_GROUPS, tm // TK, V_ROWS, TK),
                                lambda b, i: (0, b, 0, i, 0, 0))],
        out_shape=[jax.ShapeDtypeStruct((bsz, d, s), BF16),
                   jax.ShapeDtypeStruct((bsz, N_KV_GROUPS, GATE_ROWS, s), F32),
                   jax.ShapeDtypeStruct((N_ROW_STREAMS, bsz, N_KV_GROUPS, s, HEAD_DIM), BF16),
                   jax.ShapeDtypeStruct((2, bsz, N_KV_GROUPS, s // TK, V_ROWS, TK), BF16)],
        compiler_params=_cparams(("parallel", "parallel")),
        name="qkv_proj",
    )(x3d, gq, shq, scq, gkv, shkv, sckv, wq, wg, wkk, wvt, qg_b)


def _head_rms(t, gain):
    ms = jnp.mean(t * t, axis=-1, keepdims=True)
    return t * lax.rsqrt(ms + EPS) * gain


def _kvprep_kernel(kcr_ref, vcr_ref, ks_ref, kw_ref,
                   pe_ref, w1_ref, w2_ref, kg_ref, eye_ref,
                   kc_ref, vct_ref, ksa_ref, kwn_ref, tok_ref):
    half = CMP_STRIDE * HEAD_DIM
    kg = kg_ref[...]
    eye_hd = eye_ref[0:HEAD_DIM, 0:HEAD_DIM]

    def compress(src_ref, idx):
        tok_ref[...] = src_ref[...].astype(F32)
        r = jnp.concatenate([tok_ref[pl.ds(l, N_CMP_PAD, stride=CMP_STRIDE), :] for l in range(CMP_STRIDE)],
                            axis=1).astype(BF16)
        z_lo = jnp.dot(r, w1_ref[idx, 0:half, :], preferred_element_type=F32)
        z_hi = jnp.dot(r, w1_ref[idx, half:2 * half, :], preferred_element_type=F32)
        z_hi = pltpu.roll(z_hi, N_CMP_PAD - 1, 0)
        pe_b = jnp.dot(pe_ref[idx], w1_ref[idx], preferred_element_type=F32)
        hid = jax.nn.gelu(z_lo + z_hi + pe_b[0:1, :]).astype(BF16)
        return jnp.dot(hid, w2_ref[idx], preferred_element_type=F32)

    kc_ref[...] = _head_rms(compress(kcr_ref, 0), kg[0:1, :]).astype(BF16)
    vc = compress(vcr_ref, 1).astype(BF16)
    vct_ref[...] = lax.dot_general(eye_hd, vc, _NT, preferred_element_type=F32).astype(BF16)

    ksn = _head_rms(ks_ref[...].astype(F32), kg[1:2, :]).astype(BF16)
    placed = jnp.dot(ksn, eye_ref[0:HEAD_DIM, 0:K_AUG], preferred_element_type=F32)
    row = lax.broadcasted_iota(jnp.int32, (SEQ, K_AUG), 0)
    col = lax.broadcasted_iota(jnp.int32, (SEQ, K_AUG), 1)
    onehot = (col - HEAD_DIM) == lax.shift_right_logical(row, SEL_SHIFT)
    ksa_ref[...] = jnp.where(onehot, 1.0, placed).astype(BF16)
    kwn_ref[...] = _head_rms(kw_ref[...].astype(F32), kg[2:3, :]).astype(BF16)


def _kv_prep(kv4, pe8, w1, w2, k_gain, eye):
    _, bsz, ng, s, hd = kv4.shape
    sel = lambda r: pl.BlockSpec((None, None, None, s, hd), lambda b, g: (r, b, g, 0, 0))
    const = lambda shape: pl.BlockSpec(shape, lambda b, g: (0,) * len(shape))
    out2 = lambda n, w: pl.BlockSpec((None, None, n, w), lambda b, g: (b, g, 0, 0))
    return pl.pallas_call(
        _kvprep_kernel,
        grid=(bsz, ng),
        in_specs=[sel(0), sel(1), sel(2), sel(3),
                  const(pe8.shape), const(w1.shape), const(w2.shape), const(k_gain.shape),
                  const(eye.shape)],
        out_specs=[out2(N_CMP_PAD, hd), out2(hd, N_CMP_PAD), out2(s, K_AUG), out2(s, hd)],
        out_shape=[jax.ShapeDtypeStruct((bsz, ng, N_CMP_PAD, hd), BF16),
                   jax.ShapeDtypeStruct((bsz, ng, hd, N_CMP_PAD), BF16),
                   jax.ShapeDtypeStruct((bsz, ng, s, K_AUG), BF16),
                   jax.ShapeDtypeStruct((bsz, ng, s, hd), BF16)],
        scratch_shapes=[pltpu.VMEM((s, hd), F32)],
        compiler_params=_cparams(("parallel", "parallel")),
        name="kv_prep",
    )(kv4, kv4, kv4, kv4, pe8, w1, w2, k_gain, eye)


def _nsa_kernel(*refs):
    for _ in _nsa_front(0, *refs):
        pass
    for i in range(N_Q_TILES):
        streams = [_nsa_sweep(i, *refs)]
        if i + 1 < N_Q_TILES:
            streams.append(_nsa_front(i + 1, *refs))
        while streams:
            for st in list(streams):
                try:
                    next(st)
                except StopIteration:
                    streams.remove(st)


def _nsa_front(i, q_ref, gt_ref, kc_ref, vct_ref, cmapt_ref, eye_ref, cbias_ref, tri_ref, wtri_ref,
               ksa_ref, vst_ref, kwn_ref, vwt_ref, o_ref,
               qa_ref, m_ref, a_ref, acc_ref, s_buf, p_buf, sw_ref, ow_ref):
    part = slice(i * TQ, (i + 1) * TQ)
    t0 = i * TQ
    hg = HEADS_PER_GROUP
    lanes = hg * TQ
    head = lambda k: slice(k * TQ, (k + 1) * TQ)
    qa = qa_ref.at[i % 2]
    ow = ow_ref.at[i % 2]

    for k in range(hg):
        qa[0:HEAD_DIM, head(k)] = q_ref[k * HEAD_DIM:(k + 1) * HEAD_DIM, part]
    qn = qa[0:HEAD_DIM, :]

    lane_t = t0 + (lax.broadcasted_iota(jnp.int32, (1, lanes), 1) & (TQ - 1))

    def k_rows(ref, c):
        return ref[c * TK:(c + 1) * TK, :]

    n_back = WINDOW // TK
    c_w = max(i - n_back, 0)
    n_win = i - c_w + 1
    sc = jnp.dot(kc_ref[...], qn, preferred_element_type=F32)
    w_bias = wtri_ref.at[min(i, n_back)]
    m_w = None
    for j in range(n_win):
        s_j = (jnp.dot(k_rows(kwn_ref, c_w + j), qn, preferred_element_type=F32)
               + jnp.concatenate([w_bias[j * TK:(j + 1) * TK, :]] * hg, axis=1))
        sw_ref[j] = s_j
        m_j = jnp.max(s_j, axis=0, keepdims=True)
        m_w = m_j if m_w is None else jnp.maximum(m_w, m_j)
        yield

    sc = sc + jnp.concatenate([cbias_ref[i]] * hg, axis=1)
    ec = jnp.exp2(sc - jnp.max(sc, axis=0, keepdims=True))
    any_valid = jnp.where(lane_t >= CMP_BLOCK - 1, 1.0, 0.0)
    pc = ec * (any_valid / jnp.maximum(jnp.sum(ec, axis=0, keepdims=True), 1e-30))
    ow[0:HEAD_DIM, :] = jnp.dot(vct_ref[...], pc.astype(BF16), preferred_element_type=F32)

    psum = pc[:, head(0)]
    for k in range(1, hg):
        psum = psum + pc[:, head(k)]
    imp = jnp.dot(cmapt_ref[...], psum, preferred_element_type=F32,
                  precision=lax.Precision.HIGHEST)
    yield

    acc_w = jnp.zeros((V_ROWS, lanes), F32)
    for j in range(n_win):
        acc_w = acc_w + jnp.dot(vwt_ref[c_w + j], jnp.exp2(sw_ref[j] - m_w).astype(BF16),
                                preferred_element_type=F32)
        yield
    ow[HEAD_DIM:2 * HEAD_DIM, :] = acc_w[0:HEAD_DIM, :] / jnp.maximum(acc_w[HEAD_DIM:HEAD_DIM + 1, :], 1e-30)

    tq = t0 + lax.broadcasted_iota(jnp.int32, (N_SELB, TQ), 1)
    j_id = lax.broadcasted_iota(jnp.int32, (N_SELB, TQ), 0)
    cur = lax.shift_right_logical(tq, SEL_SHIFT)
    forced = (j_id == 0) | (j_id == cur) | (j_id == cur - 1)
    causal = j_id <= cur
    score = jnp.where(forced, BIG, jnp.where(causal, imp, NEG))
    rank = jnp.zeros((N_SELB, TQ), jnp.int32)
    for jp in range(N_SELB):
        other = score[jp:jp + 1, :]
        beats = (other > score) | ((other == score) & (j_id > jp))
        rank = rank + beats.astype(jnp.int32)
    chosen = (rank < N_TOP) & causal
    sel_bias = jnp.where(chosen, 0.0, NEG).astype(BF16)
    for k in range(hg):
        qa[HEAD_DIM:HEAD_DIM + N_SELB, head(k)] = sel_bias
    qa[HEAD_DIM + N_SELB:K_AUG, :] = jnp.zeros((K_AUG - HEAD_DIM - N_SELB, lanes), BF16)


def _nsa_sweep(i, q_ref, gt_ref, kc_ref, vct_ref, cmapt_ref, eye_ref, cbias_ref, tri_ref, wtri_ref,
               ksa_ref, vst_ref, kwn_ref, vwt_ref, o_ref,
               qa_ref, m_ref, a_ref, acc_ref, s_buf, p_buf, sw_ref, ow_ref):
    part = slice(i * TQ, (i + 1) * TQ)
    parity = i % 2
    hg = HEADS_PER_GROUP
    lanes = hg * TQ
    head = lambda k: slice(k * TQ, (k + 1) * TQ)
    qa = qa_ref.at[i % 2]
    ow = ow_ref.at[i % 2]

    def sel_scores(c):
        return jnp.dot(ksa_ref[c * TK:(c + 1) * TK, :], qa[...], preferred_element_type=F32)

    def softmax_update(scores):
        m_old = m_ref[...]
        m_new = jnp.maximum(m_old, jnp.max(scores(), axis=0, keepdims=True))
        m_ref[...] = m_new
        return jnp.exp2(m_old - m_new), jnp.exp2(scores() - m_new).astype(BF16)

    m_ref[...] = jnp.full(m_ref.shape, NEG, F32)
    acc_ref[...] = jnp.zeros(acc_ref.shape, F32)
    s_buf[0] = sel_scores(0)
    yield

    def pv_prev(c):
        if c == 0:
            return acc_ref[...]
        pv = jnp.dot(vst_ref[c - 1], p_buf[(c - 1) % 2], preferred_element_type=F32)
        return a_ref[...] * acc_ref[...] + pv

    for c in range(i):
        cur = c % 2
        if c >= 1:
            acc_ref[...] = pv_prev(c)
        s_buf[1 - cur] = sel_scores(c + 1)
        yield
        alpha, p = softmax_update(lambda: s_buf[cur])
        p_buf[cur] = p
        a_ref[...] = alpha

    acc_prev = pv_prev(i)
    yield
    alpha, p = softmax_update(lambda: s_buf[parity] + jnp.concatenate([tri_ref[...]] * hg, axis=1))
    acc_sel = alpha * acc_prev + jnp.dot(vst_ref[i], p, preferred_element_type=F32)
    yield
    o_sel = acc_sel[0:HEAD_DIM, :] / jnp.maximum(acc_sel[HEAD_DIM:HEAD_DIM + 1, :], 1e-30)

    gt = gt_ref[:, part]
    parts = []
    for k in range(hg):
        parts.append(gt[3 * k:3 * k + 1, :] * ow[0:HEAD_DIM, head(k)]
                     + gt[3 * k + 1:3 * k + 2, :] * o_sel[:, head(k)]
                     + gt[3 * k + 2:3 * k + 3, :] * ow[HEAD_DIM:2 * HEAD_DIM, head(k)])
    o_t = jnp.concatenate(parts, axis=0).astype(BF16)
    o_ref[part, :] = lax.dot_general(eye_ref[...], o_t, _NT, preferred_element_type=F32).astype(BF16)


def _nsa(q_t, gates_t, kc, vct, cmapt, eye, cbias, tri, wtri, ksa, kwn, v_t):
    bsz, d, s = q_t.shape
    ng = N_KV_GROUPS
    gw = HEADS_PER_GROUP * HEAD_DIM
    lanes = HEADS_PER_GROUP * TQ
    nt = s // TK
    assert TQ == TK and WINDOW % TK == 0 and gw == TQ and s == SEQ
    per_bg = lambda n, w: pl.BlockSpec((None, None, n, w), lambda b, g: (b, g, 0, 0))
    vt = lambda k: pl.BlockSpec((None, None, None, nt, V_ROWS, TK), lambda b, g: (k, b, g, 0, 0, 0))
    const = lambda shape: pl.BlockSpec(shape, lambda b, g: (0,) * len(shape))
    return pl.pallas_call(
        _nsa_kernel,
        grid=(bsz, ng),
        in_specs=[pl.BlockSpec((None, gw, s), lambda b, g: (b, g, 0)),
                  per_bg(GATE_ROWS, s),
                  per_bg(N_CMP_PAD, HEAD_DIM), per_bg(HEAD_DIM, N_CMP_PAD),
                  const(cmapt.shape), const(eye.shape), const(cbias.shape), const(tri.shape),
                  const(wtri.shape),
                  per_bg(s, K_AUG), vt(0), per_bg(s, HEAD_DIM), vt(1)],
        out_specs=pl.BlockSpec((None, s, gw), lambda b, g: (b, 0, g)),
        out_shape=jax.ShapeDtypeStruct((bsz, s, d), BF16),
        scratch_shapes=[pltpu.VMEM((2, K_AUG, lanes), BF16),
                        pltpu.VMEM((1, lanes), F32),
                        pltpu.VMEM((1, lanes), F32),
                        pltpu.VMEM((V_ROWS, lanes), F32),
                        pltpu.VMEM((2, TK, lanes), F32),
                        pltpu.VMEM((2, TK, lanes), BF16),
                        pltpu.VMEM((WINDOW // TK + 1, TK, lanes), F32),
                        pltpu.VMEM((2, 2 * HEAD_DIM, lanes), F32)],
        compiler_params=_cparams(("parallel", "parallel")),
        name="nsa",
    )(q_t, gates_t, kc, vct, cmapt, eye, cbias, tri, wtri, ksa, v_t, kwn, v_t)


def _cmp_to_sel_t():
    c0 = np.arange(N_CMP_PAD)[None, :] * CMP_STRIDE
    s0 = np.arange(N_SELB)[:, None] * SEL_BLOCK
    ov = np.minimum(c0 + CMP_BLOCK, s0 + SEL_BLOCK) - np.maximum(c0, s0)
    m = (np.clip(ov, 0, None) / CMP_BLOCK).astype(np.float32)
    m[:, N_CMP:] = 0.0
    return m


def _cmp_bias():
    n_end = np.arange(N_CMP_PAD)[None, :, None] * CMP_STRIDE + (CMP_BLOCK - 1)
    t = np.arange(SEQ // TQ)[:, None, None] * TQ + np.arange(TQ)[None, None, :]
    return np.where(n_end <= t, 0.0, NEG).astype(np.float32)


def _tri_bias():
    r = np.arange(TK)[:, None]
    c = np.arange(TQ)[None, :]
    return np.where(r <= c, 0.0, NEG).astype(np.float32)


def _window_bias():
    n_back = WINDOW // TK
    r = np.arange(WINDOW + TQ)[:, None]
    c = np.arange(TQ)[None, :]
    tabs = [np.where(r <= i * TQ + c, 0.0, NEG) for i in range(n_back)]
    tabs.append(np.where((r > c) & (r <= c + WINDOW), 0.0, NEG))
    return np.stack(tabs).astype(np.float32)


def kernel(x, c, norm_gain, w_ada, b_ada, w_a_in, conv_w, w_a_out, w_qg, q_gain, w_o,
           kv_norm_gain, w_ada_kv, b_ada_kv, w_kv, k_gain, cmp_pe, cmp_w1, cmp_w2,
           w_mlp1, w_mlp2):
    bsz, s, d = x.shape
    assert (s, d) == (SEQ, D_MODEL)
    m = bsz * s
    x2d = x.reshape(m, d)

    def split_mod(mod, n):
        return [mod[:, k * d:(k + 1) * d].reshape(bsz, 1, d) for k in range(n)]

    sh1, sc1, g1, sh2, sc2, g2 = split_mod(_ada(c, w_ada, b_ada, 0), 6)
    x2d = _short_conv(x2d, norm_gain[0, 0].reshape(1, d), sh1, sc1, w_a_in[0].astype(BF16),
                      conv_w[0], w_a_out[0].astype(BF16), g1)
    x2d = _mlp(x2d, norm_gain[0, 1].reshape(1, d), sh2, sc2, g2,
               w_mlp1[0].astype(BF16), w_mlp2[0].astype(BF16))

    sh1, sc1, g1, sh2, sc2, g2 = split_mod(_ada(c, w_ada, b_ada, 1), 6)
    sh_kv, sc_kv = split_mod(_ada(c, w_ada_kv[None], b_ada_kv[None], 0), 2)
    nq = N_HEADS * HEAD_DIM
    wq_t = w_qg[0][:, :nq].T.astype(BF16)
    per_g = HEADS_PER_GROUP * N_BRANCH
    wg = w_qg[0][:, nq:].reshape(d, N_KV_GROUPS, per_g)
    wg = jnp.pad(wg, ((0, 0), (0, 0), (0, GATE_ROWS - per_g))).reshape(d, N_KV_GROUPS * GATE_ROWS)
    qg_b = jnp.broadcast_to(q_gain[0].reshape(HEAD_DIM, 1), (HEAD_DIM, 128))
    gcols = N_KV_GROUPS * HEAD_DIM
    stream = lambda r: w_kv[:, r * gcols:(r + 1) * gcols]
    wkk = jnp.concatenate([stream(0), stream(1), stream(2), stream(4)], axis=1).astype(BF16)
    wvt = jnp.concatenate([stream(3), stream(5)], axis=1).T.astype(BF16)
    q_t, gates_t, kv4, v_t = _qkv_proj(x2d.reshape(bsz, s, d), norm_gain[1, 0].reshape(1, d), sh1, sc1,
                                       kv_norm_gain.reshape(1, d), sh_kv, sc_kv, wq_t,
                                       wg.T.astype(BF16), wkk, wvt, qg_b)
    eye = jnp.eye(TQ, dtype=BF16)
    pe8 = jnp.broadcast_to(cmp_pe.reshape(2, 1, CMP_BLOCK * HEAD_DIM), (2, 8, CMP_BLOCK * HEAD_DIM))
    kc, vct, ksa, kwn = _kv_prep(kv4, pe8.astype(BF16), cmp_w1.astype(BF16),
                                 cmp_w2.astype(BF16), k_gain, eye)
    attn = _nsa(q_t, gates_t, kc, vct, jnp.asarray(_cmp_to_sel_t()), eye,
                jnp.asarray(_cmp_bias()), jnp.asarray(_tri_bias()),
                jnp.asarray(_window_bias()), ksa, kwn, v_t)
    x2d = _mlp(x2d, norm_gain[1, 1].reshape(1, d), sh2, sc2, g2,
               w_mlp1[1].astype(BF16), w_mlp2[1].astype(BF16),
               mixer_out=(attn.reshape(m, d), w_o[0].astype(BF16), g1))
    return x2d.reshape(bsz, s, d)
```

```python
import functools

import numpy as np
import jax
import jax.numpy as jnp
from jax import lax
from jax.experimental import pallas as pl
from jax.experimental.pallas import tpu as pltpu

D_MODEL = 1024
SEQ = 2048
CONV_WIDTH = 3
HEAD_DIM = 64
N_HEADS = D_MODEL // HEAD_DIM
N_KV_GROUPS = 4
HEADS_PER_GROUP = N_HEADS // N_KV_GROUPS
N_BRANCH = 3
CMP_BLOCK = 32
CMP_STRIDE = 16
SEL_BLOCK = 64
N_SEL = 16
WINDOW = 512
EPS = 1e-6
NEG = -1e30
BIG = 1e30

N_CMP = (SEQ - CMP_BLOCK) // CMP_STRIDE + 1
N_CMP_PAD = 128
N_SELB = SEQ // SEL_BLOCK
N_TOP = min(N_SEL, N_SELB)
SEL_SHIFT = SEL_BLOCK.bit_length() - 1
assert 1 << SEL_SHIFT == SEL_BLOCK

F32 = jnp.float32
BF16 = jnp.bfloat16

VMEM_LIMIT_BYTES = 56 * 1024 * 1024

TM_PROJ = 1024
SCONV_CHUNK_ENDS = (768, 1024)
TM_QKV = 1024
TM_MLP = 1024
TF_MLP = 1024
TQ = 256
TK = 256
N_Q_TILES = SEQ // TQ
HALO = 8
GATE_ROWS = 16
V_ROWS = HEAD_DIM + 16
K_AUG = 128
N_ROW_STREAMS = 4

_NT = (((1,), (1,)), ((), ()))

Q_SCALE = HEAD_DIM ** -0.5 * float(np.log2(np.e))


def _cparams(sem):
    return pltpu.CompilerParams(dimension_semantics=sem, vmem_limit_bytes=VMEM_LIMIT_BYTES)


def _norm_mod(x, gain, shift, scale):
    ms = jnp.mean(x * x, axis=-1, keepdims=True)
    y = x * lax.rsqrt(ms + EPS) * gain
    return y * (1.0 + scale) + shift


def _ada_kernel(c_ref, w_ref, b_ref, o_ref):
    c = c_ref[...]
    ca = c * jax.nn.sigmoid(c)
    o_ref[...] = jnp.dot(ca.astype(BF16), w_ref[...].astype(BF16),
                         preferred_element_type=F32) + b_ref[...]


def _ada(c, w, b, layer):
    bsz, d = c.shape
    n = w.shape[2]
    tn = 1024
    return pl.pallas_call(
        _ada_kernel,
        grid=(n // tn,),
        in_specs=[pl.BlockSpec((bsz, d), lambda j: (0, 0)),
                  pl.BlockSpec((None, d, tn), lambda j: (layer, 0, j)),
                  pl.BlockSpec((None, 1, tn), lambda j: (layer, 0, j))],
        out_specs=pl.BlockSpec((bsz, tn), lambda j: (0, j)),
        out_shape=jax.ShapeDtypeStruct((bsz, n), F32),
        compiler_params=_cparams(("arbitrary",)),
        name="ada",
    )(c, w, b.reshape(b.shape[0], 1, n))


def _sconv_kernel(x_ref, gain_ref, sh_ref, sc_ref, wb_ref, wc_ref, wu_ref, cw_ref, wo_ref, g_ref,
                  o_ref, vs_ref, *, per_b):
    tm, d = x_ref.shape
    x = x_ref[...]
    h = _norm_mod(x, gain_ref[...], sh_ref[...], sc_ref[...]).astype(BF16)

    @pl.when(pl.program_id(0) % per_b == 0)
    def _():
        vs_ref[0:HALO, :] = jnp.zeros((HALO, d), F32)

    cw = cw_ref[...]
    bounds = [0] + list(SCONV_CHUNK_ENDS)
    n_chunk = len(SCONV_CHUNK_ENDS)
    proj = lambda w_ref, cols: jnp.dot(h, w_ref[:, cols], preferred_element_type=F32)
    trip = lambda c: tuple(proj(w, slice(bounds[c], bounds[c + 1])) for w in (wb_ref, wc_ref, wu_ref))
    mix = None
    nxt = trip(0)
    for c in range(n_chunk):
        gb, gc, u = nxt
        if c + 1 < n_chunk:
            nxt = trip(c + 1)
        cols = slice(bounds[c], bounds[c + 1])
        vs_ref[HALO:HALO + tm, cols] = gc * u
        conv = (cw[2:3, cols] * vs_ref[HALO:HALO + tm, cols]
                + cw[1:2, cols] * vs_ref[HALO - 1:HALO - 1 + tm, cols]
                + cw[0:1, cols] * vs_ref[HALO - 2:HALO - 2 + tm, cols])
        part = jnp.dot((gb * conv).astype(BF16), wo_ref[cols, :], preferred_element_type=F32)
        mix = part if mix is None else mix + part
    o_ref[...] = x + g_ref[...] * mix
    vs_ref[0:HALO, :] = vs_ref[tm:tm + HALO, :]


def _short_conv(x2d, gain, sh, sc, w_in, conv_w, w_out, gate):
    m, d = x2d.shape
    tm = TM_PROJ
    per_b = SEQ // tm
    rows = lambda: pl.BlockSpec((tm, d), lambda i: (i, 0))
    vec = lambda: pl.BlockSpec((None, 1, d), lambda i: (i // per_b, 0, 0))
    wcol = lambda k: pl.BlockSpec((d, d), lambda i: (0, k), pipeline_mode=pl.Buffered(1))
    return pl.pallas_call(
        functools.partial(_sconv_kernel, per_b=per_b),
        grid=(m // tm,),
        in_specs=[rows(), pl.BlockSpec((1, d), lambda i: (0, 0)), vec(), vec(),
                  wcol(0), wcol(1), wcol(2),
                  pl.BlockSpec((CONV_WIDTH, d), lambda i: (0, 0)),
                  pl.BlockSpec((d, d), lambda i: (0, 0), pipeline_mode=pl.Buffered(1)),
                  vec()],
        out_specs=rows(),
        out_shape=jax.ShapeDtypeStruct((m, d), F32),
        scratch_shapes=[pltpu.VMEM((tm + HALO, d), F32)],
        compiler_params=_cparams(("arbitrary",)),
        name="short_conv",
    )(x2d, gain, sh, sc, w_in, w_in, w_in, conv_w, w_out, gate)


def _mlp_kernel(*refs, with_mixer_out):
    if with_mixer_out:
        a_ref, wo_ref, g1_ref, x_ref, gain_ref, sh_ref, sc_ref, g_ref, w1_ref, w2_ref, o_ref = refs
        x = x_ref[...] + g1_ref[...] * jnp.dot(a_ref[...], wo_ref[...], preferred_element_type=F32)
    else:
        x_ref, gain_ref, sh_ref, sc_ref, g_ref, w1_ref, w2_ref, o_ref = refs
        x = x_ref[...]
    h = _norm_mod(x, gain_ref[...], sh_ref[...], sc_ref[...]).astype(BF16)
    n_chunk = w1_ref.shape[1] // TF_MLP
    up = lambda k: jnp.dot(h, w1_ref[:, k * TF_MLP:(k + 1) * TF_MLP], preferred_element_type=F32)
    acc = None
    nxt = up(0)
    for k in range(n_chunk):
        cur = nxt
        if k + 1 < n_chunk:
            nxt = up(k + 1)
        h1 = jnp.square(jnp.maximum(cur, 0.0)).astype(BF16)
        part = jnp.dot(h1, w2_ref[k * TF_MLP:(k + 1) * TF_MLP, :], preferred_element_type=F32)
        acc = part if acc is None else acc + part
    o_ref[...] = x + g_ref[...] * acc


def _mlp(x2d, gain, sh, sc, gate, w1, w2, mixer_out=None):
    m, d = x2d.shape
    ff = w1.shape[1]
    tm = TM_MLP
    per_b = SEQ // tm
    rows = lambda: pl.BlockSpec((tm, d), lambda i: (i, 0))
    vec = lambda: pl.BlockSpec((None, 1, d), lambda i: (i // per_b, 0, 0))
    resident = lambda shape: pl.BlockSpec(shape, lambda i: (0, 0), pipeline_mode=pl.Buffered(1))
    args = [x2d, gain, sh, sc, gate, w1, w2]
    in_specs = [rows(), pl.BlockSpec((1, d), lambda i: (0, 0)), vec(), vec(), vec(),
                resident((d, ff)), resident((ff, d))]
    if mixer_out is not None:
        a2d, w_out, gate1 = mixer_out
        args = [a2d, w_out, gate1] + args
        in_specs = [rows(), resident((d, d)), vec()] + in_specs
    return pl.pallas_call(
        functools.partial(_mlp_kernel, with_mixer_out=mixer_out is not None),
        grid=(m // tm,),
        in_specs=in_specs,
        out_specs=rows(),
        out_shape=jax.ShapeDtypeStruct((m, d), F32),
        compiler_params=_cparams(("parallel",)),
        name="mlp",
    )(*args)


def _qkv_kernel(x_ref, gq_ref, shq_ref, scq_ref, gkv_ref, shkv_ref, sckv_ref,
                wq_ref, wg_ref, wkk_ref, wvt_ref, qg_ref, q_ref, gt_ref, kv_ref, vt_ref):
    x = x_ref[...]
    tm = x.shape[0]
    ms = jnp.mean(x * x, axis=-1, keepdims=True)
    y = x * lax.rsqrt(ms + EPS)
    hq = ((y * gq_ref[...]) * (1.0 + scq_ref[...]) + shq_ref[...]).astype(BF16)
    hkv = ((y * gkv_ref[...]) * (1.0 + sckv_ref[...]) + shkv_ref[...]).astype(BF16)
    q_t = lax.dot_general(wq_ref[...], hq, _NT, preferred_element_type=F32)
    qg = jnp.concatenate([qg_ref[...]] * (tm // 128), axis=1)
    for h in range(N_HEADS):
        xh = q_t[h * HEAD_DIM:(h + 1) * HEAD_DIM, :]
        msh = jnp.mean(xh * xh, axis=0, keepdims=True)
        q_ref[h * HEAD_DIM:(h + 1) * HEAD_DIM, :] = (xh * lax.rsqrt(msh + EPS) * qg * Q_SCALE).astype(BF16)
    gates_t = jax.nn.sigmoid(lax.dot_general(wg_ref[...], hq, _NT, preferred_element_type=F32))
    for g in range(N_KV_GROUPS):
        gt_ref[g] = gates_t[g * GATE_ROWS:(g + 1) * GATE_ROWS, :]
    kk = jnp.dot(hkv, wkk_ref[...], preferred_element_type=F32)
    for r in range(N_ROW_STREAMS):
        for g in range(N_KV_GROUPS):
            c0 = (r * N_KV_GROUPS + g) * HEAD_DIM
            kv_ref[r, g] = kk[:, c0:c0 + HEAD_DIM].astype(BF16)
    v_t = lax.dot_general(wvt_ref[...], hkv, _NT, preferred_element_type=F32)
    ones = jnp.ones((V_ROWS - HEAD_DIM, TK), BF16)
    for s in range(2):
        for g in range(N_KV_GROUPS):
            r0 = (s * N_KV_GROUPS + g) * HEAD_DIM
            for t in range(tm // TK):
                vt_ref[s, g, t, 0:HEAD_DIM, :] = v_t[r0:r0 + HEAD_DIM, t * TK:(t + 1) * TK].astype(BF16)
                vt_ref[s, g, t, HEAD_DIM:V_ROWS, :] = ones


def _qkv_proj(x3d, gq, shq, scq, gkv, shkv, sckv, wq, wg, wkk, wvt, qg_b):
    bsz, s, d = x3d.shape
    tm = TM_QKV
    vec = lambda: pl.BlockSpec((None, 1, d), lambda b, i: (b, 0, 0))
    const = lambda shape: pl.BlockSpec(shape, lambda b, i: (0,) * len(shape), pipeline_mode=pl.Buffered(1))
    return pl.pallas_call(
        _qkv_kernel,
        grid=(bsz, s // tm),
        in_specs=[pl.BlockSpec((None, tm, d), lambda b, i: (b, i, 0)),
                  const((1, d)), vec(), vec(),
                  const((1, d)), vec(), vec(),
                  const((d, d)), const(wg.shape), const(wkk.shape), const(wvt.shape), const(qg_b.shape)],
        out_specs=[pl.BlockSpec((None, d, tm), lambda b, i: (b, 0, i)),
                   pl.BlockSpec((None, N_KV_GROUPS, GATE_ROWS, tm), lambda b, i: (b, 0, 0, i)),
                   pl.BlockSpec((N_ROW_STREAMS, None, N_KV_GROUPS, tm, HEAD_DIM),
                                lambda b, i: (0, b, 0, i, 0)),
                   pl.BlockSpec((2, None, N_KV_GROUPS, tm // TK, V_ROWS, TK),
                                lambda b, i: (0, b, 0, i, 0, 0))],
        out_shape=[jax.ShapeDtypeStruct((bsz, d, s), BF16),
                   jax.ShapeDtypeStruct((bsz, N_KV_GROUPS, GATE_ROWS, s), F32),
                   jax.ShapeDtypeStruct((N_ROW_STREAMS, bsz, N_KV_GROUPS, s, HEAD_DIM), BF16),
                   jax.ShapeDtypeStruct((2, bsz, N_KV_GROUPS, s // TK, V_ROWS, TK), BF16)],
        compiler_params=_cparams(("parallel", "parallel")),
        name="qkv_proj",
    )(x3d, gq, shq, scq, gkv, shkv, sckv, wq, wg, wkk, wvt, qg_b)


def _head_rms(t, gain):
    ms = jnp.mean(t * t, axis=-1, keepdims=True)
    return t * lax.rsqrt(ms + EPS) * gain


def _kvprep_kernel(kcr_ref, vcr_ref, ks_ref, kw_ref,
                   pe_ref, w1_ref, w2_ref, kg_ref, eye_ref,
                   kc_ref, vct_ref, ksa_ref, kwn_ref, tok_ref):
    half = CMP_STRIDE * HEAD_DIM
    kg = kg_ref[...]
    eye_hd = eye_ref[0:HEAD_DIM, 0:HEAD_DIM]

    def compress(src_ref, idx):
        tok_ref[...] = src_ref[...].astype(F32)
        r = jnp.concatenate([tok_ref[pl.ds(l, N_CMP_PAD, stride=CMP_STRIDE), :] for l in range(CMP_STRIDE)],
                            axis=1).astype(BF16)
        z_lo = jnp.dot(r, w1_ref[idx, 0:half, :], preferred_element_type=F32)
        z_hi = jnp.dot(r, w1_ref[idx, half:2 * half, :], preferred_element_type=F32)
        z_hi = pltpu.roll(z_hi, N_CMP_PAD - 1, 0)
        pe_b = jnp.dot(pe_ref[idx], w1_ref[idx], preferred_element_type=F32)
        hid = jax.nn.gelu(z_lo + z_hi + pe_b[0:1, :]).astype(BF16)
        return jnp.dot(hid, w2_ref[idx], preferred_element_type=F32)

    kc_ref[...] = _head_rms(compress(kcr_ref, 0), kg[0:1, :]).astype(BF16)
    vc = compress(vcr_ref, 1).astype(BF16)
    vct_ref[...] = lax.dot_general(eye_hd, vc, _NT, preferred_element_type=F32).astype(BF16)

    ksn = _head_rms(ks_ref[...].astype(F32), kg[1:2, :]).astype(BF16)
    placed = jnp.dot(ksn, eye_ref[0:HEAD_DIM, 0:K_AUG], preferred_element_type=F32)
    row = lax.broadcasted_iota(jnp.int32, (SEQ, K_AUG), 0)
    col = lax.broadcasted_iota(jnp.int32, (SEQ, K_AUG), 1)
    onehot = (col - HEAD_DIM) == lax.shift_right_logical(row, SEL_SHIFT)
    ksa_ref[...] = jnp.where(onehot, 1.0, placed).astype(BF16)
    kwn_ref[...] = _head_rms(kw_ref[...].astype(F32), kg[2:3, :]).astype(BF16)


def _kv_prep(kv4, pe8, w1, w2, k_gain, eye):
    _, bsz, ng, s, hd = kv4.shape
    sel = lambda r: pl.BlockSpec((None, None, None, s, hd), lambda b, g: (r, b, g, 0, 0))
    const = lambda shape: pl.BlockSpec(shape, lambda b, g: (0,) * len(shape))
    out2 = lambda n, w: pl.BlockSpec((None, None, n, w), lambda b, g: (b, g, 0, 0))
    return pl.pallas_call(
        _kvprep_kernel,
        grid=(bsz, ng),
        in_specs=[sel(0), sel(1), sel(2), sel(3),
                  const(pe8.shape), const(w1.shape), const(w2.shape), const(k_gain.shape),
                  const(eye.shape)],
        out_specs=[out2(N_CMP_PAD, hd), out2(hd, N_CMP_PAD), out2(s, K_AUG), out2(s, hd)],
        out_shape=[jax.ShapeDtypeStruct((bsz, ng, N_CMP_PAD, hd), BF16),
                   jax.ShapeDtypeStruct((bsz, ng, hd, N_CMP_PAD), BF16),
                   jax.ShapeDtypeStruct((bsz, ng, s, K_AUG), BF16),
                   jax.ShapeDtypeStruct((bsz, ng, s, hd), BF16)],
        scratch_shapes=[pltpu.VMEM((s, hd), F32)],
        compiler_params=_cparams(("parallel", "parallel")),
        name="kv_prep",
    )(kv4, kv4, kv4, kv4, pe8, w1, w2, k_gain, eye)


def _nsa_kernel(*refs):
    for which in ("window", "choice"):
        for _ in _nsa_front(0, which, *refs):
            pass
    for i in range(N_Q_TILES):
        streams = [_nsa_sweep(i, *refs)]
        if i + 1 < N_Q_TILES:
            streams += [_nsa_front(i + 1, which, *refs) for which in ("window", "choice")]
        while streams:
            for st in list(streams):
                try:
                    next(st)
                except StopIteration:
                    streams.remove(st)


def _nsa_front(i, which, q_ref, gt_ref, kc_ref, vct_ref, cmapt_ref, eye_ref, cbias_ref, tri_ref, wtri_ref,
               ksa_ref, vst_ref, kwn_ref, vwt_ref, o_ref,
               qa_ref, m_ref, a_ref, acc_ref, s_buf, p_buf, sw_ref, ow_ref):
    part = slice(i * TQ, (i + 1) * TQ)
    t0 = i * TQ
    hg = HEADS_PER_GROUP
    lanes = hg * TQ
    head = lambda k: slice(k * TQ, (k + 1) * TQ)
    qa = qa_ref.at[i % 2]
    ow = ow_ref.at[i % 2]

    if which == "window":
        for k in range(hg):
            qa[0:HEAD_DIM, head(k)] = q_ref[k * HEAD_DIM:(k + 1) * HEAD_DIM, part]
        qn = qa[0:HEAD_DIM, :]

        n_back = WINDOW // TK
        c_w = max(i - n_back, 0)
        n_win = i - c_w + 1
        w_bias = wtri_ref.at[min(i, n_back)]
        m_w = None
        for j in range(n_win):
            s_j = (jnp.dot(kwn_ref[(c_w + j) * TK:(c_w + j + 1) * TK, :], qn, preferred_element_type=F32)
                   + jnp.concatenate([w_bias[j * TK:(j + 1) * TK, :]] * hg, axis=1))
            sw_ref[j] = s_j
            m_j = jnp.max(s_j, axis=0, keepdims=True)
            m_w = m_j if m_w is None else jnp.maximum(m_w, m_j)
            yield
        acc_w = jnp.zeros((V_ROWS, lanes), F32)
        for j in range(n_win):
            acc_w = acc_w + jnp.dot(vwt_ref[c_w + j], jnp.exp2(sw_ref[j] - m_w).astype(BF16),
                                    preferred_element_type=F32)
            yield
        ow[HEAD_DIM:2 * HEAD_DIM, :] = (acc_w[0:HEAD_DIM, :]
                                        / jnp.maximum(acc_w[HEAD_DIM:HEAD_DIM + 1, :], 1e-30))
        return

    qn = qa[0:HEAD_DIM, :]
    lane_t = t0 + (lax.broadcasted_iota(jnp.int32, (1, lanes), 1) & (TQ - 1))
    sc = jnp.dot(kc_ref[...], qn, preferred_element_type=F32)
    yield

    sc = sc + jnp.concatenate([cbias_ref[i]] * hg, axis=1)
    ec = jnp.exp2(sc - jnp.max(sc, axis=0, keepdims=True))
    any_valid = jnp.where(lane_t >= CMP_BLOCK - 1, 1.0, 0.0)
    pc = ec * (any_valid / jnp.maximum(jnp.sum(ec, axis=0, keepdims=True), 1e-30))
    ow[0:HEAD_DIM, :] = jnp.dot(vct_ref[...], pc.astype(BF16), preferred_element_type=F32)

    psum = pc[:, head(0)]
    for k in range(1, hg):
        psum = psum + pc[:, head(k)]
    imp = jnp.dot(cmapt_ref[...], psum, preferred_element_type=F32,
                  precision=lax.Precision.HIGHEST)
    yield

    tq = t0 + lax.broadcasted_iota(jnp.int32, (N_SELB, TQ), 1)
    j_id = lax.broadcasted_iota(jnp.int32, (N_SELB, TQ), 0)
    cur = lax.shift_right_logical(tq, SEL_SHIFT)
    forced = (j_id == 0) | (j_id == cur) | (j_id == cur - 1)
    causal = j_id <= cur
    score = jnp.where(forced, BIG, jnp.where(causal, imp, NEG))
    rank = jnp.zeros((N_SELB, TQ), jnp.int32)
    for jp in range(N_SELB):
        other = score[jp:jp + 1, :]
        beats = (other > score) | ((other == score) & (j_id > jp))
        rank = rank + beats.astype(jnp.int32)
    chosen = (rank < N_TOP) & causal
    sel_bias = jnp.where(chosen, 0.0, NEG).astype(BF16)
    for k in range(hg):
        qa[HEAD_DIM:HEAD_DIM + N_SELB, head(k)] = sel_bias
    qa[HEAD_DIM + N_SELB:K_AUG, :] = jnp.zeros((K_AUG - HEAD_DIM - N_SELB, lanes), BF16)


def _nsa_sweep(i, q_ref, gt_ref, kc_ref, vct_ref, cmapt_ref, eye_ref, cbias_ref, tri_ref, wtri_ref,
               ksa_ref, vst_ref, kwn_ref, vwt_ref, o_ref,
               qa_ref, m_ref, a_ref, acc_ref, s_buf, p_buf, sw_ref, ow_ref):
    part = slice(i * TQ, (i + 1) * TQ)
    parity = i % 2
    hg = HEADS_PER_GROUP
    lanes = hg * TQ
    head = lambda k: slice(k * TQ, (k + 1) * TQ)
    qa = qa_ref.at[i % 2]
    ow = ow_ref.at[i % 2]

    def sel_scores(c):
        return jnp.dot(ksa_ref[c * TK:(c + 1) * TK, :], qa[...], preferred_element_type=F32)

    def softmax_update(scores):
        m_old = m_ref[...]
        m_new = jnp.maximum(m_old, jnp.max(scores(), axis=0, keepdims=True))
        m_ref[...] = m_new
        return jnp.exp2(m_old - m_new), jnp.exp2(scores() - m_new).astype(BF16)

    m_ref[...] = jnp.full(m_ref.shape, NEG, F32)
    acc_ref[...] = jnp.zeros(acc_ref.shape, F32)
    s_buf[0] = sel_scores(0)
    yield

    def pv_prev(c):
        if c == 0:
            return acc_ref[...]
        pv = jnp.dot(vst_ref[c - 1], p_buf[(c - 1) % 2], preferred_element_type=F32)
        return a_ref[...] * acc_ref[...] + pv

    for c in range(i):
        cur = c % 2
        if c >= 1:
            acc_ref[...] = pv_prev(c)
        s_buf[1 - cur] = sel_scores(c + 1)
        yield
        alpha, p = softmax_update(lambda: s_buf[cur])
        p_buf[cur] = p
        a_ref[...] = alpha

    acc_prev = pv_prev(i)
    yield
    alpha, p = softmax_update(lambda: s_buf[parity] + jnp.concatenate([tri_ref[...]] * hg, axis=1))
    acc_sel = alpha * acc_prev + jnp.dot(vst_ref[i], p, preferred_element_type=F32)
    yield
    o_sel = acc_sel[0:HEAD_DIM, :] / jnp.maximum(acc_sel[HEAD_DIM:HEAD_DIM + 1, :], 1e-30)

    gt = gt_ref[:, part]
    parts = []
    for k in range(hg):
        parts.append(gt[3 * k:3 * k + 1, :] * ow[0:HEAD_DIM, head(k)]
                     + gt[3 * k + 1:3 * k + 2, :] * o_sel[:, head(k)]
                     + gt[3 * k + 2:3 * k + 3, :] * ow[HEAD_DIM:2 * HEAD_DIM, head(k)])
    o_t = jnp.concatenate(parts, axis=0).astype(BF16)
    o_ref[part, :] = lax.dot_general(eye_ref[...], o_t, _NT, preferred_element_type=F32).astype(BF16)


def _nsa(q_t, gates_t, kc, vct, cmapt, eye, cbias, tri, wtri, ksa, kwn, v_t):
    bsz, d, s = q_t.shape
    ng = N_KV_GROUPS
    gw = HEADS_PER_GROUP * HEAD_DIM
    lanes = HEADS_PER_GROUP * TQ
    nt = s // TK
    assert TQ == TK and WINDOW % TK == 0 and gw == TQ and s == SEQ
    per_bg = lambda n, w: pl.BlockSpec((None, None, n, w), lambda b, g: (b, g, 0, 0))
    vt = lambda k: pl.BlockSpec((None, None, None, nt, V_ROWS, TK), lambda b, g: (k, b, g, 0, 0, 0))
    const = lambda shape: pl.BlockSpec(shape, lambda b, g: (0,) * len(shape))
    return pl.pallas_call(
        _nsa_kernel,
        grid=(bsz, ng),
        in_specs=[pl.BlockSpec((None, gw, s), lambda b, g: (b, g, 0)),
                  per_bg(GATE_ROWS, s),
                  per_bg(N_CMP_PAD, HEAD_DIM), per_bg(HEAD_DIM, N_CMP_PAD),
                  const(cmapt.shape), const(eye.shape), const(cbias.shape), const(tri.shape),
                  const(wtri.shape),
                  per_bg(s, K_AUG), vt(0), per_bg(s, HEAD_DIM), vt(1)],
        out_specs=pl.BlockSpec((None, s, gw), lambda b, g: (b, 0, g)),
        out_shape=jax.ShapeDtypeStruct((bsz, s, d), BF16),
        scratch_shapes=[pltpu.VMEM((2, K_AUG, lanes), BF16),
                        pltpu.VMEM((1, lanes), F32),
                        pltpu.VMEM((1, lanes), F32),
                        pltpu.VMEM((V_ROWS, lanes), F32),
                        pltpu.VMEM((2, TK, lanes), F32),
                        pltpu.VMEM((2, TK, lanes), BF16),
                        pltpu.VMEM((WINDOW // TK + 1, TK, lanes), F32),
                        pltpu.VMEM((2, 2 * HEAD_DIM, lanes), F32)],
        compiler_params=_cparams(("parallel", "parallel")),
        name="nsa",
    )(q_t, gates_t, kc, vct, cmapt, eye, cbias, tri, wtri, ksa, v_t, kwn, v_t)


def _cmp_to_sel_t():
    c0 = np.arange(N_CMP_PAD)[None, :] * CMP_STRIDE
    s0 = np.arange(N_SELB)[:, None] * SEL_BLOCK
    ov = np.minimum(c0 + CMP_BLOCK, s0 + SEL_BLOCK) - np.maximum(c0, s0)
    m = (np.clip(ov, 0, None) / CMP_BLOCK).astype(np.float32)
    m[:, N_CMP:] = 0.0
    return m


def _cmp_bias():
    n_end = np.arange(N_CMP_PAD)[None, :, None] * CMP_STRIDE + (CMP_BLOCK - 1)
    t = np.arange(SEQ // TQ)[:, None, None] * TQ + np.arange(TQ)[None, None, :]
    return np.where(n_end <= t, 0.0, NEG).astype(np.float32)


def _tri_bias():
    r = np.arange(TK)[:, None]
    c = np.arange(TQ)[None, :]
    return np.where(r <= c, 0.0, NEG).astype(np.float32)


def _window_bias():
    n_back = WINDOW // TK
    r = np.arange(WINDOW + TQ)[:, None]
    c = np.arange(TQ)[None, :]
    tabs = [np.where(r <= i * TQ + c, 0.0, NEG) for i in range(n_back)]
    tabs.append(np.where((r > c) & (r <= c + WINDOW), 0.0, NEG))
    return np.stack(tabs).astype(np.float32)


def kernel(x, c, norm_gain, w_ada, b_ada, w_a_in, conv_w, w_a_out, w_qg, q_gain, w_o,
           kv_norm_gain, w_ada_kv, b_ada_kv, w_kv, k_gain, cmp_pe, cmp_w1, cmp_w2,
           w_mlp1, w_mlp2):
    bsz, s, d = x.shape
    assert (s, d) == (SEQ, D_MODEL)
    m = bsz * s
    x2d = x.reshape(m, d)

    def split_mod(mod, n):
        return [mod[:, k * d:(k + 1) * d].reshape(bsz, 1, d) for k in range(n)]

    sh1, sc1, g1, sh2, sc2, g2 = split_mod(_ada(c, w_ada, b_ada, 0), 6)
    x2d = _short_conv(x2d, norm_gain[0, 0].reshape(1, d), sh1, sc1, w_a_in[0].astype(BF16),
                      conv_w[0], w_a_out[0].astype(BF16), g1)
    x2d = _mlp(x2d, norm_gain[0, 1].reshape(1, d), sh2, sc2, g2,
               w_mlp1[0].astype(BF16), w_mlp2[0].astype(BF16))

    sh1, sc1, g1, sh2, sc2, g2 = split_mod(_ada(c, w_ada, b_ada, 1), 6)
    sh_kv, sc_kv = split_mod(_ada(c, w_ada_kv[None], b_ada_kv[None], 0), 2)
    nq = N_HEADS * HEAD_DIM
    wq_t = w_qg[0][:, :nq].T.astype(BF16)
    per_g = HEADS_PER_GROUP * N_BRANCH
    wg = w_qg[0][:, nq:].reshape(d, N_KV_GROUPS, per_g)
    wg = jnp.pad(wg, ((0, 0), (0, 0), (0, GATE_ROWS - per_g))).reshape(d, N_KV_GROUPS * GATE_ROWS)
    qg_b = jnp.broadcast_to(q_gain[0].reshape(HEAD_DIM, 1), (HEAD_DIM, 128))
    gcols = N_KV_GROUPS * HEAD_DIM
    stream = lambda r: w_kv[:, r * gcols:(r + 1) * gcols]
    wkk = jnp.concatenate([stream(0), stream(1), stream(2), stream(4)], axis=1).astype(BF16)
    wvt = jnp.concatenate([stream(3), stream(5)], axis=1).T.astype(BF16)
    q_t, gates_t, kv4, v_t = _qkv_proj(x2d.reshape(bsz, s, d), norm_gain[1, 0].reshape(1, d), sh1, sc1,
                                       kv_norm_gain.reshape(1, d), sh_kv, sc_kv, wq_t,
                                       wg.T.astype(BF16), wkk, wvt, qg_b)
    eye = jnp.eye(TQ, dtype=BF16)
    pe8 = jnp.broadcast_to(cmp_pe.reshape(2, 1, CMP_BLOCK * HEAD_DIM), (2, 8, CMP_BLOCK * HEAD_DIM))
    kc, vct, ksa, kwn = _kv_prep(kv4, pe8.astype(BF16), cmp_w1.astype(BF16),
                                 cmp_w2.astype(BF16), k_gain, eye)
    attn = _nsa(q_t, gates_t, kc, vct, jnp.asarray(_cmp_to_sel_t()), eye,
                jnp.asarray(_cmp_bias()), jnp.asarray(_tri_bias()),
                jnp.asarray(_window_bias()), ksa, kwn, v_t)
    x2d = _mlp(x2d, norm_gain[1, 1].reshape(1, d), sh2, sc2, g2,
               w_mlp1[1].astype(BF16), w_mlp2[1].astype(BF16),
               mixer_out=(attn.reshape(m, d), w_o[0].astype(BF16), g1))
    return x2d.reshape(bsz, s, d)
```

```python
import functools

import numpy as np
import jax
import jax.numpy as jnp
from jax import lax
from jax.experimental import pallas as pl
from jax.experimental.pallas import tpu as pltpu

D_MODEL = 1024
SEQ = 2048
CONV_WIDTH = 3
HEAD_DIM = 64
N_HEADS = D_MODEL // HEAD_DIM
N_KV_GROUPS = 4
HEADS_PER_GROUP = N_HEADS // N_KV_GROUPS
N_BRANCH = 3
CMP_BLOCK = 32
CMP_STRIDE = 16
SEL_BLOCK = 64
N_SEL = 16
WINDOW = 512
EPS = 1e-6
NEG = -1e30
BIG = 1e30

N_CMP = (SEQ - CMP_BLOCK) // CMP_STRIDE + 1
N_CMP_PAD = 128
N_SELB = SEQ // SEL_BLOCK
N_TOP = min(N_SEL, N_SELB)
SEL_SHIFT = SEL_BLOCK.bit_length() - 1
assert 1 << SEL_SHIFT == SEL_BLOCK

F32 = jnp.float32
BF16 = jnp.bfloat16

VMEM_LIMIT_BYTES = 56 * 1024 * 1024

TM_PROJ = 1024
SCONV_CHUNK_ENDS = (768, 1024)
TM_QKV = 1024
TM_MLP = 1024
TF_MLP = 1024
TQ = 256
TK = 256
N_Q_TILES = SEQ // TQ
HALO = 8
GATE_ROWS = 16
V_ROWS = HEAD_DIM + 16
K_AUG = 128
N_ROW_STREAMS = 4

_NT = (((1,), (1,)), ((), ()))

Q_SCALE = HEAD_DIM ** -0.5 * float(np.log2(np.e))


def _cparams(sem):
    return pltpu.CompilerParams(dimension_semantics=sem, vmem_limit_bytes=VMEM_LIMIT_BYTES)


def _norm_mod(x, gain, shift, scale):
    ms = jnp.mean(x * x, axis=-1, keepdims=True)
    y = x * lax.rsqrt(ms + EPS) * gain
    return y * (1.0 + scale) + shift


def _ada_kernel(c_ref, w_ref, b_ref, o_ref):
    c = c_ref[...]
    ca = c * jax.nn.sigmoid(c)
    o_ref[...] = jnp.dot(ca.astype(BF16), w_ref[...].astype(BF16),
                         preferred_element_type=F32) + b_ref[...]


def _ada(c, w, b, layer):
    bsz, d = c.shape
    n = w.shape[2]
    tn = 1024
    return pl.pallas_call(
        _ada_kernel,
        grid=(n // tn,),
        in_specs=[pl.BlockSpec((bsz, d), lambda j: (0, 0)),
                  pl.BlockSpec((None, d, tn), lambda j: (layer, 0, j)),
                  pl.BlockSpec((None, 1, tn), lambda j: (layer, 0, j))],
        out_specs=pl.BlockSpec((bsz, tn), lambda j: (0, j)),
        out_shape=jax.ShapeDtypeStruct((bsz, n), F32),
        compiler_params=_cparams(("arbitrary",)),
        name="ada",
    )(c, w, b.reshape(b.shape[0], 1, n))


def _sconv_kernel(x_ref, gain_ref, sh_ref, sc_ref, wb_ref, wc_ref, wu_ref, cw_ref, wo_ref, g_ref,
                  o_ref, vs_ref, *, per_b):
    tm, d = x_ref.shape
    x = x_ref[...]
    h = _norm_mod(x, gain_ref[...], sh_ref[...], sc_ref[...]).astype(BF16)

    @pl.when(pl.program_id(0) % per_b == 0)
    def _():
        vs_ref[0:HALO, :] = jnp.zeros((HALO, d), F32)

    cw = cw_ref[...]
    bounds = [0] + list(SCONV_CHUNK_ENDS)
    n_chunk = len(SCONV_CHUNK_ENDS)
    proj = lambda w_ref, cols: jnp.dot(h, w_ref[:, cols], preferred_element_type=F32)
    trip = lambda c: tuple(proj(w, slice(bounds[c], bounds[c + 1])) for w in (wb_ref, wc_ref, wu_ref))
    mix = None
    nxt = trip(0)
    for c in range(n_chunk):
        gb, gc, u = nxt
        if c + 1 < n_chunk:
            nxt = trip(c + 1)
        cols = slice(bounds[c], bounds[c + 1])
        vs_ref[HALO:HALO + tm, cols] = gc * u
        conv = (cw[2:3, cols] * vs_ref[HALO:HALO + tm, cols]
                + cw[1:2, cols] * vs_ref[HALO - 1:HALO - 1 + tm, cols]
                + cw[0:1, cols] * vs_ref[HALO - 2:HALO - 2 + tm, cols])
        part = jnp.dot((gb * conv).astype(BF16), wo_ref[cols, :], preferred_element_type=F32)
        mix = part if mix is None else mix + part
    o_ref[...] = x + g_ref[...] * mix
    vs_ref[0:HALO, :] = vs_ref[tm:tm + HALO, :]


def _short_conv(x2d, gain, sh, sc, w_in, conv_w, w_out, gate):
    m, d = x2d.shape
    tm = TM_PROJ
    per_b = SEQ // tm
    rows = lambda: pl.BlockSpec((tm, d), lambda i: (i, 0))
    vec = lambda: pl.BlockSpec((None, 1, d), lambda i: (i // per_b, 0, 0))
    wcol = lambda k: pl.BlockSpec((d, d), lambda i: (0, k), pipeline_mode=pl.Buffered(1))
    return pl.pallas_call(
        functools.partial(_sconv_kernel, per_b=per_b),
        grid=(m // tm,),
        in_specs=[rows(), pl.BlockSpec((1, d), lambda i: (0, 0)), vec(), vec(),
                  wcol(0), wcol(1), wcol(2),
                  pl.BlockSpec((CONV_WIDTH, d), lambda i: (0, 0)),
                  pl.BlockSpec((d, d), lambda i: (0, 0), pipeline_mode=pl.Buffered(1)),
                  vec()],
        out_specs=rows(),
        out_shape=jax.ShapeDtypeStruct((m, d), F32),
        scratch_shapes=[pltpu.VMEM((tm + HALO, d), F32)],
        compiler_params=_cparams(("arbitrary",)),
        name="short_conv",
    )(x2d, gain, sh, sc, w_in, w_in, w_in, conv_w, w_out, gate)


def _mlp_kernel(*refs, with_mixer_out):
    if with_mixer_out:
        a_ref, wo_ref, g1_ref, x_ref, gain_ref, sh_ref, sc_ref, g_ref, w1_ref, w2_ref, o_ref = refs
        x = x_ref[...] + g1_ref[...] * jnp.dot(a_ref[...], wo_ref[...], preferred_element_type=F32)
    else:
        x_ref, gain_ref, sh_ref, sc_ref, g_ref, w1_ref, w2_ref, o_ref = refs
        x = x_ref[...]
    h = _norm_mod(x, gain_ref[...], sh_ref[...], sc_ref[...]).astype(BF16)
    n_chunk = w1_ref.shape[1] // TF_MLP
    up = lambda k: jnp.dot(h, w1_ref[:, k * TF_MLP:(k + 1) * TF_MLP], preferred_element_type=F32)
    acc = None
    nxt = up(0)
    for k in range(n_chunk):
        cur = nxt
        if k + 1 < n_chunk:
            nxt = up(k + 1)
        h1 = jnp.square(jnp.maximum(cur, 0.0)).astype(BF16)
        part = jnp.dot(h1, w2_ref[k * TF_MLP:(k + 1) * TF_MLP, :], preferred_element_type=F32)
        acc = part if acc is None else acc + part
    o_ref[...] = x + g_ref[...] * acc


def _mlp(x2d, gain, sh, sc, gate, w1, w2, mixer_out=None):
    m, d = x2d.shape
    ff = w1.shape[1]
    tm = TM_MLP
    per_b = SEQ // tm
    rows = lambda: pl.BlockSpec((tm, d), lambda i: (i, 0))
    vec = lambda: pl.BlockSpec((None, 1, d), lambda i: (i // per_b, 0, 0))
    resident = lambda shape: pl.BlockSpec(shape, lambda i: (0, 0), pipeline_mode=pl.Buffered(1))
    args = [x2d, gain, sh, sc, gate, w1, w2]
    in_specs = [rows(), pl.BlockSpec((1, d), lambda i: (0, 0)), vec(), vec(), vec(),
                resident((d, ff)), resident((ff, d))]
    if mixer_out is not None:
        a2d, w_out, gate1 = mixer_out
        args = [a2d, w_out, gate1] + args
        in_specs = [rows(), resident((d, d)), vec()] + in_specs
    return pl.pallas_call(
        functools.partial(_mlp_kernel, with_mixer_out=mixer_out is not None),
        grid=(m // tm,),
        in_specs=in_specs,
        out_specs=rows(),
        out_shape=jax.ShapeDtypeStruct((m, d), F32),
        compiler_params=_cparams(("parallel",)),
        name="mlp",
    )(*args)


def _qkv_kernel(x_ref, gq_ref, shq_ref, scq_ref, gkv_ref, shkv_ref, sckv_ref,
                wq_ref, wg_ref, wkk_ref, wvt_ref, qg_ref, q_ref, gt_ref, kv_ref, vt_ref):
    x = x_ref[...]
    tm = x.shape[0]
    ms = jnp.mean(x * x, axis=-1, keepdims=True)
    y = x * lax.rsqrt(ms + EPS)
    hq = ((y * gq_ref[...]) * (1.0 + scq_ref[...]) + shq_ref[...]).astype(BF16)
    hkv = ((y * gkv_ref[...]) * (1.0 + sckv_ref[...]) + shkv_ref[...]).astype(BF16)
    q_t = lax.dot_general(wq_ref[...], hq, _NT, preferred_element_type=F32)
    qg = jnp.concatenate([qg_ref[...]] * (tm // 128), axis=1)
    for h in range(N_HEADS):
        xh = q_t[h * HEAD_DIM:(h + 1) * HEAD_DIM, :]
        msh = jnp.mean(xh * xh, axis=0, keepdims=True)
        q_ref[h * HEAD_DIM:(h + 1) * HEAD_DIM, :] = (xh * lax.rsqrt(msh + EPS) * qg * Q_SCALE).astype(BF16)
    gates_t = jax.nn.sigmoid(lax.dot_general(wg_ref[...], hq, _NT, preferred_element_type=F32))
    for g in range(N_KV_GROUPS):
        gt_ref[g] = gates_t[g * GATE_ROWS:(g + 1) * GATE_ROWS, :]
    kk = jnp.dot(hkv, wkk_ref[...], preferred_element_type=F32)
    for r in range(N_ROW_STREAMS):
        for g in range(N_KV_GROUPS):
            c0 = (r * N_KV_GROUPS + g) * HEAD_DIM
            kv_ref[r, g] = kk[:, c0:c0 + HEAD_DIM].astype(BF16)
    v_t = lax.dot_general(wvt_ref[...], hkv, _NT, preferred_element_type=F32)
    ones = jnp.ones((V_ROWS - HEAD_DIM, TK), BF16)
    for s in range(2):
        for g in range(N_KV_GROUPS):
            r0 = (s * N_KV_GROUPS + g) * HEAD_DIM
            for t in range(tm // TK):
                vt_ref[s, g, t, 0:HEAD_DIM, :] = v_t[r0:r0 + HEAD_DIM, t * TK:(t + 1) * TK].astype(BF16)
                vt_ref[s, g, t, HEAD_DIM:V_ROWS, :] = ones


def _qkv_proj(x3d, gq, shq, scq, gkv, shkv, sckv, wq, wg, wkk, wvt, qg_b):
    bsz, s, d = x3d.shape
    tm = TM_QKV
    vec = lambda: pl.BlockSpec((None, 1, d), lambda b, i: (b, 0, 0))
    const = lambda shape: pl.BlockSpec(shape, lambda b, i: (0,) * len(shape), pipeline_mode=pl.Buffered(1))
    return pl.pallas_call(
        _qkv_kernel,
        grid=(bsz, s // tm),
        in_specs=[pl.BlockSpec((None, tm, d), lambda b, i: (b, i, 0)),
                  const((1, d)), vec(), vec(),
                  const((1, d)), vec(), vec(),
                  const((d, d)), const(wg.shape), const(wkk.shape), const(wvt.shape), const(qg_b.shape)],
        out_specs=[pl.BlockSpec((None, d, tm), lambda b, i: (b, 0, i)),
                   pl.BlockSpec((None, N_KV_GROUPS, GATE_ROWS, tm), lambda b, i: (b, 0, 0, i)),
                   pl.BlockSpec((N_ROW_STREAMS, None, N_KV_GROUPS, tm, HEAD_DIM),
                                lambda b, i: (0, b, 0, i, 0)),
                   pl.BlockSpec((2, None, N_KV_GROUPS, tm // TK, V_ROWS, TK),
                                lambda b, i: (0, b, 0, i, 0, 0))],
        out_shape=[jax.ShapeDtypeStruct((bsz, d, s), BF16),
                   jax.ShapeDtypeStruct((bsz, N_KV_GROUPS, GATE_ROWS, s), F32),
                   jax.ShapeDtypeStruct((N_ROW_STREAMS, bsz, N_KV_GROUPS, s, HEAD_DIM), BF16),
                   jax.ShapeDtypeStruct((2, bsz, N_KV_GROUPS, s // TK, V_ROWS, TK), BF16)],
        compiler_params=_cparams(("parallel", "parallel")),
        name="qkv_proj",
    )(x3d, gq, shq, scq, gkv, shkv, sckv, wq, wg, wkk, wvt, qg_b)


def _head_rms(t, gain):
    ms = jnp.mean(t * t, axis=-1, keepdims=True)
    return t * lax.rsqrt(ms + EPS) * gain


def _kvprep_kernel(kcr_ref, vcr_ref, ks_ref, kw_ref,
                   pe_ref, w1_ref, w2_ref, kg_ref, eye_ref,
                   kc_ref, vct_ref, ksa_ref, kwn_ref, tok_ref):
    half = CMP_STRIDE * HEAD_DIM
    kg = kg_ref[...]
    eye_hd = eye_ref[0:HEAD_DIM, 0:HEAD_DIM]

    def compress(src_ref, idx):
        tok_ref[...] = src_ref[...].astype(F32)
        r = jnp.concatenate([tok_ref[pl.ds(l, N_CMP_PAD, stride=CMP_STRIDE), :] for l in range(CMP_STRIDE)],
                            axis=1).astype(BF16)
        z_lo = jnp.dot(r, w1_ref[idx, 0:half, :], preferred_element_type=F32)
        z_hi = jnp.dot(r, w1_ref[idx, half:2 * half, :], preferred_element_type=F32)
        z_hi = pltpu.roll(z_hi, N_CMP_PAD - 1, 0)
        pe_b = jnp.dot(pe_ref[idx], w1_ref[idx], preferred_element_type=F32)
        hid = jax.nn.gelu(z_lo + z_hi + pe_b[0:1, :]).astype(BF16)
        return jnp.dot(hid, w2_ref[idx], preferred_element_type=F32)

    kc_ref[...] = _head_rms(compress(kcr_ref, 0), kg[0:1, :]).astype(BF16)
    vc = compress(vcr_ref, 1).astype(BF16)
    vct_ref[...] = lax.dot_general(eye_hd, vc, _NT, preferred_element_type=F32).astype(BF16)

    ksn = _head_rms(ks_ref[...].astype(F32), kg[1:2, :]).astype(BF16)
    placed = jnp.dot(ksn, eye_ref[0:HEAD_DIM, 0:K_AUG], preferred_element_type=F32)
    row = lax.broadcasted_iota(jnp.int32, (SEQ, K_AUG), 0)
    col = lax.broadcasted_iota(jnp.int32, (SEQ, K_AUG), 1)
    onehot = (col - HEAD_DIM) == lax.shift_right_logical(row, SEL_SHIFT)
    ksa_ref[...] = jnp.where(onehot, 1.0, placed).astype(BF16)
    kwn_ref[...] = _head_rms(kw_ref[...].astype(F32), kg[2:3, :]).astype(BF16)


def _kv_prep(kv4, pe8, w1, w2, k_gain, eye):
    _, bsz, ng, s, hd = kv4.shape
    sel = lambda r: pl.BlockSpec((None, None, None, s, hd), lambda b, g: (r, b, g, 0, 0))
    const = lambda shape: pl.BlockSpec(shape, lambda b, g: (0,) * len(shape))
    out2 = lambda n, w: pl.BlockSpec((None, None, n, w), lambda b, g: (b, g, 0, 0))
    return pl.pallas_call(
        _kvprep_kernel,
        grid=(bsz, ng),
        in_specs=[sel(0), sel(1), sel(2), sel(3),
                  const(pe8.shape), const(w1.shape), const(w2.shape), const(k_gain.shape),
                  const(eye.shape)],
        out_specs=[out2(N_CMP_PAD, hd), out2(hd, N_CMP_PAD), out2(s, K_AUG), out2(s, hd)],
        out_shape=[jax.ShapeDtypeStruct((bsz, ng, N_CMP_PAD, hd), BF16),
                   jax.ShapeDtypeStruct((bsz, ng, hd, N_CMP_PAD), BF16),
                   jax.ShapeDtypeStruct((bsz, ng, s, K_AUG), BF16),
                   jax.ShapeDtypeStruct((bsz, ng, s, hd), BF16)],
        scratch_shapes=[pltpu.VMEM((s, hd), F32)],
        compiler_params=_cparams(("parallel", "parallel")),
        name="kv_prep",
    )(kv4, kv4, kv4, kv4, pe8, w1, w2, k_gain, eye)


def _nsa_kernel(*refs):
    for _ in _nsa_front(0, *refs):
        pass
    for i in range(N_Q_TILES):
        streams = [_nsa_sweep(i, *refs)]
        if i + 1 < N_Q_TILES:
            streams.append(_nsa_front(i + 1, *refs))
        while streams:
            for st in list(streams):
                try:
                    next(st)
                except StopIteration:
                    streams.remove(st)


def _nsa_front(i, q_ref, gt_ref, kc_ref, vct_ref, cmapt_ref, eye_ref, cbias_ref, tri_ref, wtri_ref,
               ksa_ref, vst_ref, kwn_ref, vwt_ref, o_ref,
               qa_ref, m_ref, a_ref, acc_ref, s_buf, p_buf, sw_ref, ow_ref):
    part = slice(i * TQ, (i + 1) * TQ)
    t0 = i * TQ
    hg = HEADS_PER_GROUP
    lanes = hg * TQ
    head = lambda k: slice(k * TQ, (k + 1) * TQ)
    qa = qa_ref.at[i % 2]
    ow = ow_ref.at[i % 2]

    for k in range(hg):
        qa[0:HEAD_DIM, head(k)] = q_ref[k * HEAD_DIM:(k + 1) * HEAD_DIM, part]
    qn = qa[0:HEAD_DIM, :]

    lane_t = t0 + (lax.broadcasted_iota(jnp.int32, (1, lanes), 1) & (TQ - 1))

    def k_rows(ref, c):
        return ref[c * TK:(c + 1) * TK, :]

    n_back = WINDOW // TK
    c_w = max(i - n_back, 0)
    n_win = i - c_w + 1
    sc = jnp.dot(kc_ref[...], qn, preferred_element_type=F32)
    w_bias = wtri_ref.at[min(i, n_back)]
    m_w = None
    for j in range(n_win):
        s_j = (jnp.dot(k_rows(kwn_ref, c_w + j), qn, preferred_element_type=F32)
               + jnp.concatenate([w_bias[j * TK:(j + 1) * TK, :]] * hg, axis=1))
        sw_ref[j] = s_j
        m_j = jnp.max(s_j, axis=0, keepdims=True)
        m_w = m_j if m_w is None else jnp.maximum(m_w, m_j)
        yield

    sc = sc + jnp.concatenate([cbias_ref[i]] * hg, axis=1)
    ec = jnp.exp2(sc - jnp.max(sc, axis=0, keepdims=True))
    any_valid = jnp.where(lane_t >= CMP_BLOCK - 1, 1.0, 0.0)
    pc = ec * (any_valid / jnp.maximum(jnp.sum(ec, axis=0, keepdims=True), 1e-30))
    ow[0:HEAD_DIM, :] = jnp.dot(vct_ref[...], pc.astype(BF16), preferred_element_type=F32)

    psum = pc[:, head(0)]
    for k in range(1, hg):
        psum = psum + pc[:, head(k)]
    imp = jnp.dot(cmapt_ref[...], psum, preferred_element_type=F32,
                  precision=lax.Precision.HIGHEST)
    yield

    acc_w = jnp.zeros((V_ROWS, lanes), F32)
    for j in range(n_win):
        acc_w = acc_w + jnp.dot(vwt_ref[c_w + j], jnp.exp2(sw_ref[j] - m_w).astype(BF16),
                                preferred_element_type=F32)
        yield
    ow[HEAD_DIM:2 * HEAD_DIM, :] = acc_w[0:HEAD_DIM, :] / jnp.maximum(acc_w[HEAD_DIM:HEAD_DIM + 1, :], 1e-30)

    tq = t0 + lax.broadcasted_iota(jnp.int32, (N_SELB, TQ), 1)
    j_id = lax.broadcasted_iota(jnp.int32, (N_SELB, TQ), 0)
    cur = lax.shift_right_logical(tq, SEL_SHIFT)
    forced = (j_id == 0) | (j_id == cur) | (j_id == cur - 1)
    causal = j_id <= cur
    score = jnp.where(forced, BIG, jnp.where(causal, imp, NEG))
    rank = jnp.zeros((N_SELB, TQ), jnp.int32)
    for jp in range(N_SELB):
        other = score[jp:jp + 1, :]
        beats = (other > score) | ((other == score) & (j_id > jp))
        rank = rank + beats.astype(jnp.int32)
    chosen = (rank < N_TOP) & causal
    sel_bias = jnp.where(chosen, 0.0, NEG).astype(BF16)
    for k in range(hg):
        qa[HEAD_DIM:HEAD_DIM + N_SELB, head(k)] = sel_bias
    qa[HEAD_DIM + N_SELB:K_AUG, :] = jnp.zeros((K_AUG - HEAD_DIM - N_SELB, lanes), BF16)


def _nsa_sweep(i, q_ref, gt_ref, kc_ref, vct_ref, cmapt_ref, eye_ref, cbias_ref, tri_ref, wtri_ref,
               ksa_ref, vst_ref, kwn_ref, vwt_ref, o_ref,
               qa_ref, m_ref, a_ref, acc_ref, s_buf, p_buf, sw_ref, ow_ref):
    part = slice(i * TQ, (i + 1) * TQ)
    parity = i % 2
    hg = HEADS_PER_GROUP
    lanes = hg * TQ
    head = lambda k: slice(k * TQ, (k + 1) * TQ)
    qa = qa_ref.at[i % 2]
    ow = ow_ref.at[i % 2]

    def sel_scores(c):
        return jnp.dot(ksa_ref[c * TK:(c + 1) * TK, :], qa[...], preferred_element_type=F32)

    def softmax_update(scores):
        m_old = m_ref[...]
        m_new = jnp.maximum(m_old, jnp.max(scores(), axis=0, keepdims=True))
        m_ref[...] = m_new
        return jnp.exp2(m_old - m_new), jnp.exp2(scores() - m_new).astype(BF16)

    m_ref[...] = jnp.full(m_ref.shape, NEG, F32)
    acc_ref[...] = jnp.zeros(acc_ref.shape, F32)
    s_buf[0] = sel_scores(0)
    yield

    def pv_prev(c):
        if c == 0:
            return acc_ref[...]
        pv = jnp.dot(vst_ref[c - 1], p_buf[(c - 1) % 2], preferred_element_type=F32)
        return a_ref[...] * acc_ref[...] + pv

    for c in range(i):
        cur = c % 2
        if c >= 1:
            acc_ref[...] = pv_prev(c)
            yield
        s_buf[1 - cur] = sel_scores(c + 1)
        yield
        alpha, p = softmax_update(lambda: s_buf[cur])
        p_buf[cur] = p
        a_ref[...] = alpha

    acc_prev = pv_prev(i)
    yield
    alpha, p = softmax_update(lambda: s_buf[parity] + jnp.concatenate([tri_ref[...]] * hg, axis=1))
    acc_sel = alpha * acc_prev + jnp.dot(vst_ref[i], p, preferred_element_type=F32)
    yield
    o_sel = acc_sel[0:HEAD_DIM, :] / jnp.maximum(acc_sel[HEAD_DIM:HEAD_DIM + 1, :], 1e-30)

    gt = gt_ref[:, part]
    parts = []
    for k in range(hg):
        parts.append(gt[3 * k:3 * k + 1, :] * ow[0:HEAD_DIM, head(k)]
                     + gt[3 * k + 1:3 * k + 2, :] * o_sel[:, head(k)]
                     + gt[3 * k + 2:3 * k + 3, :] * ow[HEAD_DIM:2 * HEAD_DIM, head(k)])
    o_t = jnp.concatenate(parts, axis=0).astype(BF16)
    o_ref[part, :] = lax.dot_general(eye_ref[...], o_t, _NT, preferred_element_type=F32).astype(BF16)


def _nsa(q_t, gates_t, kc, vct, cmapt, eye, cbias, tri, wtri, ksa, kwn, v_t):
    bsz, d, s = q_t.shape
    ng = N_KV_GROUPS
    gw = HEADS_PER_GROUP * HEAD_DIM
    lanes = HEADS_PER_GROUP * TQ
    nt = s // TK
    assert TQ == TK and WINDOW % TK == 0 and gw == TQ and s == SEQ
    per_bg = lambda n, w: pl.BlockSpec((None, None, n, w), lambda b, g: (b, g, 0, 0))
    vt = lambda k: pl.BlockSpec((None, None, None, nt, V_ROWS, TK), lambda b, g: (k, b, g, 0, 0, 0))
    const = lambda shape: pl.BlockSpec(shape, lambda b, g: (0,) * len(shape))
    return pl.pallas_call(
        _nsa_kernel,
        grid=(bsz, ng),
        in_specs=[pl.BlockSpec((None, gw, s), lambda b, g: (b, g, 0)),
                  per_bg(GATE_ROWS, s),
                  per_bg(N_CMP_PAD, HEAD_DIM), per_bg(HEAD_DIM, N_CMP_PAD),
                  const(cmapt.shape), const(eye.shape), const(cbias.shape), const(tri.shape),
                  const(wtri.shape),
                  per_bg(s, K_AUG), vt(0), per_bg(s, HEAD_DIM), vt(1)],
        out_specs=pl.BlockSpec((None, s, gw), lambda b, g: (b, 0, g)),
        out_shape=jax.ShapeDtypeStruct((bsz, s, d), BF16),
        scratch_shapes=[pltpu.VMEM((2, K_AUG, lanes), BF16),
                        pltpu.VMEM((1, lanes), F32),
                        pltpu.VMEM((1, lanes), F32),
                        pltpu.VMEM((V_ROWS, lanes), F32),
                        pltpu.VMEM((2, TK, lanes), F32),
                        pltpu.VMEM((2, TK, lanes), BF16),
                        pltpu.VMEM((WINDOW // TK + 1, TK, lanes), F32),
                        pltpu.VMEM((2, 2 * HEAD_DIM, lanes), F32)],
        compiler_params=_cparams(("parallel", "parallel")),
        name="nsa",
    )(q_t, gates_t, kc, vct, cmapt, eye, cbias, tri, wtri, ksa, v_t, kwn, v_t)


def _cmp_to_sel_t():
    c0 = np.arange(N_CMP_PAD)[None, :] * CMP_STRIDE
    s0 = np.arange(N_SELB)[:, None] * SEL_BLOCK
    ov = np.minimum(c0 + CMP_BLOCK, s0 + SEL_BLOCK) - np.maximum(c0, s0)
    m = (np.clip(ov, 0, None) / CMP_BLOCK).astype(np.float32)
    m[:, N_CMP:] = 0.0
    return m


def _cmp_bias():
    n_end = np.arange(N_CMP_PAD)[None, :, None] * CMP_STRIDE + (CMP_BLOCK - 1)
    t = np.arange(SEQ // TQ)[:, None, None] * TQ + np.arange(TQ)[None, None, :]
    return np.where(n_end <= t, 0.0, NEG).astype(np.float32)


def _tri_bias():
    r = np.arange(TK)[:, None]
    c = np.arange(TQ)[None, :]
    return np.where(r <= c, 0.0, NEG).astype(np.float32)


def _window_bias():
    n_back = WINDOW // TK
    r = np.arange(WINDOW + TQ)[:, None]
    c = np.arange(TQ)[None, :]
    tabs = [np.where(r <= i * TQ + c, 0.0, NEG) for i in range(n_back)]
    tabs.append(np.where((r > c) & (r <= c + WINDOW), 0.0, NEG))
    return np.stack(tabs).astype(np.float32)


def kernel(x, c, norm_gain, w_ada, b_ada, w_a_in, conv_w, w_a_out, w_qg, q_gain, w_o,
           kv_norm_gain, w_ada_kv, b_ada_kv, w_kv, k_gain, cmp_pe, cmp_w1, cmp_w2,
           w_mlp1, w_mlp2):
    bsz, s, d = x.shape
    assert (s, d) == (SEQ, D_MODEL)
    m = bsz * s
    x2d = x.reshape(m, d)

    def split_mod(mod, n):
        return [mod[:, k * d:(k + 1) * d].reshape(bsz, 1, d) for k in range(n)]

    sh1, sc1, g1, sh2, sc2, g2 = split_mod(_ada(c, w_ada, b_ada, 0), 6)
    x2d = _short_conv(x2d, norm_gain[0, 0].reshape(1, d), sh1, sc1, w_a_in[0].astype(BF16),
                      conv_w[0], w_a_out[0].astype(BF16), g1)
    x2d = _mlp(x2d, norm_gain[0, 1].reshape(1, d), sh2, sc2, g2,
               w_mlp1[0].astype(BF16), w_mlp2[0].astype(BF16))

    sh1, sc1, g1, sh2, sc2, g2 = split_mod(_ada(c, w_ada, b_ada, 1), 6)
    sh_kv, sc_kv = split_mod(_ada(c, w_ada_kv[None], b_ada_kv[None], 0), 2)
    nq = N_HEADS * HEAD_DIM
    wq_t = w_qg[0][:, :nq].T.astype(BF16)
    per_g = HEADS_PER_GROUP * N_BRANCH
    wg = w_qg[0][:, nq:].reshape(d, N_KV_GROUPS, per_g)
    wg = jnp.pad(wg, ((0, 0), (0, 0), (0, GATE_ROWS - per_g))).reshape(d, N_KV_GROUPS * GATE_ROWS)
    qg_b = jnp.broadcast_to(q_gain[0].reshape(HEAD_DIM, 1), (HEAD_DIM, 128))
    gcols = N_KV_GROUPS * HEAD_DIM
    stream = lambda r: w_kv[:, r * gcols:(r + 1) * gcols]
    wkk = jnp.concatenate([stream(0), stream(1), stream(2), stream(4)], axis=1).astype(BF16)
    wvt = jnp.concatenate([stream(3), stream(5)], axis=1).T.astype(BF16)
    q_t, gates_t, kv4, v_t = _qkv_proj(x2d.reshape(bsz, s, d), norm_gain[1, 0].reshape(1, d), sh1, sc1,
                                       kv_norm_gain.reshape(1, d), sh_kv, sc_kv, wq_t,
                                       wg.T.astype(BF16), wkk, wvt, qg_b)
    eye = jnp.eye(TQ, dtype=BF16)
    pe8 = jnp.broadcast_to(cmp_pe.reshape(2, 1, CMP_BLOCK * HEAD_DIM), (2, 8, CMP_BLOCK * HEAD_DIM))
    kc, vct, ksa, kwn = _kv_prep(kv4, pe8.astype(BF16), cmp_w1.astype(BF16),
                                 cmp_w2.astype(BF16), k_gain, eye)
    attn = _nsa(q_t, gates_t, kc, vct, jnp.asarray(_cmp_to_sel_t()), eye,
                jnp.asarray(_cmp_bias()), jnp.asarray(_tri_bias()),
                jnp.asarray(_window_bias()), ksa, kwn, v_t)
    x2d = _mlp(x2d, norm_gain[1, 1].reshape(1, d), sh2, sc2, g2,
               w_mlp1[1].astype(BF16), w_mlp2[1].astype(BF16),
               mixer_out=(attn.reshape(m, d), w_o[0].astype(BF16), g1))
    return x2d.reshape(bsz, s, d)
```

```python
import functools

import numpy as np
import jax
import jax.numpy as jnp
from jax import lax
from jax.experimental import pallas as pl
from jax.experimental.pallas import tpu as pltpu

D_MODEL = 1024
SEQ = 2048
CONV_WIDTH = 3
HEAD_DIM = 64
N_HEADS = D_MODEL // HEAD_DIM
N_KV_GROUPS = 4
HEADS_PER_GROUP = N_HEADS // N_KV_GROUPS
N_BRANCH = 3
CMP_BLOCK = 32
CMP_STRIDE = 16
SEL_BLOCK = 64
N_SEL = 16
WINDOW = 512
EPS = 1e-6
NEG = -1e30
BIG = 1e30

N_CMP = (SEQ - CMP_BLOCK) // CMP_STRIDE + 1
N_CMP_PAD = 128
N_SELB = SEQ // SEL_BLOCK
N_TOP = min(N_SEL, N_SELB)
SEL_SHIFT = SEL_BLOCK.bit_length() - 1
assert 1 << SEL_SHIFT == SEL_BLOCK

F32 = jnp.float32
BF16 = jnp.bfloat16

VMEM_LIMIT_BYTES = 56 * 1024 * 1024

TM_PROJ = 1024
SCONV_CHUNK_ENDS = (768, 1024)
TM_QKV = 1024
TM_MLP = 1024
TF_MLP = 1024
TQ = 256
TK = 256
N_Q_TILES = SEQ // TQ
HALO = 8
GATE_ROWS = 16
V_ROWS = HEAD_DIM + 16
K_AUG = 128
N_ROW_STREAMS = 4

_NT = (((1,), (1,)), ((), ()))

Q_SCALE = HEAD_DIM ** -0.5 * float(np.log2(np.e))


def _cparams(sem):
    return pltpu.CompilerParams(dimension_semantics=sem, vmem_limit_bytes=VMEM_LIMIT_BYTES)


def _norm_mod(x, gain, shift, scale):
    ms = jnp.mean(x * x, axis=-1, keepdims=True)
    y = x * lax.rsqrt(ms + EPS) * gain
    return y * (1.0 + scale) + shift


def _ada_kernel(c_ref, w_ref, b_ref, o_ref):
    c = c_ref[...]
    ca = c * jax.nn.sigmoid(c)
    o_ref[...] = jnp.dot(ca.astype(BF16), w_ref[...].astype(BF16),
                         preferred_element_type=F32) + b_ref[...]


def _ada(c, w, b, layer):
    bsz, d = c.shape
    n = w.shape[2]
    tn = 1024
    return pl.pallas_call(
        _ada_kernel,
        grid=(n // tn,),
        in_specs=[pl.BlockSpec((bsz, d), lambda j: (0, 0)),
                  pl.BlockSpec((None, d, tn), lambda j: (layer, 0, j)),
                  pl.BlockSpec((None, 1, tn), lambda j: (layer, 0, j))],
        out_specs=pl.BlockSpec((bsz, tn), lambda j: (0, j)),
        out_shape=jax.ShapeDtypeStruct((bsz, n), F32),
        compiler_params=_cparams(("arbitrary",)),
        name="ada",
    )(c, w, b.reshape(b.shape[0], 1, n))


def _sconv_kernel(x_ref, gain_ref, sh_ref, sc_ref, wb_ref, wc_ref, wu_ref, cw_ref, wo_ref, g_ref,
                  o_ref, vs_ref, *, per_b):
    tm, d = x_ref.shape
    x = x_ref[...]
    h = _norm_mod(x, gain_ref[...], sh_ref[...], sc_ref[...]).astype(BF16)

    @pl.when(pl.program_id(0) % per_b == 0)
    def _():
        vs_ref[0:HALO, :] = jnp.zeros((HALO, d), F32)

    cw = cw_ref[...]
    bounds = [0] + list(SCONV_CHUNK_ENDS)
    n_chunk = len(SCONV_CHUNK_ENDS)
    proj = lambda w_ref, cols: jnp.dot(h, w_ref[:, cols], preferred_element_type=F32)
    trip = lambda c: tuple(proj(w, slice(bounds[c], bounds[c + 1])) for w in (wb_ref, wc_ref, wu_ref))
    mix = None
    nxt = trip(0)
    for c in range(n_chunk):
        gb, gc, u = nxt
        if c + 1 < n_chunk:
            nxt = trip(c + 1)
        cols = slice(bounds[c], bounds[c + 1])
        vs_ref[HALO:HALO + tm, cols] = gc * u
        conv = (cw[2:3, cols] * vs_ref[HALO:HALO + tm, cols]
                + cw[1:2, cols] * vs_ref[HALO - 1:HALO - 1 + tm, cols]
                + cw[0:1, cols] * vs_ref[HALO - 2:HALO - 2 + tm, cols])
        part = jnp.dot((gb * conv).astype(BF16), wo_ref[cols, :], preferred_element_type=F32)
        mix = part if mix is None else mix + part
    o_ref[...] = x + g_ref[...] * mix
    vs_ref[0:HALO, :] = vs_ref[tm:tm + HALO, :]


def _short_conv(x2d, gain, sh, sc, w_in, conv_w, w_out, gate):
    m, d = x2d.shape
    tm = TM_PROJ
    per_b = SEQ // tm
    rows = lambda: pl.BlockSpec((tm, d), lambda i: (i, 0))
    vec = lambda: pl.BlockSpec((None, 1, d), lambda i: (i // per_b, 0, 0))
    wcol = lambda k: pl.BlockSpec((d, d), lambda i: (0, k), pipeline_mode=pl.Buffered(1))
    return pl.pallas_call(
        functools.partial(_sconv_kernel, per_b=per_b),
        grid=(m // tm,),
        in_specs=[rows(), pl.BlockSpec((1, d), lambda i: (0, 0)), vec(), vec(),
                  wcol(0), wcol(1), wcol(2),
                  pl.BlockSpec((CONV_WIDTH, d), lambda i: (0, 0)),
                  pl.BlockSpec((d, d), lambda i: (0, 0), pipeline_mode=pl.Buffered(1)),
                  vec()],
        out_specs=rows(),
        out_shape=jax.ShapeDtypeStruct((m, d), F32),
        scratch_shapes=[pltpu.VMEM((tm + HALO, d), F32)],
        compiler_params=_cparams(("arbitrary",)),
        name="short_conv",
    )(x2d, gain, sh, sc, w_in, w_in, w_in, conv_w, w_out, gate)


def _mlp_kernel(*refs, with_mixer_out):
    if with_mixer_out:
        a_ref, wo_ref, g1_ref, x_ref, gain_ref, sh_ref, sc_ref, g_ref, w1_ref, w2_ref, o_ref = refs
        x = x_ref[...] + g1_ref[...] * jnp.dot(a_ref[...], wo_ref[...], preferred_element_type=F32)
    else:
        x_ref, gain_ref, sh_ref, sc_ref, g_ref, w1_ref, w2_ref, o_ref = refs
        x = x_ref[...]
    h = _norm_mod(x, gain_ref[...], sh_ref[...], sc_ref[...]).astype(BF16)
    n_chunk = w1_ref.shape[1] // TF_MLP
    up = lambda k: jnp.dot(h, w1_ref[:, k * TF_MLP:(k + 1) * TF_MLP], preferred_element_type=F32)
    acc = None
    nxt = up(0)
    for k in range(n_chunk):
        cur = nxt
        if k + 1 < n_chunk:
            nxt = up(k + 1)
        h1 = jnp.square(jnp.maximum(cur, 0.0)).astype(BF16)
        part = jnp.dot(h1, w2_ref[k * TF_MLP:(k + 1) * TF_MLP, :], preferred_element_type=F32)
        acc = part if acc is None else acc + part
    o_ref[...] = x + g_ref[...] * acc


def _mlp(x2d, gain, sh, sc, gate, w1, w2, mixer_out=None):
    m, d = x2d.shape
    ff = w1.shape[1]
    tm = TM_MLP
    per_b = SEQ // tm
    rows = lambda: pl.BlockSpec((tm, d), lambda i: (i, 0))
    vec = lambda: pl.BlockSpec((None, 1, d), lambda i: (i // per_b, 0, 0))
    resident = lambda shape: pl.BlockSpec(shape, lambda i: (0, 0), pipeline_mode=pl.Buffered(1))
    args = [x2d, gain, sh, sc, gate, w1, w2]
    in_specs = [rows(), pl.BlockSpec((1, d), lambda i: (0, 0)), vec(), vec(), vec(),
                resident((d, ff)), resident((ff, d))]
    if mixer_out is not None:
        a2d, w_out, gate1 = mixer_out
        args = [a2d, w_out, gate1] + args
        in_specs = [rows(), resident((d, d)), vec()] + in_specs
    return pl.pallas_call(
        functools.partial(_mlp_kernel, with_mixer_out=mixer_out is not None),
        grid=(m // tm,),
        in_specs=in_specs,
        out_specs=rows(),
        out_shape=jax.ShapeDtypeStruct((m, d), F32),
        compiler_params=_cparams(("parallel",)),
        name="mlp",
    )(*args)


def _qkv_kernel(x_ref, gq_ref, shq_ref, scq_ref, gkv_ref, shkv_ref, sckv_ref,
                wq_ref, wg_ref, wkk_ref, wvt_ref, qg_ref, q_ref, gt_ref, kv_ref, vt_ref):
    x = x_ref[...]
    tm = x.shape[0]
    ms = jnp.mean(x * x, axis=-1, keepdims=True)
    y = x * lax.rsqrt(ms + EPS)
    hq = ((y * gq_ref[...]) * (1.0 + scq_ref[...]) + shq_ref[...]).astype(BF16)
    hkv = ((y * gkv_ref[...]) * (1.0 + sckv_ref[...]) + shkv_ref[...]).astype(BF16)
    q_t = lax.dot_general(wq_ref[...], hq, _NT, preferred_element_type=F32)
    qg = jnp.concatenate([qg_ref[...]] * (tm // 128), axis=1)
    for h in range(N_HEADS):
        xh = q_t[h * HEAD_DIM:(h + 1) * HEAD_DIM, :]
        msh = jnp.mean(xh * xh, axis=0, keepdims=True)
        q_ref[h * HEAD_DIM:(h + 1) * HEAD_DIM, :] = (xh * lax.rsqrt(msh + EPS) * qg * Q_SCALE).astype(BF16)
    gates_t = jax.nn.sigmoid(lax.dot_general(wg_ref[...], hq, _NT, preferred_element_type=F32))
    for g in range(N_KV_GROUPS):
        gt_ref[g] = gates_t[g * GATE_ROWS:(g + 1) * GATE_ROWS, :]
    kk = jnp.dot(hkv, wkk_ref[...], preferred_element_type=F32)
    for r in range(N_ROW_STREAMS):
        for g in range(N_KV_GROUPS):
            c0 = (r * N_KV_GROUPS + g) * HEAD_DIM
            kv_ref[r, g] = kk[:, c0:c0 + HEAD_DIM].astype(BF16)
    v_t = lax.dot_general(wvt_ref[...], hkv, _NT, preferred_element_type=F32)
    ones = jnp.ones((V_ROWS - HEAD_DIM, TK), BF16)
    for s in range(2):
        for g in range(N_KV_GROUPS):
            r0 = (s * N_KV_GROUPS + g) * HEAD_DIM
            for t in range(tm // TK):
                vt_ref[s, g, t, 0:HEAD_DIM, :] = v_t[r0:r0 + HEAD_DIM, t * TK:(t + 1) * TK].astype(BF16)
                vt_ref[s, g, t, HEAD_DIM:V_ROWS, :] = ones


def _qkv_proj(x3d, gq, shq, scq, gkv, shkv, sckv, wq, wg, wkk, wvt, qg_b):
    bsz, s, d = x3d.shape
    tm = TM_QKV
    vec = lambda: pl.BlockSpec((None, 1, d), lambda b, i: (b, 0, 0))
    const = lambda shape: pl.BlockSpec(shape, lambda b, i: (0,) * len(shape), pipeline_mode=pl.Buffered(1))
    return pl.pallas_call(
        _qkv_kernel,
        grid=(bsz, s // tm),
        in_specs=[pl.BlockSpec((None, tm, d), lambda b, i: (b, i, 0)),
                  const((1, d)), vec(), vec(),
                  const((1, d)), vec(), vec(),
                  const((d, d)), const(wg.shape), const(wkk.shape), const(wvt.shape), const(qg_b.shape)],
        out_specs=[pl.BlockSpec((None, d, tm), lambda b, i: (b, 0, i)),
                   pl.BlockSpec((None, N_KV_GROUPS, GATE_ROWS, tm), lambda b, i: (b, 0, 0, i)),
                   pl.BlockSpec((N_ROW_STREAMS, None, N_KV_GROUPS, tm, HEAD_DIM),
                                lambda b, i: (0, b, 0, i, 0)),
                   pl.BlockSpec((2, None, N_KV_GROUPS, tm // TK, V_ROWS, TK),
                                lambda b, i: (0, b, 0, i, 0, 0))],
        out_shape=[jax.ShapeDtypeStruct((bsz, d, s), BF16),
                   jax.ShapeDtypeStruct((bsz, N_KV_GROUPS, GATE_ROWS, s), F32),
                   jax.ShapeDtypeStruct((N_ROW_STREAMS, bsz, N_KV_GROUPS, s, HEAD_DIM), BF16),
                   jax.ShapeDtypeStruct((2, bsz, N_KV_GROUPS, s // TK, V_ROWS, TK), BF16)],
        compiler_params=_cparams(("parallel", "parallel")),
        name="qkv_proj",
    )(x3d, gq, shq, scq, gkv, shkv, sckv, wq, wg, wkk, wvt, qg_b)


def _head_rms(t, gain):
    ms = jnp.mean(t * t, axis=-1, keepdims=True)
    return t * lax.rsqrt(ms + EPS) * gain


def _kvprep_kernel(kcr_ref, vcr_ref, ks_ref, kw_ref,
                   pe_ref, w1_ref, w2_ref, kg_ref, eye_ref,
                   kc_ref, vct_ref, ksa_ref, kwn_ref, tok_ref):
    half = CMP_STRIDE * HEAD_DIM
    kg = kg_ref[...]
    eye_hd = eye_ref[0:HEAD_DIM, 0:HEAD_DIM]

    def compress(src_ref, idx):
        tok_ref[...] = src_ref[...].astype(F32)
        r = jnp.concatenate([tok_ref[pl.ds(l, N_CMP_PAD, stride=CMP_STRIDE), :] for l in range(CMP_STRIDE)],
                            axis=1).astype(BF16)
        z_lo = jnp.dot(r, w1_ref[idx, 0:half, :], preferred_element_type=F32)
        z_hi = jnp.dot(r, w1_ref[idx, half:2 * half, :], preferred_element_type=F32)
        z_hi = pltpu.roll(z_hi, N_CMP_PAD - 1, 0)
        pe_b = jnp.dot(pe_ref[idx], w1_ref[idx], preferred_element_type=F32)
        hid = jax.nn.gelu(z_lo + z_hi + pe_b[0:1, :]).astype(BF16)
        return jnp.dot(hid, w2_ref[idx], preferred_element_type=F32)

    kc_ref[...] = _head_rms(compress(kcr_ref, 0), kg[0:1, :]).astype(BF16)
    vc = compress(vcr_ref, 1).astype(BF16)
    vct_ref[...] = lax.dot_general(eye_hd, vc, _NT, preferred_element_type=F32).astype(BF16)

    ksn = _head_rms(ks_ref[...].astype(F32), kg[1:2, :]).astype(BF16)
    placed = jnp.dot(ksn, eye_ref[0:HEAD_DIM, 0:K_AUG], preferred_element_type=F32)
    row = lax.broadcasted_iota(jnp.int32, (SEQ, K_AUG), 0)
    col = lax.broadcasted_iota(jnp.int32, (SEQ, K_AUG), 1)
    onehot = (col - HEAD_DIM) == lax.shift_right_logical(row, SEL_SHIFT)
    ksa_ref[...] = jnp.where(onehot, 1.0, placed).astype(BF16)
    kwn_ref[...] = _head_rms(kw_ref[...].astype(F32), kg[2:3, :]).astype(BF16)


def _kv_prep(kv4, pe8, w1, w2, k_gain, eye):
    _, bsz, ng, s, hd = kv4.shape
    sel = lambda r: pl.BlockSpec((None, None, None, s, hd), lambda b, g: (r, b, g, 0, 0))
    const = lambda shape: pl.BlockSpec(shape, lambda b, g: (0,) * len(shape))
    out2 = lambda n, w: pl.BlockSpec((None, None, n, w), lambda b, g: (b, g, 0, 0))
    return pl.pallas_call(
        _kvprep_kernel,
        grid=(bsz, ng),
        in_specs=[sel(0), sel(1), sel(2), sel(3),
                  const(pe8.shape), const(w1.shape), const(w2.shape), const(k_gain.shape),
                  const(eye.shape)],
        out_specs=[out2(N_CMP_PAD, hd), out2(hd, N_CMP_PAD), out2(s, K_AUG), out2(s, hd)],
        out_shape=[jax.ShapeDtypeStruct((bsz, ng, N_CMP_PAD, hd), BF16),
                   jax.ShapeDtypeStruct((bsz, ng, hd, N_CMP_PAD), BF16),
                   jax.ShapeDtypeStruct((bsz, ng, s, K_AUG), BF16),
                   jax.ShapeDtypeStruct((bsz, ng, s, hd), BF16)],
        scratch_shapes=[pltpu.VMEM((s, hd), F32)],
        compiler_params=_cparams(("parallel", "parallel")),
        name="kv_prep",
    )(kv4, kv4, kv4, kv4, pe8, w1, w2, k_gain, eye)


def _nsa_kernel(*refs):
    for _ in _nsa_front(0, *refs):
        pass
    for i in range(N_Q_TILES):
        streams = [_nsa_sweep(i, *refs)]
        if i + 1 < N_Q_TILES:
            streams.append(_nsa_front(i + 1, *refs))
        while streams:
            for st in list(streams):
                try:
                    next(st)
                except StopIteration:
                    streams.remove(st)


def _nsa_front(i, q_ref, gt_ref, kc_ref, vct_ref, cmapt_ref, eye_ref, cbias_ref, tri_ref, wtri_ref,
               ksa_ref, vst_ref, kwn_ref, vwt_ref, o_ref,
               qa_ref, m_ref, a_ref, acc_ref, s_buf, p_buf, sw_ref, ow_ref):
    part = slice(i * TQ, (i + 1) * TQ)
    t0 = i * TQ
    hg = HEADS_PER_GROUP
    lanes = hg * TQ
    head = lambda k: slice(k * TQ, (k + 1) * TQ)
    qa = qa_ref.at[i % 2]
    ow = ow_ref.at[i % 2]

    for k in range(hg):
        qa[0:HEAD_DIM, head(k)] = q_ref[k * HEAD_DIM:(k + 1) * HEAD_DIM, part]
    qn = qa[0:HEAD_DIM, :]

    lane_t = t0 + (lax.broadcasted_iota(jnp.int32, (1, lanes), 1) & (TQ - 1))

    def k_rows(ref, c):
        return ref[c * TK:(c + 1) * TK, :]

    n_back = WINDOW // TK
    c_w = max(i - n_back, 0)
    n_win = i - c_w + 1
    sc = jnp.dot(kc_ref[...], qn, preferred_element_type=F32)
    w_bias = wtri_ref.at[min(i, n_back)]
    m_w = None
    for j in range(n_win):
        s_j = (jnp.dot(k_rows(kwn_ref, c_w + j), qn, preferred_element_type=F32)
               + jnp.concatenate([w_bias[j * TK:(j + 1) * TK, :]] * hg, axis=1))
        sw_ref[j] = s_j
        m_j = jnp.max(s_j, axis=0, keepdims=True)
        m_w = m_j if m_w is None else jnp.maximum(m_w, m_j)
    yield

    sc = sc + jnp.concatenate([cbias_ref[i]] * hg, axis=1)
    ec = jnp.exp2(sc - jnp.max(sc, axis=0, keepdims=True))
    any_valid = jnp.where(lane_t >= CMP_BLOCK - 1, 1.0, 0.0)
    pc = ec * (any_valid / jnp.maximum(jnp.sum(ec, axis=0, keepdims=True), 1e-30))
    ow[0:HEAD_DIM, :] = jnp.dot(vct_ref[...], pc.astype(BF16), preferred_element_type=F32)

    psum = pc[:, head(0)]
    for k in range(1, hg):
        psum = psum + pc[:, head(k)]
    imp = jnp.dot(cmapt_ref[...], psum, preferred_element_type=F32,
                  precision=lax.Precision.HIGHEST)
    yield

    acc_w = jnp.zeros((V_ROWS, lanes), F32)
    for j in range(n_win):
        acc_w = acc_w + jnp.dot(vwt_ref[c_w + j], jnp.exp2(sw_ref[j] - m_w).astype(BF16),
                                preferred_element_type=F32)
    yield
    ow[HEAD_DIM:2 * HEAD_DIM, :] = acc_w[0:HEAD_DIM, :] / jnp.maximum(acc_w[HEAD_DIM:HEAD_DIM + 1, :], 1e-30)

    tq = t0 + lax.broadcasted_iota(jnp.int32, (N_SELB, TQ), 1)
    j_id = lax.broadcasted_iota(jnp.int32, (N_SELB, TQ), 0)
    cur = lax.shift_right_logical(tq, SEL_SHIFT)
    forced = (j_id == 0) | (j_id == cur) | (j_id == cur - 1)
    causal = j_id <= cur
    score = jnp.where(forced, BIG, jnp.where(causal, imp, NEG))
    rank = jnp.zeros((N_SELB, TQ), jnp.int32)
    for jp in range(N_SELB):
        other = score[jp:jp + 1, :]
        beats = (other > score) | ((other == score) & (j_id > jp))
        rank = rank + beats.astype(jnp.int32)
    chosen = (rank < N_TOP) & causal
    sel_bias = jnp.where(chosen, 0.0, NEG).astype(BF16)
    for k in range(hg):
        qa[HEAD_DIM:HEAD_DIM + N_SELB, head(k)] = sel_bias
    qa[HEAD_DIM + N_SELB:K_AUG, :] = jnp.zeros((K_AUG - HEAD_DIM - N_SELB, lanes), BF16)


def _nsa_sweep(i, q_ref, gt_ref, kc_ref, vct_ref, cmapt_ref, eye_ref, cbias_ref, tri_ref, wtri_ref,
               ksa_ref, vst_ref, kwn_ref, vwt_ref, o_ref,
               qa_ref, m_ref, a_ref, acc_ref, s_buf, p_buf, sw_ref, ow_ref):
    part = slice(i * TQ, (i + 1) * TQ)
    parity = i % 2
    hg = HEADS_PER_GROUP
    lanes = hg * TQ
    head = lambda k: slice(k * TQ, (k + 1) * TQ)
    qa = qa_ref.at[i % 2]
    ow = ow_ref.at[i % 2]

    def sel_scores(c):
        return jnp.dot(ksa_ref[c * TK:(c + 1) * TK, :], qa[...], preferred_element_type=F32)

    def softmax_update(scores):
        m_old = m_ref[...]
        m_new = jnp.maximum(m_old, jnp.max(scores(), axis=0, keepdims=True))
        m_ref[...] = m_new
        return jnp.exp2(m_old - m_new), jnp.exp2(scores() - m_new).astype(BF16)

    m_ref[...] = jnp.full(m_ref.shape, NEG, F32)
    acc_ref[...] = jnp.zeros(acc_ref.shape, F32)
    s_buf[0] = sel_scores(0)
    yield

    def pv_prev(c):
        if c == 0:
            return acc_ref[...]
        pv = jnp.dot(vst_ref[c - 1], p_buf[(c - 1) % 2], preferred_element_type=F32)
        return a_ref[...] * acc_ref[...] + pv

    for c in range(i):
        cur = c % 2
        if c >= 1:
            acc_ref[...] = pv_prev(c)
        s_buf[1 - cur] = sel_scores(c + 1)
        yield
        alpha, p = softmax_update(lambda: s_buf[cur])
        p_buf[cur] = p
        a_ref[...] = alpha

    acc_prev = pv_prev(i)
    yield
    alpha, p = softmax_update(lambda: s_buf[parity] + jnp.concatenate([tri_ref[...]] * hg, axis=1))
    acc_sel = alpha * acc_prev + jnp.dot(vst_ref[i], p, preferred_element_type=F32)
    yield
    o_sel = acc_sel[0:HEAD_DIM, :] / jnp.maximum(acc_sel[HEAD_DIM:HEAD_DIM + 1, :], 1e-30)

    gt = gt_ref[:, part]
    parts = []
    for k in range(hg):
        parts.append(gt[3 * k:3 * k + 1, :] * ow[0:HEAD_DIM, head(k)]
                     + gt[3 * k + 1:3 * k + 2, :] * o_sel[:, head(k)]
                     + gt[3 * k + 2:3 * k + 3, :] * ow[HEAD_DIM:2 * HEAD_DIM, head(k)])
    o_t = jnp.concatenate(parts, axis=0).astype(BF16)
    o_ref[part, :] = lax.dot_general(eye_ref[...], o_t, _NT, preferred_element_type=F32).astype(BF16)


def _nsa(q_t, gates_t, kc, vct, cmapt, eye, cbias, tri, wtri, ksa, kwn, v_t):
    bsz, d, s = q_t.shape
    ng = N_KV_GROUPS
    gw = HEADS_PER_GROUP * HEAD_DIM
    lanes = HEADS_PER_GROUP * TQ
    nt = s // TK
    assert TQ == TK and WINDOW % TK == 0 and gw == TQ and s == SEQ
    per_bg = lambda n, w: pl.BlockSpec((None, None, n, w), lambda b, g: (b, g, 0, 0))
    vt = lambda k: pl.BlockSpec((None, None, None, nt, V_ROWS, TK), lambda b, g: (k, b, g, 0, 0, 0))
    const = lambda shape: pl.BlockSpec(shape, lambda b, g: (0,) * len(shape))
    return pl.pallas_call(
        _nsa_kernel,
        grid=(bsz, ng),
        in_specs=[pl.BlockSpec((None, gw, s), lambda b, g: (b, g, 0)),
                  per_bg(GATE_ROWS, s),
                  per_bg(N_CMP_PAD, HEAD_DIM), per_bg(HEAD_DIM, N_CMP_PAD),
                  const(cmapt.shape), const(eye.shape), const(cbias.shape), const(tri.shape),
                  const(wtri.shape),
                  per_bg(s, K_AUG), vt(0), per_bg(s, HEAD_DIM), vt(1)],
        out_specs=pl.BlockSpec((None, s, gw), lambda b, g: (b, 0, g)),
        out_shape=jax.ShapeDtypeStruct((bsz, s, d), BF16),
        scratch_shapes=[pltpu.VMEM((2, K_AUG, lanes), BF16),
                        pltpu.VMEM((1, lanes), F32),
                        pltpu.VMEM((1, lanes), F32),
                        pltpu.VMEM((V_ROWS, lanes), F32),
                        pltpu.VMEM((2, TK, lanes), F32),
                        pltpu.VMEM((2, TK, lanes), BF16),
                        pltpu.VMEM((WINDOW // TK + 1, TK, lanes), F32),
                        pltpu.VMEM((2, 2 * HEAD_DIM, lanes), F32)],
        compiler_params=_cparams(("parallel", "parallel")),
        name="nsa",
    )(q_t, gates_t, kc, vct, cmapt, eye, cbias, tri, wtri, ksa, v_t, kwn, v_t)


def _cmp_to_sel_t():
    c0 = np.arange(N_CMP_PAD)[None, :] * CMP_STRIDE
    s0 = np.arange(N_SELB)[:, None] * SEL_BLOCK
    ov = np.minimum(c0 + CMP_BLOCK, s0 + SEL_BLOCK) - np.maximum(c0, s0)
    m = (np.clip(ov, 0, None) / CMP_BLOCK).astype(np.float32)
    m[:, N_CMP:] = 0.0
    return m


def _cmp_bias():
    n_end = np.arange(N_CMP_PAD)[None, :, None] * CMP_STRIDE + (CMP_BLOCK - 1)
    t = np.arange(SEQ // TQ)[:, None, None] * TQ + np.arange(TQ)[None, None, :]
    return np.where(n_end <= t, 0.0, NEG).astype(np.float32)


def _tri_bias():
    r = np.arange(TK)[:, None]
    c = np.arange(TQ)[None, :]
    return np.where(r <= c, 0.0, NEG).astype(np.float32)


def _window_bias():
    n_back = WINDOW // TK
    r = np.arange(WINDOW + TQ)[:, None]
    c = np.arange(TQ)[None, :]
    tabs = [np.where(r <= i * TQ + c, 0.0, NEG) for i in range(n_back)]
    tabs.append(np.where((r > c) & (r <= c + WINDOW), 0.0, NEG))
    return np.stack(tabs).astype(np.float32)


def kernel(x, c, norm_gain, w_ada, b_ada, w_a_in, conv_w, w_a_out, w_qg, q_gain, w_o,
           kv_norm_gain, w_ada_kv, b_ada_kv, w_kv, k_gain, cmp_pe, cmp_w1, cmp_w2,
           w_mlp1, w_mlp2):
    bsz, s, d = x.shape
    assert (s, d) == (SEQ, D_MODEL)
    m = bsz * s
    x2d = x.reshape(m, d)

    def split_mod(mod, n):
        return [mod[:, k * d:(k + 1) * d].reshape(bsz, 1, d) for k in range(n)]

    sh1, sc1, g1, sh2, sc2, g2 = split_mod(_ada(c, w_ada, b_ada, 0), 6)
    x2d = _short_conv(x2d, norm_gain[0, 0].reshape(1, d), sh1, sc1, w_a_in[0].astype(BF16),
                      conv_w[0], w_a_out[0].astype(BF16), g1)
    x2d = _mlp(x2d, norm_gain[0, 1].reshape(1, d), sh2, sc2, g2,
               w_mlp1[0].astype(BF16), w_mlp2[0].astype(BF16))

    sh1, sc1, g1, sh2, sc2, g2 = split_mod(_ada(c, w_ada, b_ada, 1), 6)
    sh_kv, sc_kv = split_mod(_ada(c, w_ada_kv[None], b_ada_kv[None], 0), 2)
    nq = N_HEADS * HEAD_DIM
    wq_t = w_qg[0][:, :nq].T.astype(BF16)
    per_g = HEADS_PER_GROUP * N_BRANCH
    wg = w_qg[0][:, nq:].reshape(d, N_KV_GROUPS, per_g)
    wg = jnp.pad(wg, ((0, 0), (0, 0), (0, GATE_ROWS - per_g))).reshape(d, N_KV_GROUPS * GATE_ROWS)
    qg_b = jnp.broadcast_to(q_gain[0].reshape(HEAD_DIM, 1), (HEAD_DIM, 128))
    gcols = N_KV_GROUPS * HEAD_DIM
    stream = lambda r: w_kv[:, r * gcols:(r + 1) * gcols]
    wkk = jnp.concatenate([stream(0), stream(1), stream(2), stream(4)], axis=1).astype(BF16)
    wvt = jnp.concatenate([stream(3), stream(5)], axis=1).T.astype(BF16)
    q_t, gates_t, kv4, v_t = _qkv_proj(x2d.reshape(bsz, s, d), norm_gain[1, 0].reshape(1, d), sh1, sc1,
                                       kv_norm_gain.reshape(1, d), sh_kv, sc_kv, wq_t,
                                       wg.T.astype(BF16), wkk, wvt, qg_b)
    eye = jnp.eye(TQ, dtype=BF16)
    pe8 = jnp.broadcast_to(cmp_pe.reshape(2, 1, CMP_BLOCK * HEAD_DIM), (2, 8, CMP_BLOCK * HEAD_DIM))
    kc, vct, ksa, kwn = _kv_prep(kv4, pe8.astype(BF16), cmp_w1.astype(BF16),
                                 cmp_w2.astype(BF16), k_gain, eye)
    attn = _nsa(q_t, gates_t, kc, vct, jnp.asarray(_cmp_to_sel_t()), eye,
                jnp.asarray(_cmp_bias()), jnp.asarray(_tri_bias()),
                jnp.asarray(_window_bias()), ksa, kwn, v_t)
    x2d = _mlp(x2d, norm_gain[1, 1].reshape(1, d), sh2, sc2, g2,
               w_mlp1[1].astype(BF16), w_mlp2[1].astype(BF16),
               mixer_out=(attn.reshape(m, d), w_o[0].astype(BF16), g1))
    return x2d.reshape(bsz, s, d)
```

```python
import functools

import numpy as np
import jax
import jax.numpy as jnp
from jax import lax
from jax.experimental import pallas as pl
from jax.experimental.pallas import tpu as pltpu

D_MODEL = 1024
SEQ = 2048
CONV_WIDTH = 3
HEAD_DIM = 64
N_HEADS = D_MODEL // HEAD_DIM
N_KV_GROUPS = 4
HEADS_PER_GROUP = N_HEADS // N_KV_GROUPS
N_BRANCH = 3
CMP_BLOCK = 32
CMP_STRIDE = 16
SEL_BLOCK = 64
N_SEL = 16
WINDOW = 512
EPS = 1e-6
NEG = -1e30
BIG = 1e30

N_CMP = (SEQ - CMP_BLOCK) // CMP_STRIDE + 1
N_CMP_PAD = 128
N_SELB = SEQ // SEL_BLOCK
N_TOP = min(N_SEL, N_SELB)
SEL_SHIFT = SEL_BLOCK.bit_length() - 1
assert 1 << SEL_SHIFT == SEL_BLOCK

F32 = jnp.float32
BF16 = jnp.bfloat16

VMEM_LIMIT_BYTES = 56 * 1024 * 1024

TM_PROJ = 1024
SCONV_CHUNK_ENDS = (768, 1024)
TM_QKV = 1024
TM_MLP = 1024
TF_MLP = 1024
TQ = 256
TK = 256
N_Q_TILES = SEQ // TQ
HALO = 8
GATE_ROWS = 16
V_ROWS = HEAD_DIM + 16
K_AUG = 128
N_ROW_STREAMS = 4

_NT = (((1,), (1,)), ((), ()))

Q_SCALE = HEAD_DIM ** -0.5 * float(np.log2(np.e))


def _cparams(sem):
    return pltpu.CompilerParams(dimension_semantics=sem, vmem_limit_bytes=VMEM_LIMIT_BYTES)


def _norm_mod(x, gain, shift, scale):
    ms = jnp.mean(x * x, axis=-1, keepdims=True)
    y = x * lax.rsqrt(ms + EPS) * gain
    return y * (1.0 + scale) + shift


def _ada_kernel(c_ref, w_ref, b_ref, o_ref):
    c = c_ref[...]
    ca = c * jax.nn.sigmoid(c)
    o_ref[...] = jnp.dot(ca.astype(BF16), w_ref[...].astype(BF16),
                         preferred_element_type=F32) + b_ref[...]


def _ada(c, w, b, layer):
    bsz, d = c.shape
    n = w.shape[2]
    tn = 1024
    return pl.pallas_call(
        _ada_kernel,
        grid=(n // tn,),
        in_specs=[pl.BlockSpec((bsz, d), lambda j: (0, 0)),
                  pl.BlockSpec((None, d, tn), lambda j: (layer, 0, j)),
                  pl.BlockSpec((None, 1, tn), lambda j: (layer, 0, j))],
        out_specs=pl.BlockSpec((bsz, tn), lambda j: (0, j)),
        out_shape=jax.ShapeDtypeStruct((bsz, n), F32),
        compiler_params=_cparams(("arbitrary",)),
        name="ada",
    )(c, w, b.reshape(b.shape[0], 1, n))


def _sconv_kernel(x_ref, gain_ref, sh_ref, sc_ref, wb_ref, wc_ref, wu_ref, cw_ref, wo_ref, g_ref,
                  o_ref, vs_ref, *, per_b):
    tm, d = x_ref.shape
    x = x_ref[...]
    h = _norm_mod(x, gain_ref[...], sh_ref[...], sc_ref[...]).astype(BF16)

    @pl.when(pl.program_id(0) % per_b == 0)
    def _():
        vs_ref[0:HALO, :] = jnp.zeros((HALO, d), F32)

    cw = cw_ref[...]
    bounds = [0] + list(SCONV_CHUNK_ENDS)
    n_chunk = len(SCONV_CHUNK_ENDS)
    proj = lambda w_ref, cols: jnp.dot(h, w_ref[:, cols], preferred_element_type=F32)
    trip = lambda c: tuple(proj(w, slice(bounds[c], bounds[c + 1])) for w in (wb_ref, wc_ref, wu_ref))
    mix = None
    nxt = trip(0)
    for c in range(n_chunk):
        gb, gc, u = nxt
        if c + 1 < n_chunk:
            nxt = trip(c + 1)
        cols = slice(bounds[c], bounds[c + 1])
        vs_ref[HALO:HALO + tm, cols] = gc * u
        conv = (cw[2:3, cols] * vs_ref[HALO:HALO + tm, cols]
                + cw[1:2, cols] * vs_ref[HALO - 1:HALO - 1 + tm, cols]
                + cw[0:1, cols] * vs_ref[HALO - 2:HALO - 2 + tm, cols])
        part = jnp.dot((gb * conv).astype(BF16), wo_ref[cols, :], preferred_element_type=F32)
        mix = part if mix is None else mix + part
    o_ref[...] = x + g_ref[...] * mix
    vs_ref[0:HALO, :] = vs_ref[tm:tm + HALO, :]


def _short_conv(x2d, gain, sh, sc, w_in, conv_w, w_out, gate):
    m, d = x2d.shape
    tm = TM_PROJ
    per_b = SEQ // tm
    rows = lambda: pl.BlockSpec((tm, d), lambda i: (i, 0))
    vec = lambda: pl.BlockSpec((None, 1, d), lambda i: (i // per_b, 0, 0))
    wcol = lambda k: pl.BlockSpec((d, d), lambda i: (0, k), pipeline_mode=pl.Buffered(1))
    return pl.pallas_call(
        functools.partial(_sconv_kernel, per_b=per_b),
        grid=(m // tm,),
        in_specs=[rows(), pl.BlockSpec((1, d), lambda i: (0, 0)), vec(), vec(),
                  wcol(0), wcol(1), wcol(2),
                  pl.BlockSpec((CONV_WIDTH, d), lambda i: (0, 0)),
                  pl.BlockSpec((d, d), lambda i: (0, 0), pipeline_mode=pl.Buffered(1)),
                  vec()],
        out_specs=rows(),
        out_shape=jax.ShapeDtypeStruct((m, d), F32),
        scratch_shapes=[pltpu.VMEM((tm + HALO, d), F32)],
        compiler_params=_cparams(("arbitrary",)),
        name="short_conv",
    )(x2d, gain, sh, sc, w_in, w_in, w_in, conv_w, w_out, gate)


def _mlp_kernel(*refs, with_mixer_out):
    if with_mixer_out:
        a_ref, wo_ref, g1_ref, x_ref, gain_ref, sh_ref, sc_ref, g_ref, w1_ref, w2_ref, o_ref = refs
        x = x_ref[...] + g1_ref[...] * jnp.dot(a_ref[...], wo_ref[...], preferred_element_type=F32)
    else:
        x_ref, gain_ref, sh_ref, sc_ref, g_ref, w1_ref, w2_ref, o_ref = refs
        x = x_ref[...]
    h = _norm_mod(x, gain_ref[...], sh_ref[...], sc_ref[...]).astype(BF16)
    n_chunk = w1_ref.shape[1] // TF_MLP
    up = lambda k: jnp.dot(h, w1_ref[:, k * TF_MLP:(k + 1) * TF_MLP], preferred_element_type=F32)
    acc = None
    nxt = up(0)
    for k in range(n_chunk):
        cur = nxt
        if k + 1 < n_chunk:
            nxt = up(k + 1)
        h1 = jnp.square(jnp.maximum(cur, 0.0)).astype(BF16)
        part = jnp.dot(h1, w2_ref[k * TF_MLP:(k + 1) * TF_MLP, :], preferred_element_type=F32)
        acc = part if acc is None else acc + part
    o_ref[...] = x + g_ref[...] * acc


def _mlp(x2d, gain, sh, sc, gate, w1, w2, mixer_out=None):
    m, d = x2d.shape
    ff = w1.shape[1]
    tm = TM_MLP
    per_b = SEQ // tm
    rows = lambda: pl.BlockSpec((tm, d), lambda i: (i, 0))
    vec = lambda: pl.BlockSpec((None, 1, d), lambda i: (i // per_b, 0, 0))
    resident = lambda shape: pl.BlockSpec(shape, lambda i: (0, 0), pipeline_mode=pl.Buffered(1))
    args = [x2d, gain, sh, sc, gate, w1, w2]
    in_specs = [rows(), pl.BlockSpec((1, d), lambda i: (0, 0)), vec(), vec(), vec(),
                resident((d, ff)), resident((ff, d))]
    if mixer_out is not None:
        a2d, w_out, gate1 = mixer_out
        args = [a2d, w_out, gate1] + args
        in_specs = [rows(), resident((d, d)), vec()] + in_specs
    return pl.pallas_call(
        functools.partial(_mlp_kernel, with_mixer_out=mixer_out is not None),
        grid=(m // tm,),
        in_specs=in_specs,
        out_specs=rows(),
        out_shape=jax.ShapeDtypeStruct((m, d), F32),
        compiler_params=_cparams(("parallel",)),
        name="mlp",
    )(*args)


def _qkv_kernel(x_ref, gq_ref, shq_ref, scq_ref, gkv_ref, shkv_ref, sckv_ref,
                wq_ref, wg_ref, wkk_ref, wvt_ref, qg_ref, q_ref, gt_ref, kv_ref, vt_ref):
    x = x_ref[...]
    tm = x.shape[0]
    ms = jnp.mean(x * x, axis=-1, keepdims=True)
    y = x * lax.rsqrt(ms + EPS)
    hq = ((y * gq_ref[...]) * (1.0 + scq_ref[...]) + shq_ref[...]).astype(BF16)
    hkv = ((y * gkv_ref[...]) * (1.0 + sckv_ref[...]) + shkv_ref[...]).astype(BF16)
    q_t = lax.dot_general(wq_ref[...], hq, _NT, preferred_element_type=F32)
    qg = jnp.concatenate([qg_ref[...]] * (tm // 128), axis=1)
    for h in range(N_HEADS):
        xh = q_t[h * HEAD_DIM:(h + 1) * HEAD_DIM, :]
        msh = jnp.mean(xh * xh, axis=0, keepdims=True)
        q_ref[h * HEAD_DIM:(h + 1) * HEAD_DIM, :] = (xh * lax.rsqrt(msh + EPS) * qg * Q_SCALE).astype(BF16)
    gates_t = jax.nn.sigmoid(lax.dot_general(wg_ref[...], hq, _NT, preferred_element_type=F32))
    for g in range(N_KV_GROUPS):
        gt_ref[g] = gates_t[g * GATE_ROWS:(g + 1) * GATE_ROWS, :]
    kk = jnp.dot(hkv, wkk_ref[...], preferred_element_type=F32)
    for r in range(N_ROW_STREAMS):
        for g in range(N_KV_GROUPS):
            c0 = (r * N_KV_GROUPS + g) * HEAD_DIM
            kv_ref[r, g] = kk[:, c0:c0 + HEAD_DIM].astype(BF16)
    v_t = lax.dot_general(wvt_ref[...], hkv, _NT, preferred_element_type=F32)
    ones = jnp.ones((V_ROWS - HEAD_DIM, TK), BF16)
    for s in range(2):
        for g in range(N_KV_GROUPS):
            r0 = (s * N_KV_GROUPS + g) * HEAD_DIM
            for t in range(tm // TK):
                vt_ref[s, g, t, 0:HEAD_DIM, :] = v_t[r0:r0 + HEAD_DIM, t * TK:(t + 1) * TK].astype(BF16)
                vt_ref[s, g, t, HEAD_DIM:V_ROWS, :] = ones


def _qkv_proj(x3d, gq, shq, scq, gkv, shkv, sckv, wq, wg, wkk, wvt, qg_b):
    bsz, s, d = x3d.shape
    tm = TM_QKV
    vec = lambda: pl.BlockSpec((None, 1, d), lambda b, i: (b, 0, 0))
    const = lambda shape: pl.BlockSpec(shape, lambda b, i: (0,) * len(shape), pipeline_mode=pl.Buffered(1))
    return pl.pallas_call(
        _qkv_kernel,
        grid=(bsz, s // tm),
        in_specs=[pl.BlockSpec((None, tm, d), lambda b, i: (b, i, 0)),
                  const((1, d)), vec(), vec(),
                  const((1, d)), vec(), vec(),
                  const((d, d)), const(wg.shape), const(wkk.shape), const(wvt.shape), const(qg_b.shape)],
        out_specs=[pl.BlockSpec((None, d, tm), lambda b, i: (b, 0, i)),
                   pl.BlockSpec((None, N_KV_GROUPS, GATE_ROWS, tm), lambda b, i: (b, 0, 0, i)),
                   pl.BlockSpec((N_ROW_STREAMS, None, N_KV_GROUPS, tm, HEAD_DIM),
                                lambda b, i: (0, b, 0, i, 0)),
                   pl.BlockSpec((2, None, N_KV_GROUPS, tm // TK, V_ROWS, TK),
                                lambda b, i: (0, b, 0, i, 0, 0))],
        out_shape=[jax.ShapeDtypeStruct((bsz, d, s), BF16),
                   jax.ShapeDtypeStruct((bsz, N_KV_GROUPS, GATE_ROWS, s), F32),
                   jax.ShapeDtypeStruct((N_ROW_STREAMS, bsz, N_KV_GROUPS, s, HEAD_DIM), BF16),
                   jax.ShapeDtypeStruct((2, bsz, N_KV_GROUPS, s // TK, V_ROWS, TK), BF16)],
        compiler_params=_cparams(("parallel", "parallel")),
        name="qkv_proj",
    )(x3d, gq, shq, scq, gkv, shkv, sckv, wq, wg, wkk, wvt, qg_b)


def _head_rms(t, gain):
    ms = jnp.mean(t * t, axis=-1, keepdims=True)
    return t * lax.rsqrt(ms + EPS) * gain


def _kvprep_kernel(kcr_ref, vcr_ref, ks_ref, kw_ref,
                   pe_ref, w1_ref, w2_ref, kg_ref, eye_ref,
                   kc_ref, vct_ref, ksa_ref, kwn_ref, tok_ref):
    half = CMP_STRIDE * HEAD_DIM
    kg = kg_ref[...]
    eye_hd = eye_ref[0:HEAD_DIM, 0:HEAD_DIM]

    def compress(src_ref, idx):
        tok_ref[...] = src_ref[...].astype(F32)
        r = jnp.concatenate([tok_ref[pl.ds(l, N_CMP_PAD, stride=CMP_STRIDE), :] for l in range(CMP_STRIDE)],
                            axis=1).astype(BF16)
        z_lo = jnp.dot(r, w1_ref[idx, 0:half, :], preferred_element_type=F32)
        z_hi = jnp.dot(r, w1_ref[idx, half:2 * half, :], preferred_element_type=F32)
        z_hi = pltpu.roll(z_hi, N_CMP_PAD - 1, 0)
        pe_b = jnp.dot(pe_ref[idx], w1_ref[idx], preferred_element_type=F32)
        hid = jax.nn.gelu(z_lo + z_hi + pe_b[0:1, :]).astype(BF16)
        return jnp.dot(hid, w2_ref[idx], preferred_element_type=F32)

    kc_ref[...] = _head_rms(compress(kcr_ref, 0), kg[0:1, :]).astype(BF16)
    vc = compress(vcr_ref, 1).astype(BF16)
    vct_ref[...] = lax.dot_general(eye_hd, vc, _NT, preferred_element_type=F32).astype(BF16)

    ksn = _head_rms(ks_ref[...].astype(F32), kg[1:2, :]).astype(BF16)
    placed = jnp.dot(ksn, eye_ref[0:HEAD_DIM, 0:K_AUG], preferred_element_type=F32)
    row = lax.broadcasted_iota(jnp.int32, (SEQ, K_AUG), 0)
    col = lax.broadcasted_iota(jnp.int32, (SEQ, K_AUG), 1)
    onehot = (col - HEAD_DIM) == lax.shift_right_logical(row, SEL_SHIFT)
    ksa_ref[...] = jnp.where(onehot, 1.0, placed).astype(BF16)
    kwn_ref[...] = _head_rms(kw_ref[...].astype(F32), kg[2:3, :]).astype(BF16)


def _kv_prep(kv4, pe8, w1, w2, k_gain, eye):
    _, bsz, ng, s, hd = kv4.shape
    sel = lambda r: pl.BlockSpec((None, None, None, s, hd), lambda b, g: (r, b, g, 0, 0))
    const = lambda shape: pl.BlockSpec(shape, lambda b, g: (0,) * len(shape))
    out2 = lambda n, w: pl.BlockSpec((None, None, n, w), lambda b, g: (b, g, 0, 0))
    return pl.pallas_call(
        _kvprep_kernel,
        grid=(bsz, ng),
        in_specs=[sel(0), sel(1), sel(2), sel(3),
                  const(pe8.shape), const(w1.shape), const(w2.shape), const(k_gain.shape),
                  const(eye.shape)],
        out_specs=[out2(N_CMP_PAD, hd), out2(hd, N_CMP_PAD), out2(s, K_AUG), out2(s, hd)],
        out_shape=[jax.ShapeDtypeStruct((bsz, ng, N_CMP_PAD, hd), BF16),
                   jax.ShapeDtypeStruct((bsz, ng, hd, N_CMP_PAD), BF16),
                   jax.ShapeDtypeStruct((bsz, ng, s, K_AUG), BF16),
                   jax.ShapeDtypeStruct((bsz, ng, s, hd), BF16)],
        scratch_shapes=[pltpu.VMEM((s, hd), F32)],
        compiler_params=_cparams(("parallel", "parallel")),
        name="kv_prep",
    )(kv4, kv4, kv4, kv4, pe8, w1, w2, k_gain, eye)


def _nsa_kernel(*refs):
    for _ in _nsa_front(0, *refs):
        pass
    for i in range(N_Q_TILES):
        streams = [_nsa_sweep(i, *refs)]
        if i + 1 < N_Q_TILES:
            streams.insert(0, _nsa_front(i + 1, *refs))
        while streams:
            for st in list(streams):
                try:
                    next(st)
                except StopIteration:
                    streams.remove(st)


def _nsa_front(i, q_ref, gt_ref, kc_ref, vct_ref, cmapt_ref, eye_ref, cbias_ref, tri_ref, wtri_ref,
               ksa_ref, vst_ref, kwn_ref, vwt_ref, o_ref,
               qa_ref, m_ref, a_ref, acc_ref, s_buf, p_buf, sw_ref, ow_ref):
    part = slice(i * TQ, (i + 1) * TQ)
    t0 = i * TQ
    hg = HEADS_PER_GROUP
    lanes = hg * TQ
    head = lambda k: slice(k * TQ, (k + 1) * TQ)
    qa = qa_ref.at[i % 2]
    ow = ow_ref.at[i % 2]

    for k in range(hg):
        qa[0:HEAD_DIM, head(k)] = q_ref[k * HEAD_DIM:(k + 1) * HEAD_DIM, part]
    qn = qa[0:HEAD_DIM, :]

    lane_t = t0 + (lax.broadcasted_iota(jnp.int32, (1, lanes), 1) & (TQ - 1))

    def k_rows(ref, c):
        return ref[c * TK:(c + 1) * TK, :]

    n_back = WINDOW // TK
    c_w = max(i - n_back, 0)
    n_win = i - c_w + 1
    sc = jnp.dot(kc_ref[...], qn, preferred_element_type=F32)
    w_bias = wtri_ref.at[min(i, n_back)]
    m_w = None
    for j in range(n_win):
        s_j = (jnp.dot(k_rows(kwn_ref, c_w + j), qn, preferred_element_type=F32)
               + jnp.concatenate([w_bias[j * TK:(j + 1) * TK, :]] * hg, axis=1))
        sw_ref[j] = s_j
        m_j = jnp.max(s_j, axis=0, keepdims=True)
        m_w = m_j if m_w is None else jnp.maximum(m_w, m_j)
        yield

    sc = sc + jnp.concatenate([cbias_ref[i]] * hg, axis=1)
    ec = jnp.exp2(sc - jnp.max(sc, axis=0, keepdims=True))
    any_valid = jnp.where(lane_t >= CMP_BLOCK - 1, 1.0, 0.0)
    pc = ec * (any_valid / jnp.maximum(jnp.sum(ec, axis=0, keepdims=True), 1e-30))
    ow[0:HEAD_DIM, :] = jnp.dot(vct_ref[...], pc.astype(BF16), preferred_element_type=F32)

    psum = pc[:, head(0)]
    for k in range(1, hg):
        psum = psum + pc[:, head(k)]
    imp = jnp.dot(cmapt_ref[...], psum, preferred_element_type=F32,
                  precision=lax.Precision.HIGHEST)
    yield

    acc_w = jnp.zeros((V_ROWS, lanes), F32)
    for j in range(n_win):
        acc_w = acc_w + jnp.dot(vwt_ref[c_w + j], jnp.exp2(sw_ref[j] - m_w).astype(BF16),
                                preferred_element_type=F32)
        yield
    ow[HEAD_DIM:2 * HEAD_DIM, :] = acc_w[0:HEAD_DIM, :] / jnp.maximum(acc_w[HEAD_DIM:HEAD_DIM + 1, :], 1e-30)

    tq = t0 + lax.broadcasted_iota(jnp.int32, (N_SELB, TQ), 1)
    j_id = lax.broadcasted_iota(jnp.int32, (N_SELB, TQ), 0)
    cur = lax.shift_right_logical(tq, SEL_SHIFT)
    forced = (j_id == 0) | (j_id == cur) | (j_id == cur - 1)
    causal = j_id <= cur
    score = jnp.where(forced, BIG, jnp.where(causal, imp, NEG))
    rank = jnp.zeros((N_SELB, TQ), jnp.int32)
    for jp in range(N_SELB):
        other = score[jp:jp + 1, :]
        beats = (other > score) | ((other == score) & (j_id > jp))
        rank = rank + beats.astype(jnp.int32)
    chosen = (rank < N_TOP) & causal
    sel_bias = jnp.where(chosen, 0.0, NEG).astype(BF16)
    for k in range(hg):
        qa[HEAD_DIM:HEAD_DIM + N_SELB, head(k)] = sel_bias
    qa[HEAD_DIM + N_SELB:K_AUG, :] = jnp.zeros((K_AUG - HEAD_DIM - N_SELB, lanes), BF16)


def _nsa_sweep(i, q_ref, gt_ref, kc_ref, vct_ref, cmapt_ref, eye_ref, cbias_ref, tri_ref, wtri_ref,
               ksa_ref, vst_ref, kwn_ref, vwt_ref, o_ref,
               qa_ref, m_ref, a_ref, acc_ref, s_buf, p_buf, sw_ref, ow_ref):
    part = slice(i * TQ, (i + 1) * TQ)
    parity = i % 2
    hg = HEADS_PER_GROUP
    lanes = hg * TQ
    head = lambda k: slice(k * TQ, (k + 1) * TQ)
    qa = qa_ref.at[i % 2]
    ow = ow_ref.at[i % 2]

    def sel_scores(c):
        return jnp.dot(ksa_ref[c * TK:(c + 1) * TK, :], qa[...], preferred_element_type=F32)

    def softmax_update(scores):
        m_old = m_ref[...]
        m_new = jnp.maximum(m_old, jnp.max(scores(), axis=0, keepdims=True))
        m_ref[...] = m_new
        return jnp.exp2(m_old - m_new), jnp.exp2(scores() - m_new).astype(BF16)

    m_ref[...] = jnp.full(m_ref.shape, NEG, F32)
    acc_ref[...] = jnp.zeros(acc_ref.shape, F32)
    s_buf[0] = sel_scores(0)
    yield

    def pv_prev(c):
        if c == 0:
            return acc_ref[...]
        pv = jnp.dot(vst_ref[c - 1], p_buf[(c - 1) % 2], preferred_element_type=F32)
        return a_ref[...] * acc_ref[...] + pv

    for c in range(i):
        cur = c % 2
        if c >= 1:
            acc_ref[...] = pv_prev(c)
        s_buf[1 - cur] = sel_scores(c + 1)
        yield
        alpha, p = softmax_update(lambda: s_buf[cur])
        p_buf[cur] = p
        a_ref[...] = alpha

    acc_prev = pv_prev(i)
    yield
    alpha, p = softmax_update(lambda: s_buf[parity] + jnp.concatenate([tri_ref[...]] * hg, axis=1))
    acc_sel = alpha * acc_prev + jnp.dot(vst_ref[i], p, preferred_element_type=F32)
    yield
    o_sel = acc_sel[0:HEAD_DIM, :] / jnp.maximum(acc_sel[HEAD_DIM:HEAD_DIM + 1, :], 1e-30)

    gt = gt_ref[:, part]
    parts = []
    for k in range(hg):
        parts.append(gt[3 * k:3 * k + 1, :] * ow[0:HEAD_DIM, head(k)]
                     + gt[3 * k + 1:3 * k + 2, :] * o_sel[:, head(k)]
                     + gt[3 * k + 2:3 * k + 3, :] * ow[HEAD_DIM:2 * HEAD_DIM, head(k)])
    o_t = jnp.concatenate(parts, axis=0).astype(BF16)
    o_ref[part, :] = lax.dot_general(eye_ref[...], o_t, _NT, preferred_element_type=F32).astype(BF16)


def _nsa(q_t, gates_t, kc, vct, cmapt, eye, cbias, tri, wtri, ksa, kwn, v_t):
    bsz, d, s = q_t.shape
    ng = N_KV_GROUPS
    gw = HEADS_PER_GROUP * HEAD_DIM
    lanes = HEADS_PER_GROUP * TQ
    nt = s // TK
    assert TQ == TK and WINDOW % TK == 0 and gw == TQ and s == SEQ
    per_bg = lambda n, w: pl.BlockSpec((None, None, n, w), lambda b, g: (b, g, 0, 0))
    vt = lambda k: pl.BlockSpec((None, None, None, nt, V_ROWS, TK), lambda b, g: (k, b, g, 0, 0, 0))
    const = lambda shape: pl.BlockSpec(shape, lambda b, g: (0,) * len(shape))
    return pl.pallas_call(
        _nsa_kernel,
        grid=(bsz, ng),
        in_specs=[pl.BlockSpec((None, gw, s), lambda b, g: (b, g, 0)),
                  per_bg(GATE_ROWS, s),
                  per_bg(N_CMP_PAD, HEAD_DIM), per_bg(HEAD_DIM, N_CMP_PAD),
                  const(cmapt.shape), const(eye.shape), const(cbias.shape), const(tri.shape),
                  const(wtri.shape),
                  per_bg(s, K_AUG), vt(0), per_bg(s, HEAD_DIM), vt(1)],
        out_specs=pl.BlockSpec((None, s, gw), lambda b, g: (b, 0, g)),
        out_shape=jax.ShapeDtypeStruct((bsz, s, d), BF16),
        scratch_shapes=[pltpu.VMEM((2, K_AUG, lanes), BF16),
                        pltpu.VMEM((1, lanes), F32),
                        pltpu.VMEM((1, lanes), F32),
                        pltpu.VMEM((V_ROWS, lanes), F32),
                        pltpu.VMEM((2, TK, lanes), F32),
                        pltpu.VMEM((2, TK, lanes), BF16),
                        pltpu.VMEM((WINDOW // TK + 1, TK, lanes), F32),
                        pltpu.VMEM((2, 2 * HEAD_DIM, lanes), F32)],
        compiler_params=_cparams(("parallel", "parallel")),
        name="nsa",
    )(q_t, gates_t, kc, vct, cmapt, eye, cbias, tri, wtri, ksa, v_t, kwn, v_t)


def _cmp_to_sel_t():
    c0 = np.arange(N_CMP_PAD)[None, :] * CMP_STRIDE
    s0 = np.arange(N_SELB)[:, None] * SEL_BLOCK
    ov = np.minimum(c0 + CMP_BLOCK, s0 + SEL_BLOCK) - np.maximum(c0, s0)
    m = (np.clip(ov, 0, None) / CMP_BLOCK).astype(np.float32)
    m[:, N_CMP:] = 0.0
    return m


def _cmp_bias():
    n_end = np.arange(N_CMP_PAD)[None, :, None] * CMP_STRIDE + (CMP_BLOCK - 1)
    t = np.arange(SEQ // TQ)[:, None, None] * TQ + np.arange(TQ)[None, None, :]
    return np.where(n_end <= t, 0.0, NEG).astype(np.float32)


def _tri_bias():
    r = np.arange(TK)[:, None]
    c = np.arange(TQ)[None, :]
    return np.where(r <= c, 0.0, NEG).astype(np.float32)


def _window_bias():
    n_back = WINDOW // TK
    r = np.arange(WINDOW + TQ)[:, None]
    c = np.arange(TQ)[None, :]
    tabs = [np.where(r <= i * TQ + c, 0.0, NEG) for i in range(n_back)]
    tabs.append(np.where((r > c) & (r <= c + WINDOW), 0.0, NEG))
    return np.stack(tabs).astype(np.float32)


def kernel(x, c, norm_gain, w_ada, b_ada, w_a_in, conv_w, w_a_out, w_qg, q_gain, w_o,
           kv_norm_gain, w_ada_kv, b_ada_kv, w_kv, k_gain, cmp_pe, cmp_w1, cmp_w2,
           w_mlp1, w_mlp2):
    bsz, s, d = x.shape
    assert (s, d) == (SEQ, D_MODEL)
    m = bsz * s
    x2d = x.reshape(m, d)

    def split_mod(mod, n):
        return [mod[:, k * d:(k + 1) * d].reshape(bsz, 1, d) for k in range(n)]

    sh1, sc1, g1, sh2, sc2, g2 = split_mod(_ada(c, w_ada, b_ada, 0), 6)
    x2d = _short_conv(x2d, norm_gain[0, 0].reshape(1, d), sh1, sc1, w_a_in[0].astype(BF16),
                      conv_w[0], w_a_out[0].astype(BF16), g1)
    x2d = _mlp(x2d, norm_gain[0, 1].reshape(1, d), sh2, sc2, g2,
               w_mlp1[0].astype(BF16), w_mlp2[0].astype(BF16))

    sh1, sc1, g1, sh2, sc2, g2 = split_mod(_ada(c, w_ada, b_ada, 1), 6)
    sh_kv, sc_kv = split_mod(_ada(c, w_ada_kv[None], b_ada_kv[None], 0), 2)
    nq = N_HEADS * HEAD_DIM
    wq_t = w_qg[0][:, :nq].T.astype(BF16)
    per_g = HEADS_PER_GROUP * N_BRANCH
    wg = w_qg[0][:, nq:].reshape(d, N_KV_GROUPS, per_g)
    wg = jnp.pad(wg, ((0, 0), (0, 0), (0, GATE_ROWS - per_g))).reshape(d, N_KV_GROUPS * GATE_ROWS)
    qg_b = jnp.broadcast_to(q_gain[0].reshape(HEAD_DIM, 1), (HEAD_DIM, 128))
    gcols = N_KV_GROUPS * HEAD_DIM
    stream = lambda r: w_kv[:, r * gcols:(r + 1) * gcols]
    wkk = jnp.concatenate([stream(0), stream(1), stream(2), stream(4)], axis=1).astype(BF16)
    wvt = jnp.concatenate([stream(3), stream(5)], axis=1).T.astype(BF16)
    q_t, gates_t, kv4, v_t = _qkv_proj(x2d.reshape(bsz, s, d), norm_gain[1, 0].reshape(1, d), sh1, sc1,
                                       kv_norm_gain.reshape(1, d), sh_kv, sc_kv, wq_t,
                                       wg.T.astype(BF16), wkk, wvt, qg_b)
    eye = jnp.eye(TQ, dtype=BF16)
    pe8 = jnp.broadcast_to(cmp_pe.reshape(2, 1, CMP_BLOCK * HEAD_DIM), (2, 8, CMP_BLOCK * HEAD_DIM))
    kc, vct, ksa, kwn = _kv_prep(kv4, pe8.astype(BF16), cmp_w1.astype(BF16),
                                 cmp_w2.astype(BF16), k_gain, eye)
    attn = _nsa(q_t, gates_t, kc, vct, jnp.asarray(_cmp_to_sel_t()), eye,
                jnp.asarray(_cmp_bias()), jnp.asarray(_tri_bias()),
                jnp.asarray(_window_bias()), ksa, kwn, v_t)
    x2d = _mlp(x2d, norm_gain[1, 1].reshape(1, d), sh2, sc2, g2,
               w_mlp1[1].astype(BF16), w_mlp2[1].astype(BF16),
               mixer_out=(attn.reshape(m, d), w_o[0].astype(BF16), g1))
    return x2d.reshape(bsz, s, d)
```
